```python
import math
import jax, jax.numpy as jnp
from jax import lax
import numpy as np

D_MODEL = 1024
BATCH = 32
SEQ = 2048
DEPTH = 1

PLE_DIM = 256
EPS = 1e-6
NEG = -1e30

HEAD_DIM_A = 64
A_WIDTH = D_MODEL // 2
N_HEADS_A = A_WIDTH // HEAD_DIM_A
DILATED_BRANCHES = ((128, 1), (512, 4), (2048, 16))
BAND_BLOCK = 128

REL_BUCKETS = 32
REL_MAX_DISTANCE = 2048

M_WIDTH = D_MODEL - A_WIDTH
V_DIM = 64
N_HEADS_M = M_WIDTH // V_DIM
Q_LORA = 384
KV_LORA = 256
NOPE_DIM = 64
ROPE_DIM = 32
ROPE_THETA = 10000.0
QUERY_BLOCK = 128

IN_COLS = 3 * A_WIDTH + Q_LORA + KV_LORA + ROPE_DIM

N_GROUPS = 4
EXPERTS_PER_GROUP = 8
N_EXPERTS = N_GROUPS * EXPERTS_PER_GROUP
TOP_K_IN_GROUP = 2
EXPERT_FF = 512
MOE_BLOCK = 256

kernel_name = "hybrid_dilated_mla_hiermoe_layer"


def rms_norm(x, gain):
    xf = x.astype(jnp.float32)
    y = xf * lax.rsqrt(jnp.mean(xf * xf, axis=-1, keepdims=True) + EPS)
    return (y * gain.astype(jnp.float32)).astype(x.dtype)


def rope(x, pos):
    half = ROPE_DIM // 2
    inv = 1.0 / (ROPE_THETA ** (jnp.arange(half, dtype=jnp.float32) * 2.0 / ROPE_DIM))
    ang = pos.astype(jnp.float32)[:, None] * inv[None, :]
    cos = jnp.cos(ang)[:, None, :]
    sin = jnp.sin(ang)[:, None, :]
    xf = x.astype(jnp.float32)
    x1, x2 = xf[..., :half], xf[..., half:]
    return jnp.concatenate([x1 * cos - x2 * sin, x2 * cos + x1 * sin], axis=-1).astype(x.dtype)


def t5_bucket(dist):
    max_exact = REL_BUCKETS // 2
    n = jnp.maximum(dist, 0)
    nf = jnp.maximum(n, 1).astype(jnp.float32)
    large = max_exact + (jnp.log(nf / max_exact) / math.log(REL_MAX_DISTANCE / max_exact)
                         * (REL_BUCKETS - max_exact)).astype(jnp.int32)
    large = jnp.minimum(large, REL_BUCKETS - 1)
    return jnp.where(n < max_exact, n, large)


def dilated_branch(q, k, v, rel_bias, window, dilation):
    B, S, H, Dh = q.shape
    L = S // dilation
    nb = -(-L // BAND_BLOCK)
    Lp = nb * BAND_BLOCK
    band = window // dilation
    BB = BAND_BLOCK

    def to_sub(t):
        t = t.reshape(B, L, dilation, H, Dh).transpose(0, 3, 2, 1, 4)
        t = jnp.pad(t, ((0, 0), (0, 0), (0, 0), (0, Lp - L), (0, 0)))
        return t.reshape(B, H, dilation, nb, BB, Dh)

    def with_prev(t):
        prev = jnp.pad(t, ((0, 0), (0, 0), (0, 0), (1, 0), (0, 0), (0, 0)))[:, :, :, :-1]
        return jnp.concatenate([prev, t], axis=4)

    qs = to_sub(q)
    kk = with_prev(to_sub(k))
    vv = with_prev(to_sub(v)).astype(jnp.float32)

    s = jnp.einsum('bhrnid,bhrnjd->bhrnij', qs, kk).astype(jnp.float32) * (Dh ** -0.5)
    i = jnp.arange(BB)[:, None]
    j = jnp.arange(2 * BB)[None, :]
    delta = BB + i - j
    bias = rel_bias.astype(jnp.float32)[t5_bucket(delta * dilation)]
    bias = bias.transpose(2, 0, 1)[:, None, None]
    band_ok = (delta >= 0) & (delta <= band)
    key_ok = (jnp.arange(nb)[:, None, None] > 0) | (j[None] >= BB)
    mask = band_ok[None] & key_ok
    s = jnp.where(mask, s + bias, NEG)
    m = jnp.max(s, axis=-1, keepdims=True)
    e = jnp.exp(s - m)
    den = jnp.sum(e, axis=-1)
    o = jnp.einsum('bhrnij,bhrnjd->bhrnid', e, vv) / den[..., None]
    lse = m[..., 0] + jnp.log(den)

    o = o.reshape(B, H, dilation, Lp, Dh)[:, :, :, :L].transpose(0, 3, 2, 1, 4).reshape(B, S, H, Dh)
    lse = lse.reshape(B, H, dilation, Lp)[:, :, :, :L].transpose(0, 3, 2, 1).reshape(B, S, H)
    return o, lse


def dilated_attention(q, k, v, rel_bias):
    outs, lses = [], []
    for window, dilation in DILATED_BRANCHES:
        o, lse = dilated_branch(q, k, v, rel_bias, window, dilation)
        outs.append(o)
        lses.append(lse)
    w = jax.nn.softmax(jnp.stack(lses), axis=0)
    o = jnp.sum(w[..., None] * jnp.stack(outs), axis=0)
    B, S, H, Dh = o.shape
    return o.reshape(B, S, H * Dh)


def mla(cq, ckv, kr, q_a_gain, w_q_up, kv_a_gain, w_kv_up,
        qn_nope_gain, qn_rope_gain, kn_nope_gain, kn_rope_gain, pos):
    B, S, _ = cq.shape
    q = (rms_norm(cq, q_a_gain) @ w_q_up).reshape(B, S, N_HEADS_M, NOPE_DIM + ROPE_DIM)
    kv = (rms_norm(ckv, kv_a_gain) @ w_kv_up).reshape(B, S, N_HEADS_M, NOPE_DIM + V_DIM)
    k_nope, v = kv[..., :NOPE_DIM], kv[..., NOPE_DIM:]
    q_nope = rms_norm(q[..., :NOPE_DIM], qn_nope_gain)
    q_rope = rope(rms_norm(q[..., NOPE_DIM:], qn_rope_gain), pos)
    k_nope = rms_norm(k_nope, kn_nope_gain)
    k_rope = rope(rms_norm(kr, kn_rope_gain)[:, :, None, :], pos)
    qh = jnp.concatenate([q_nope, q_rope], axis=-1).transpose(0, 2, 1, 3)
    kh = jnp.concatenate([k_nope, jnp.broadcast_to(k_rope, (B, S, N_HEADS_M, ROPE_DIM))],
                         axis=-1).transpose(0, 2, 1, 3)
    vh = v.transpose(0, 2, 1, 3).astype(jnp.float32)
    scale = (NOPE_DIM + ROPE_DIM) ** -0.5
    nq = S // QUERY_BLOCK
    qb = qh.reshape(B, N_HEADS_M, nq, QUERY_BLOCK, NOPE_DIM + ROPE_DIM).transpose(2, 0, 1, 3, 4)
    kpos = jnp.arange(S)

    def block(args):
        qi, bi = args
        s = jnp.einsum('bhid,bhjd->bhij', qi, kh).astype(jnp.float32) * scale
        qpos = bi * QUERY_BLOCK + jnp.arange(QUERY_BLOCK)
        s = jnp.where(kpos[None, :] <= qpos[:, None], s, NEG)
        pr = jax.nn.softmax(s, axis=-1)
        return jnp.einsum('bhij,bhjd->bhid', pr, vh)

    o = lax.map(block, (qb, jnp.arange(nq)))
    return o.transpose(1, 0, 3, 2, 4).reshape(B, S, N_HEADS_M * V_DIM)


def hier_moe(xn, w_rg, b_rg, w_re, b_re, w_gate, w_up, w_down):
    B, S, D = xn.shape
    N = B * S
    K = TOP_K_IN_GROUP
    xt = xn.reshape(N, D)
    gprob = jax.nn.softmax((xt @ w_rg + b_rg).astype(jnp.float32), axis=-1)
    g_gate, g_idx = lax.top_k(gprob, 1)
    elog = (xt @ w_re + b_re).astype(jnp.float32).reshape(N, N_GROUPS, EXPERTS_PER_GROUP)
    elog = jnp.take_along_axis(elog, g_idx[:, :, None], axis=1)[:, 0]
    e_w, e_idx = lax.top_k(jax.nn.softmax(elog, axis=-1), K)
    e_w = e_w / jnp.sum(e_w, axis=-1, keepdims=True)
    weights = (g_gate * e_w).reshape(-1)
    flat_e = (g_idx * EXPERTS_PER_GROUP + e_idx).reshape(-1).astype(jnp.int32)
    flat_t = jnp.repeat(jnp.arange(N, dtype=jnp.int32), K)
    order = jnp.argsort(flat_e)
    sorted_e = flat_e[order]
    counts = jax.ops.segment_sum(jnp.ones_like(flat_e), flat_e, num_segments=N_EXPERTS)
    pcounts = (counts + MOE_BLOCK - 1) // MOE_BLOCK * MOE_BLOCK
    pend = jnp.cumsum(pcounts)
    pstart = pend - pcounts
    start = jnp.cumsum(counts) - counts
    dest = pstart[sorted_e] + (jnp.arange(N * K, dtype=jnp.int32) - start[sorted_e])
    R = N * K + N_EXPERTS * MOE_BLOCK
    row_tok = jnp.full((R,), N, jnp.int32).at[dest].set(flat_t[order])
    row_w = jnp.zeros((R,), jnp.float32).at[dest].set(weights[order])
    x_pad = jnp.concatenate([xt, jnp.zeros((1, D), xt.dtype)], axis=0)
    xr = x_pad[row_tok].reshape(R // MOE_BLOCK, MOE_BLOCK, D)
    blk_e = jnp.minimum(jnp.searchsorted(pend, jnp.arange(R // MOE_BLOCK) * MOE_BLOCK, side='right'),
                        N_EXPERTS - 1)

    def expert_block(args):
        xb, e = args
        hdn = jax.nn.silu(xb @ w_gate[e]) * (xb @ w_up[e])
        return hdn @ w_down[e]

    y = lax.map(expert_block, (xr, blk_e)).reshape(R, D)
    out = jax.ops.segment_sum(y.astype(jnp.float32) * row_w[:, None], row_tok, num_segments=N + 1)[:N]
    return out.reshape(B, S, D).astype(xn.dtype)


def setup_inputs(seed: int = 0) -> dict:
    key = jax.random.key(seed)
    ks = jax.random.split(key, 32)
    f32 = jnp.float32
    L = DEPTH
    D = D_MODEL

    def w(k, shape, fan_in):
        return jax.random.normal(k, shape, f32) * (fan_in ** -0.5)

    def gain(k, shape):
        return 1.0 + 0.05 * jax.random.normal(k, shape, f32)

    return {
        "x": jax.random.normal(ks[0], (BATCH, SEQ, D), f32),
        "p": jax.random.normal(ks[1], (DEPTH, BATCH, SEQ, PLE_DIM), f32),
        "rel_bias": 0.5 * jax.random.normal(ks[2], (REL_BUCKETS, N_HEADS_A), f32),
        "norm_mix_gain": gain(ks[3], (L, D)),
        "w_in": w(ks[4], (L, D, IN_COLS), D),
        "qn_a_gain": gain(ks[5], (L, HEAD_DIM_A)),
        "kn_a_gain": gain(ks[6], (L, HEAD_DIM_A)),
        "q_a_gain": gain(ks[7], (L, Q_LORA)),
        "w_q_up": w(ks[8], (L, Q_LORA, N_HEADS_M * (NOPE_DIM + ROPE_DIM)), Q_LORA),
        "kv_a_gain": gain(ks[9], (L, KV_LORA)),
        "w_kv_up": w(ks[10], (L, KV_LORA, N_HEADS_M * (NOPE_DIM + V_DIM)), KV_LORA),
        "qn_nope_gain": gain(ks[11], (L, NOPE_DIM)),
        "qn_rope_gain": gain(ks[12], (L, ROPE_DIM)),
        "kn_nope_gain": gain(ks[13], (L, NOPE_DIM)),
        "kn_rope_gain": gain(ks[14], (L, ROPE_DIM)),
        "w_out": w(ks[15], (L, A_WIDTH + M_WIDTH, D), A_WIDTH + M_WIDTH),
        "norm_ffn_gain": gain(ks[16], (L, D)),
        "w_router_group": w(ks[17], (L, D, N_GROUPS), D),
        "b_router_group": 0.01 * jax.random.normal(ks[18], (L, N_GROUPS), f32),
        "w_router_expert": w(ks[19], (L, D, N_EXPERTS), D),
        "b_router_expert": 0.01 * jax.random.normal(ks[20], (L, N_EXPERTS), f32),
        "w_exp_gate": w(ks[21], (L, N_EXPERTS, D, EXPERT_FF), D),
        "w_exp_up": w(ks[22], (L, N_EXPERTS, D, EXPERT_FF), D),
        "w_exp_down": w(ks[23], (L, N_EXPERTS, EXPERT_FF, D), EXPERT_FF),
        "w_ple_proj": w(ks[24], (L, PLE_DIM, D), PLE_DIM),
        "ple_norm_gain": gain(ks[25], (L, D)),
        "w_ple_gate": w(ks[26], (L, D, D), D),
        "b_ple_gate": 0.1 * jax.random.normal(ks[27], (L, D), f32),
    }


def reference(x, p, rel_bias, norm_mix_gain, w_in, qn_a_gain, kn_a_gain, q_a_gain, w_q_up,
              kv_a_gain, w_kv_up, qn_nope_gain, qn_rope_gain, kn_nope_gain, kn_rope_gain,
              w_out, norm_ffn_gain, w_router_group, b_router_group, w_router_expert,
              b_router_expert, w_exp_gate, w_exp_up, w_exp_down, w_ple_proj, ple_norm_gain,
              w_ple_gate, b_ple_gate):
    B, S, D = x.shape
    pos = jnp.arange(S, dtype=jnp.int32)
    h = x
    c0 = 0
    c1 = A_WIDTH
    c2 = 2 * A_WIDTH
    c3 = 3 * A_WIDTH
    c4 = c3 + Q_LORA
    c5 = c4 + KV_LORA
    for i in range(DEPTH):
        xn = rms_norm(h, norm_mix_gain[i])
        proj = xn @ w_in[i]
        qa = rms_norm(proj[..., c0:c1].reshape(B, S, N_HEADS_A, HEAD_DIM_A), qn_a_gain[i])
        ka = rms_norm(proj[..., c1:c2].reshape(B, S, N_HEADS_A, HEAD_DIM_A), kn_a_gain[i])
        va = proj[..., c2:c3].reshape(B, S, N_HEADS_A, HEAD_DIM_A)
        o_a = dilated_attention(qa, ka, va, rel_bias)
        o_m = mla(proj[..., c3:c4], proj[..., c4:c5], proj[..., c5:], q_a_gain[i], w_q_up[i],
                  kv_a_gain[i], w_kv_up[i], qn_nope_gain[i], qn_rope_gain[i],
                  kn_nope_gain[i], kn_rope_gain[i], pos)
        mix = jnp.concatenate([o_a, o_m], axis=-1).astype(h.dtype)
        h = h + mix @ w_out[i]
        h = h + hier_moe(rms_norm(h, norm_ffn_gain[i]), w_router_group[i], b_router_group[i],
                         w_router_expert[i], b_router_expert[i], w_exp_gate[i], w_exp_up[i],
                         w_exp_down[i])
        e = rms_norm(p[i] @ w_ple_proj[i], ple_norm_gain[i]).astype(jnp.float32)
        g = jax.nn.sigmoid((h @ w_ple_gate[i] + b_ple_gate[i]).astype(jnp.float32))
        h = h + (g * e).astype(h.dtype)
    return h
```

```python
import functools
import math

import jax
import jax.numpy as jnp
import numpy as np
from jax import lax
from jax.experimental import pallas as pl
from jax.experimental.pallas import tpu as pltpu

F32 = jnp.float32
BF16 = jnp.bfloat16

D_MODEL = 1024
SEQ = 2048
PLE_DIM = 256
EPS = 1e-6
NEG = -1e30
HEAD_DIM_A = 64
A_WIDTH = 512
N_HEADS_A = 8
DILATED_BRANCHES = ((128, 1), (512, 4), (2048, 16))
BAND_BLOCK = 128
REL_BUCKETS = 32
REL_MAX_DISTANCE = 2048
M_WIDTH = 512
V_DIM = 64
N_HEADS_M = 8
Q_LORA = 384
KV_LORA = 256
NOPE_DIM = 64
ROPE_DIM = 32
ROPE_THETA = 10000.0
N_GROUPS = 4
EXPERTS_PER_GROUP = 8
N_EXPERTS = 32
TOP_K = 2
EXPERT_FF = 512

LANES = 128
N_RES = 16
M_SUB = SEQ // N_RES
RES_PER_STEP = 4
TM = RES_PER_STEP * M_SUB
N_RES_STEPS = N_RES // RES_PER_STEP
HEAD_SLOT = LANES
MLA_TQ = 256
MLA_TK = 256
MOE_TM = 512
VMEM_LIMIT = 48 * 1024 * 1024

IN_COLS_PAD = 3 * A_WIDTH + Q_LORA + KV_LORA + LANES
C_Q0, C_K0, C_V0, C_CQ0, C_CKV0, C_KR0 = 0, 512, 1024, 1536, 1920, 2176


def _cparams(sem):
    return pltpu.CompilerParams(dimension_semantics=sem, vmem_limit_bytes=VMEM_LIMIT)


def _rows(ref, width):
    return jnp.concatenate([ref[0, :, j * width:(j + 1) * width] for j in range(RES_PER_STEP)], axis=0)


def _store_rows(ref, val, width, col0=0, ncols=None):
    ncols = width if ncols is None else ncols
    for j in range(RES_PER_STEP):
        ref[0, :, j * width + col0:j * width + col0 + ncols] = val[j * M_SUB:(j + 1) * M_SUB]


def _rms(x, n):
    return x * lax.rsqrt(jnp.sum(x * x, axis=-1, keepdims=True) * (1.0 / n) + EPS)


def _inproj_kernel(x_ref, g_ref, w_ref, gsum_ref, gq_ref, gk_ref, gcq_ref, gckv_ref,
                   qa_ref, ka_ref, va_ref, cq_ref, ckv_ref, kr_ref):
    x = _rows(x_ref, D_MODEL)
    xn = (_rms(x, D_MODEL) * g_ref[...]).astype(BF16)

    def proj(c0, c1):
        return jnp.dot(xn, w_ref[:, c0:c1], preferred_element_type=F32)

    def head_norm(t, gain_ref):
        ss = jnp.dot((t * t).astype(BF16), gsum_ref[...], preferred_element_type=F32)
        return t * lax.rsqrt(ss * (1.0 / HEAD_DIM_A) + EPS) * gain_ref[...]

    def put(ref, val):
        ref[0] = val.astype(ref.dtype).reshape(RES_PER_STEP, M_SUB, val.shape[-1])

    put(qa_ref, head_norm(proj(C_Q0, C_K0), gq_ref))
    put(ka_ref, head_norm(proj(C_K0, C_V0), gk_ref))
    put(va_ref, proj(C_V0, C_CQ0))
    put(cq_ref, _rms(proj(C_CQ0, C_CKV0), Q_LORA) * gcq_ref[...])
    put(ckv_ref, _rms(proj(C_CKV0, C_KR0), KV_LORA) * gckv_ref[...])
    put(kr_ref, proj(C_KR0, IN_COLS_PAD))


def _inproj(xv, g, w_in_p, gsum, gq, gk, gcq, gckv):
    B = xv.shape[0]
    full = lambda a: pl.BlockSpec(a.shape, lambda b, j: (0,) * a.ndim)
    out = lambda c, dt: (jax.ShapeDtypeStruct((B, N_RES, M_SUB, c), dt),
                         pl.BlockSpec((1, RES_PER_STEP, M_SUB, c), lambda b, j: (b, j, 0, 0)))
    outs = [out(A_WIDTH, BF16), out(A_WIDTH, BF16), out(A_WIDTH, BF16),
            out(Q_LORA, BF16), out(KV_LORA, BF16), out(LANES, F32)]
    return pl.pallas_call(
        _inproj_kernel,
        grid=(B, N_RES_STEPS),
        in_specs=[pl.BlockSpec((1, M_SUB, RES_PER_STEP * D_MODEL), lambda b, j: (b, 0, j)),
                  full(g), full(w_in_p), full(gsum), full(gq), full(gk), full(gcq), full(gckv)],
        out_specs=[o[1] for o in outs],
        out_shape=[o[0] for o in outs],
        compiler_params=_cparams(("parallel", "parallel")),
        name="inproj",
    )(xv, g, w_in_p, gsum, gq, gk, gcq, gckv)


def _mla_prep_kernel(cq_ref, ckv_ref, kr_ref, wq_ref, wkv_ref, gsq_ref, gsk_ref, icq_ref,
                     gq_ref, gk_ref, gkr_ref, cos_ref, sa_ref, sb_ref, q_ref, k_ref, v_ref):
    cq = cq_ref[0].reshape(TM, Q_LORA)
    ckv = ckv_ref[0].reshape(TM, KV_LORA)
    kr = kr_ref[0].reshape(TM, LANES)
    cos = cos_ref[...].reshape(TM, LANES)
    sa = sa_ref[...].reshape(TM, LANES)
    sb = sb_ref[...].reshape(TM, LANES)

    def rope(t):
        return t * cos + pltpu.roll(t, 16, 1) * sa + pltpu.roll(t, LANES - 16, 1) * sb

    q = jnp.dot(cq, wq_ref[...], preferred_element_type=F32)
    kv = jnp.dot(ckv, wkv_ref[...], preferred_element_type=F32)
    k_rope = rope(_rms(kr, ROPE_DIM) * gkr_ref[...])
    for h in range(N_HEADS_M):
        sl = slice(h * HEAD_SLOT, (h + 1) * HEAD_SLOT)
        qh = q[:, sl]
        ss = jnp.dot((qh * qh).astype(BF16), gsq_ref[...], preferred_element_type=F32)
        qn = qh * lax.rsqrt(ss * icq_ref[...] + EPS) * gq_ref[...]
        _store_rows(q_ref, rope(qn).astype(BF16), N_HEADS_M * HEAD_SLOT, h * HEAD_SLOT, HEAD_SLOT)
        kh = kv[:, sl]
        ssk = jnp.dot((kh * kh).astype(BF16), gsk_ref[...], preferred_element_type=F32)
        kn = kh * lax.rsqrt(ssk * (1.0 / NOPE_DIM) + EPS) * gk_ref[...] + k_rope
        _store_rows(k_ref, kn.astype(BF16), N_HEADS_M * HEAD_SLOT, h * HEAD_SLOT, HEAD_SLOT)
    _store_rows(v_ref, kv[:, N_HEADS_M * HEAD_SLOT:].astype(BF16), M_WIDTH)


def _mla_prep(cq, ckv, kr, wq_p, wkv_p, gsq, gsk, icq, gq, gk, gkr, cos, sa, sb):
    B = cq.shape[0]
    full = lambda a: pl.BlockSpec(a.shape, lambda b, j: (0,) * a.ndim)
    rowin = lambda c: pl.BlockSpec((1, RES_PER_STEP, M_SUB, c), lambda b, j: (b, j, 0, 0))
    tab = pl.BlockSpec((RES_PER_STEP, M_SUB, LANES), lambda b, j: (j, 0, 0))
    view = lambda c: (jax.ShapeDtypeStruct((B, M_SUB, N_RES * c), BF16),
                      pl.BlockSpec((1, M_SUB, RES_PER_STEP * c), lambda b, j: (b, 0, j)))
    outs = [view(N_HEADS_M * HEAD_SLOT), view(N_HEADS_M * HEAD_SLOT), view(M_WIDTH)]
    return pl.pallas_call(
        _mla_prep_kernel,
        grid=(B, N_RES_STEPS),
        in_specs=[rowin(Q_LORA), rowin(KV_LORA), rowin(LANES), full(wq_p), full(wkv_p), full(gsq),
                  full(gsk), full(icq), full(gq), full(gk), full(gkr), tab, tab, tab],
        out_specs=[o[1] for o in outs],
        out_shape=[o[0] for o in outs],
        compiler_params=_cparams(("parallel", "parallel")),
        name="mla_prep",
    )(cq, ckv, kr, wq_p, wkv_p, gsq, gsk, icq, gq, gk, gkr, cos, sa, sb)


def _mla_attn_kernel(q_ref, k_ref, v_ref, o_ref):
    i = pl.program_id(2)
    row = lax.broadcasted_iota(jnp.int32, (MLA_TQ, MLA_TK), 0)
    col = lax.broadcasted_iota(jnp.int32, (MLA_TQ, MLA_TK), 1)
    lane = lax.broadcasted_iota(jnp.int32, (MLA_TQ, LANES), 1)
    outs = []
    for hh in range(2):
        hs = slice(hh * HEAD_SLOT, (hh + 1) * HEAD_SLOT)
        qh = q_ref[0, :, hs]

        def step(j, carry, diagonal, hs=hs, qh=qh):
            m, l, acc = carry
            ks = pl.ds(pl.multiple_of(j * MLA_TK, MLA_TK), MLA_TK)
            s = lax.dot_general(qh, k_ref[0, ks, hs], (((1,), (1,)), ((), ())),
                                preferred_element_type=F32)
            if diagonal:
                s = jnp.where(col <= row, s, NEG)
            m_new = jnp.maximum(m, jnp.max(s, axis=-1, keepdims=True))
            alpha = jnp.exp(m - m_new)
            p = jnp.exp(s - m_new)
            l = alpha * l + jnp.sum(p, axis=-1, keepdims=True)
            acc = alpha * acc + jnp.dot(p.astype(BF16), v_ref[0, ks, :], preferred_element_type=F32)
            return m_new, l, acc

        init = (jnp.full((MLA_TQ, 1), NEG, F32), jnp.zeros((MLA_TQ, 1), F32),
                jnp.zeros((MLA_TQ, LANES), F32))
        carry = lax.fori_loop(0, i, functools.partial(step, diagonal=False), init)
        _, l, acc = step(i, carry, True)
        outs.append(acc / l)
    o_ref[0] = jnp.where(lane < V_DIM, outs[0], outs[1]).astype(BF16)


def _mla_attn(q, k, v):
    B = q.shape[0]
    return pl.pallas_call(
        _mla_attn_kernel,
        grid=(B, N_HEADS_M // 2, SEQ // MLA_TQ),
        in_specs=[pl.BlockSpec((1, MLA_TQ, 2 * HEAD_SLOT), lambda b, h, i: (b, i, h)),
                  pl.BlockSpec((1, SEQ, 2 * HEAD_SLOT), lambda b, h, i: (b, 0, h)),
                  pl.BlockSpec((1, SEQ, 2 * V_DIM), lambda b, h, i: (b, 0, h))],
        out_specs=pl.BlockSpec((1, MLA_TQ, 2 * V_DIM), lambda b, h, i: (b, i, h)),
        out_shape=jax.ShapeDtypeStruct((B, SEQ, M_WIDTH), BF16),
        compiler_params=_cparams(("parallel", "parallel", "arbitrary")),
        name="mla_attn",
    )(q, k, v)


N_BLK = SEQ // BAND_BLOCK


def _dilated_kernel(q_ref, k_ref, v_ref, t1_ref, t2_ref, t3_ref, o_ref,
                    q1, k1, v1, q2, k2, v2, ob1, ls1, ob2, ls2, ob3, ls3):
    bb = BAND_BLOCK
    lane = lax.broadcasted_iota(jnp.int32, (bb, LANES), 1)
    lane_row = lax.broadcasted_iota(jnp.int32, (1, LANES), 1)
    head_mask = [(lane_row < HEAD_DIM_A).astype(BF16), (lane_row >= HEAD_DIM_A).astype(BF16)]

    for src, d1, d2 in ((q_ref, q1, q2), (k_ref, k1, k2), (v_ref, v1, v2)):
        for n2 in range(N_BLK // 2):
            pieces = [src[0, r, 16 * n2:16 * n2 + 16, :].astype(F32) for r in range(N_RES)]
            d1[2 * n2] = jnp.concatenate([p[0:8] for p in pieces], axis=0).astype(BF16)
            d1[2 * n2 + 1] = jnp.concatenate([p[8:16] for p in pieces], axis=0).astype(BF16)
        for n in range(4):
            for r4 in range(4):
                d2[n * 4 + r4] = jnp.concatenate(
                    [src[0, r4 + 4 * c, 32 * n:32 * n + 32, :] for c in range(4)], axis=0)

    def attend(qb, t_ref, cur, prev, o_dst, l_dst, idx):
        o_heads, l_heads = [], []
        for hh in range(2):
            qh = qb * head_mask[hh]
            dots = lambda kk: lax.dot_general(qh, kk, (((1,), (1,)), ((), ())),
                                              preferred_element_type=F32)
            s_c = dots(cur[0]) + t_ref[hh, :, bb:2 * bb]
            m = jnp.max(s_c, axis=-1, keepdims=True)
            if prev is not None:
                s_p = dots(prev[0]) + t_ref[hh, :, 0:bb]
                m = jnp.maximum(m, jnp.max(s_p, axis=-1, keepdims=True))
            e_c = jnp.exp(s_c - m)
            den = jnp.sum(e_c, axis=-1, keepdims=True)
            acc = jnp.dot(e_c.astype(BF16), cur[1], preferred_element_type=F32)
            if prev is not None:
                e_p = jnp.exp(s_p - m)
                den = den + jnp.sum(e_p, axis=-1, keepdims=True)
                acc = acc + jnp.dot(e_p.astype(BF16), prev[1], preferred_element_type=F32)
            o_heads.append(acc / den)
            l_heads.append(m + jnp.log(den))
        o_dst[idx] = jnp.where(lane < HEAD_DIM_A, o_heads[0], o_heads[1])
        l_dst[idx] = jnp.where(lane < HEAD_DIM_A, l_heads[0], l_heads[1])

    def run(qs, ks, vs, t_ref, o_dst, l_dst, n_first, prev_off):
        def first(idx, c):
            attend(qs[idx], t_ref, (ks[idx], vs[idx]), None, o_dst, l_dst, idx)
            return c

        def rest(idx, c):
            attend(qs[idx], t_ref, (ks[idx], vs[idx]), (ks[idx - prev_off], vs[idx - prev_off]),
                   o_dst, l_dst, idx)
            return c

        lax.fori_loop(0, n_first, first, 0)
        if n_first < N_BLK:
            lax.fori_loop(n_first, N_BLK, rest, 0)

    run(q1, k1, v1, t1_ref, ob1, ls1, 1, 1)
    run(q2, k2, v2, t2_ref, ob2, ls2, 4, 4)
    run(q_ref.at[0], k_ref.at[0], v_ref.at[0], t3_ref, ob3, ls3, N_BLK, 0)

    for r in range(N_RES):
        r4, c = r % 4, r // 4
        gather1 = lambda ref: jnp.concatenate([ref[n, 8 * r:8 * r + 8, :] for n in range(N_BLK)], axis=0)
        gather2 = lambda ref: jnp.concatenate(
            [ref[n * 4 + r4, 32 * c:32 * c + 32, :] for n in range(4)], axis=0)
        o_b = [gather1(ob1), gather2(ob2), ob3[r]]
        l_b = [gather1(ls1), gather2(ls2), ls3[r]]
        top = jnp.maximum(jnp.maximum(l_b[0], l_b[1]), l_b[2])
        w_b = [jnp.exp(l - top) for l in l_b]
        num = w_b[0] * o_b[0] + w_b[1] * o_b[1] + w_b[2] * o_b[2]
        o_ref[0, r] = (num / (w_b[0] + w_b[1] + w_b[2])).astype(BF16)


def _dilated(qa, ka, va, t1, t2, t3):
    B = qa.shape[0]
    blk = pl.BlockSpec((1, N_RES, M_SUB, LANES), lambda b, h: (b, 0, 0, h))
    tab = lambda t: pl.BlockSpec((2,) + t.shape[1:], lambda b, h: (h, 0, 0))
    blocked_bf16 = pltpu.VMEM((N_BLK, BAND_BLOCK, LANES), BF16)
    blocked_f32 = pltpu.VMEM((N_BLK, BAND_BLOCK, LANES), F32)
    return pl.pallas_call(
        _dilated_kernel,
        grid=(B, N_HEADS_A // 2),
        in_specs=[blk, blk, blk, tab(t1), tab(t2), tab(t3)],
        out_specs=blk,
        out_shape=jax.ShapeDtypeStruct((B, N_RES, M_SUB, A_WIDTH), BF16),
        scratch_shapes=[blocked_bf16] * 6 + [blocked_f32] * 6,
        compiler_params=_cparams(("parallel", "parallel")),
        name="dilated_attn",
    )(qa, ka, va, t1, t2, t3)


def _split_bf16(t):
    hi = t.astype(BF16)
    return hi, (t - hi.astype(F32)).astype(BF16)


def _outproj_kernel(oa_ref, om_ref, x_ref, wo_ref, g_ref, wrh_ref, wrl_ref, br_ref,
                    h_ref, xn_ref, lg_ref):
    oa = oa_ref[0].reshape(TM, A_WIDTH)
    om = _rows(om_ref, M_WIDTH)
    x = _rows(x_ref, D_MODEL)
    h = (x + jnp.dot(oa, wo_ref[0:A_WIDTH, :], preferred_element_type=F32)
         + jnp.dot(om, wo_ref[A_WIDTH:, :], preferred_element_type=F32))
    h_ref[0] = h.reshape(RES_PER_STEP, M_SUB, D_MODEL)
    xn = _rms(h, D_MODEL) * g_ref[...]
    xn_ref[0] = xn.astype(BF16).reshape(RES_PER_STEP, M_SUB, D_MODEL)
    hi, lo = _split_bf16(xn)
    lg = (jnp.dot(hi, wrh_ref[...], preferred_element_type=F32)
          + jnp.dot(lo, wrh_ref[...], preferred_element_type=F32)
          + jnp.dot(hi, wrl_ref[...], preferred_element_type=F32)) + br_ref[...]
    lg_ref[0] = lg.reshape(RES_PER_STEP, M_SUB, LANES)


def _outproj(oa, omv, xv, wo, g, wrh, wrl, br):
    B = oa.shape[0]
    full = lambda a: pl.BlockSpec(a.shape, lambda b, j: (0,) * a.ndim)
    rows = lambda c: pl.BlockSpec((1, RES_PER_STEP, M_SUB, c), lambda b, j: (b, j, 0, 0))
    view = lambda c: pl.BlockSpec((1, M_SUB, RES_PER_STEP * c), lambda b, j: (b, 0, j))
    shp = lambda c, dt: jax.ShapeDtypeStruct((B, N_RES, M_SUB, c), dt)
    return pl.pallas_call(
        _outproj_kernel,
        grid=(B, N_RES_STEPS),
        in_specs=[rows(A_WIDTH), view(M_WIDTH), view(D_MODEL), full(wo), full(g), full(wrh),
                  full(wrl), full(br)],
        out_specs=[rows(D_MODEL), rows(D_MODEL), rows(LANES)],
        out_shape=[shp(D_MODEL, F32), shp(D_MODEL, BF16), shp(LANES, F32)],
        compiler_params=_cparams(("parallel", "parallel")),
        name="outproj_router",
    )(oa, omv, xv, wo, g, wrh, wrl, br)


def _expert_kernel(blk_e_ref, n_used_ref, x_ref, wg_ref, wu_ref, wd_ref, y_ref):
    @pl.when(pl.program_id(0) < n_used_ref[0])
    def _():
        x = x_ref[...]
        gate = jnp.dot(x, wg_ref[0], preferred_element_type=F32)
        up = jnp.dot(x, wu_ref[0], preferred_element_type=F32)
        hdn = (gate * jax.nn.sigmoid(gate) * up).astype(BF16)
        y_ref[...] = jnp.dot(hdn, wd_ref[0], preferred_element_type=F32).astype(y_ref.dtype)


def _experts(blk_e, n_used, xs, wg, wu, wd):
    n_blocks = xs.shape[0] // MOE_TM
    row = lambda i, be, nu: (jnp.minimum(i, nu[0] - 1), 0)
    wsel = lambda i, be, nu: (be[i], 0, 0)
    return pl.pallas_call(
        _expert_kernel,
        grid_spec=pltpu.PrefetchScalarGridSpec(
            num_scalar_prefetch=2,
            grid=(n_blocks,),
            in_specs=[pl.BlockSpec((MOE_TM, D_MODEL), row),
                      pl.BlockSpec((1, D_MODEL, EXPERT_FF), wsel),
                      pl.BlockSpec((1, D_MODEL, EXPERT_FF), wsel),
                      pl.BlockSpec((1, EXPERT_FF, D_MODEL), wsel)],
            out_specs=pl.BlockSpec((MOE_TM, D_MODEL), row)),
        out_shape=jax.ShapeDtypeStruct(xs.shape, BF16),
        compiler_params=_cparams(("arbitrary",)),
        name="expert_ffn",
    )(blk_e, n_used, xs, wg, wu, wd)


def _ple_kernel(h_ref, p_ref, wp_ref, gp_ref, wg_ref, bg_ref, o_ref):
    h = h_ref[0].reshape(TM, D_MODEL)
    p = _rows(p_ref, PLE_DIM).astype(BF16)
    e = _rms(jnp.dot(p, wp_ref[...], preferred_element_type=F32), D_MODEL) * gp_ref[...]
    g = jax.nn.sigmoid(jnp.dot(h.astype(BF16), wg_ref[...], preferred_element_type=F32) + bg_ref[...])
    _store_rows(o_ref, h + g * e, D_MODEL)


def _ple(h, pv, wp, gp, wg, bg):
    B = h.shape[0]
    full = lambda a: pl.BlockSpec(a.shape, lambda b, j: (0,) * a.ndim)
    view = lambda c: pl.BlockSpec((1, M_SUB, RES_PER_STEP * c), lambda b, j: (b, 0, j))
    return pl.pallas_call(
        _ple_kernel,
        grid=(B, N_RES_STEPS),
        in_specs=[pl.BlockSpec((1, RES_PER_STEP, M_SUB, D_MODEL), lambda b, j: (b, j, 0, 0)),
                  view(PLE_DIM), full(wp), full(gp), full(wg), full(bg)],
        out_specs=view(D_MODEL),
        out_shape=jax.ShapeDtypeStruct((B, M_SUB, N_RES * D_MODEL), F32),
        compiler_params=_cparams(("parallel", "parallel")),
        name="ple_gate",
    )(h, pv, wp, gp, wg, bg)


def _t5_bucket(dist):
    max_exact = REL_BUCKETS // 2
    n = jnp.maximum(dist, 0)
    nf = jnp.maximum(n, 1).astype(F32)
    large = max_exact + (jnp.log(nf / max_exact) / math.log(REL_MAX_DISTANCE / max_exact)
                         * (REL_BUCKETS - max_exact)).astype(jnp.int32)
    large = jnp.minimum(large, REL_BUCKETS - 1)
    return jnp.where(n < max_exact, n, large)


def _bias_table(rel_bias, local_index, dilation, with_prev):
    loc = np.asarray(local_index)
    delta = loc[:, None] - loc[None, :]
    if with_prev:
        delta = np.concatenate([delta + BAND_BLOCK, delta], axis=1)
    ok = (delta >= 0) & (delta <= BAND_BLOCK)
    bias = rel_bias.astype(F32)[_t5_bucket(jnp.asarray(delta * dilation, jnp.int32))]
    return jnp.where(jnp.asarray(ok)[None], bias.transpose(2, 0, 1), NEG)


def _block_diag_ones(sizes, total):
    g = np.zeros((total, total), np.float32)
    o = 0
    for s, on in sizes:
        if on:
            g[o:o + s, o:o + s] = 1.0
        o += s
    return jnp.asarray(g, BF16)


def _route(logits, n_tokens):
    glog = logits[:, :N_GROUPS]
    gprob = jax.nn.softmax(glog, axis=-1)
    g_gate, g_idx = lax.top_k(gprob, 1)
    elog = logits[:, N_GROUPS:N_GROUPS + N_EXPERTS].reshape(n_tokens, N_GROUPS, EXPERTS_PER_GROUP)
    elog = jnp.take_along_axis(elog, g_idx[:, :, None], axis=1)[:, 0]
    e_w, e_idx = lax.top_k(jax.nn.softmax(elog, axis=-1), TOP_K)
    e_w = e_w / jnp.sum(e_w, axis=-1, keepdims=True)
    weights = g_gate * e_w
    flat_e = (g_idx * EXPERTS_PER_GROUP + e_idx).reshape(-1).astype(jnp.int32)
    n_assign = n_tokens * TOP_K
    order = jnp.argsort(flat_e)
    sorted_e = flat_e[order]
    counts = jnp.sum(flat_e[:, None] == jnp.arange(N_EXPERTS, dtype=jnp.int32)[None, :], axis=0,
                     dtype=jnp.int32)
    pcounts = (counts + MOE_TM - 1) // MOE_TM * MOE_TM
    pend = jnp.cumsum(pcounts)
    pstart = pend - pcounts
    start = jnp.cumsum(counts) - counts
    dest = pstart[sorted_e] + (jnp.arange(n_assign, dtype=jnp.int32) - start[sorted_e])
    n_rows = n_assign + N_EXPERTS * MOE_TM
    row_tok = jnp.zeros((n_rows,), jnp.int32).at[dest].set((order // TOP_K).astype(jnp.int32))
    pos = jnp.zeros((n_assign,), jnp.int32).at[order].set(dest.astype(jnp.int32))
    n_blocks = n_rows // MOE_TM
    n_used = (pend[-1] // MOE_TM).astype(jnp.int32)
    blk_start = jnp.minimum(jnp.arange(n_blocks, dtype=jnp.int32), n_used - 1) * MOE_TM
    blk_e = jnp.minimum(jnp.searchsorted(pend, blk_start, side='right'), N_EXPERTS - 1).astype(jnp.int32)
    return weights, row_tok, pos.reshape(n_tokens, TOP_K), blk_e, n_used.reshape(1)


def kernel(x, p, rel_bias, norm_mix_gain, w_in, qn_a_gain, kn_a_gain, q_a_gain, w_q_up, kv_a_gain, w_kv_up, qn_nope_gain, qn_rope_gain, kn_nope_gain, kn_rope_gain, w_out, norm_ffn_gain, w_router_group, b_router_group, w_router_expert, b_router_expert, w_exp_gate, w_exp_up, w_exp_down, w_ple_proj, ple_norm_gain, w_ple_gate, b_ple_gate):
    B, S, D = x.shape
    assert (S, D) == (SEQ, D_MODEL) and p.shape[0] == 1
    n_tokens = B * S
    row = lambda a: a.reshape(1, -1).astype(F32)
    zeros = lambda *s: jnp.zeros(s, F32)

    w_in_p = jnp.concatenate([w_in[0, :, :C_KR0], zeros(D, 64), w_in[0, :, C_KR0:], zeros(D, 32)],
                             axis=1).astype(BF16)
    gsum_a = _block_diag_ones([(HEAD_DIM_A, True)] * N_HEADS_A, A_WIDTH)
    gq_a = row(jnp.tile(qn_a_gain[0], N_HEADS_A)) * (HEAD_DIM_A ** -0.5)
    gk_a = row(jnp.tile(kn_a_gain[0], N_HEADS_A))

    wq_p = jnp.pad(w_q_up[0].reshape(Q_LORA, N_HEADS_M, NOPE_DIM + ROPE_DIM),
                   ((0, 0), (0, 0), (0, HEAD_SLOT - NOPE_DIM - ROPE_DIM))).reshape(Q_LORA, -1).astype(BF16)
    wkv = w_kv_up[0].reshape(KV_LORA, N_HEADS_M, NOPE_DIM + V_DIM)
    wk_p = jnp.pad(wkv[..., :NOPE_DIM], ((0, 0), (0, 0), (0, HEAD_SLOT - NOPE_DIM))).reshape(KV_LORA, -1)
    wkv_p = jnp.concatenate([wk_p, wkv[..., NOPE_DIM:].reshape(KV_LORA, -1)], axis=1).astype(BF16)
    gs_q = _block_diag_ones([(NOPE_DIM, True), (ROPE_DIM, True), (32, False)], HEAD_SLOT)
    gs_k = _block_diag_ones([(NOPE_DIM, True), (64, False)], HEAD_SLOT)
    inv_cnt_q = jnp.asarray(np.concatenate([np.full(64, 1 / NOPE_DIM), np.full(32, 1 / ROPE_DIM),
                                            np.ones(32)]).astype(np.float32)).reshape(1, HEAD_SLOT)
    mla_scale = (NOPE_DIM + ROPE_DIM) ** -0.5
    gq_m = row(jnp.concatenate([qn_nope_gain[0], qn_rope_gain[0], zeros(32)])) * mla_scale
    gk_m = row(jnp.concatenate([kn_nope_gain[0], zeros(64)]))
    gkr_m = row(jnp.concatenate([zeros(64), kn_rope_gain[0], zeros(32)]))

    half = ROPE_DIM // 2
    inv = 1.0 / (ROPE_THETA ** (jnp.arange(half, dtype=F32) * 2.0 / ROPE_DIM))
    pos = (jnp.arange(M_SUB, dtype=jnp.int32)[None, :] * N_RES + jnp.arange(N_RES, dtype=jnp.int32)[:, None])
    ang = pos.astype(F32)[..., None] * inv
    cosv, sinv = jnp.cos(ang), jnp.sin(ang)
    tshape = (N_RES, M_SUB, 1)
    cos_t = jnp.concatenate([jnp.ones(tshape[:2] + (64,), F32), cosv, cosv, jnp.ones(tshape[:2] + (32,), F32)], -1)
    zpad = lambda n: jnp.zeros(tshape[:2] + (n,), F32)
    sin_a = jnp.concatenate([zpad(80), sinv, zpad(32)], -1)
    sin_b = jnp.concatenate([zpad(64), -sinv, zpad(48)], -1)

    loc1 = [16 * a + r for r in range(16) for a in range(8)]
    loc2 = [4 * a + c for c in range(4) for a in range(32)]
    loc3 = list(range(BAND_BLOCK))
    t1 = _bias_table(rel_bias, loc1, 1, True)
    t2 = _bias_table(rel_bias, loc2, 4, True)
    t3 = jnp.concatenate([jnp.full((N_HEADS_A, BAND_BLOCK, BAND_BLOCK), NEG, F32),
                          _bias_table(rel_bias, loc3, 16, False)], axis=-1)

    w_r = jnp.concatenate([w_router_group[0], w_router_expert[0],
                           zeros(D, LANES - N_GROUPS - N_EXPERTS)], axis=1)
    w_r_hi = w_r.astype(BF16)
    w_r_lo = (w_r - w_r_hi.astype(F32)).astype(BF16)
    b_r = row(jnp.concatenate([b_router_group[0], b_router_expert[0], zeros(LANES - N_GROUPS - N_EXPERTS)]))

    xv = x.reshape(B, M_SUB, N_RES * D)
    qa, ka, va, cq, ckv, kr = _inproj(xv, row(norm_mix_gain[0]), w_in_p, gsum_a, gq_a, gk_a,
                                      row(q_a_gain[0]), row(kv_a_gain[0]))
    o_a = _dilated(qa, ka, va, t1, t2, t3)
    q_m, k_m, v_m = _mla_prep(cq, ckv, kr, wq_p, wkv_p, gs_q, gs_k, inv_cnt_q, gq_m, gk_m, gkr_m,
                              cos_t, sin_a, sin_b)
    o_m = _mla_attn(q_m.reshape(B, S, -1), k_m.reshape(B, S, -1), v_m.reshape(B, S, -1))
    h1, xn2, logits = _outproj(o_a, o_m.reshape(B, M_SUB, N_RES * M_WIDTH), xv, w_out[0].astype(BF16),
                               row(norm_ffn_gain[0]), w_r_hi, w_r_lo, b_r)

    weights, row_tok, pos_tk, blk_e, n_used = _route(logits.reshape(n_tokens, LANES), n_tokens)
    xs = xn2.reshape(n_tokens, D)[row_tok]
    y = _experts(blk_e, n_used, xs, w_exp_gate[0].astype(BF16), w_exp_up[0].astype(BF16),
                 w_exp_down[0].astype(BF16))
    moe = (weights[:, 0:1] * y[pos_tk[:, 0]].astype(F32) + weights[:, 1:2] * y[pos_tk[:, 1]].astype(F32))
    h2 = h1 + moe.reshape(h1.shape)

    out = _ple(h2, p[0].reshape(B, M_SUB, N_RES * PLE_DIM), w_ple_proj[0].astype(BF16),
               row(ple_norm_gain[0]), w_ple_gate[0].astype(BF16), row(b_ple_gate[0]))
    return out.reshape(B, S, D)
```

```python
import functools
import math

import jax
import jax.numpy as jnp
import numpy as np
from jax import lax
from jax.experimental import pallas as pl
from jax.experimental.pallas import tpu as pltpu

F32 = jnp.float32
BF16 = jnp.bfloat16

D_MODEL = 1024
SEQ = 2048
PLE_DIM = 256
EPS = 1e-6
NEG = -1e30
HEAD_DIM_A = 64
A_WIDTH = 512
N_HEADS_A = 8
BAND_BLOCK = 128
REL_BUCKETS = 32
REL_MAX_DISTANCE = 2048
M_WIDTH = 512
V_DIM = 64
N_HEADS_M = 8
Q_LORA = 384
KV_LORA = 256
NOPE_DIM = 64
ROPE_DIM = 32
ROPE_THETA = 10000.0
N_GROUPS = 4
EXPERTS_PER_GROUP = 8
N_EXPERTS = 32
TOP_K = 2
EXPERT_FF = 512

LANES = 128
N_RES = 16
M_SUB = SEQ // N_RES
TM = 512
N_ROW_STEPS = SEQ // TM
M_STEP = TM // N_RES
HEAD_SLOT = LANES
MLA_TQ = 512
MLA_TK = 256
MOE_TM = 512
ATTN_GROUP = 4
VMEM_LIMIT = 48 * 1024 * 1024

IN_COLS_PAD = 3 * A_WIDTH + Q_LORA + KV_LORA + LANES
C_Q0, C_K0, C_V0, C_CQ0, C_CKV0, C_KR0 = 0, 512, 1024, 1536, 1920, 2176
ROUTE_LANE0 = N_GROUPS

NT_DIMS = (((1,), (1,)), ((), ()))


def _cparams(sem):
    return pltpu.CompilerParams(dimension_semantics=sem, vmem_limit_bytes=VMEM_LIMIT)


def _full(a):
    return pl.BlockSpec(a.shape, lambda *_: (0,) * a.ndim)


def _rms(x, n):
    return x * lax.rsqrt(jnp.sum(x * x, axis=-1, keepdims=True) * (1.0 / n) + EPS)


def _residue_rows(r):
    return pl.ds(r, M_STEP, stride=N_RES)


def _inproj_kernel(x_ref, g_ref, w_ref, gsum_ref, gq_ref, gk_ref, gcq_ref, gckv_ref,
                   qa_ref, ka_ref, va_ref, cq_ref, ckv_ref, kr_ref, perm_scr):
    xn = (_rms(x_ref[0], D_MODEL) * g_ref[...]).astype(BF16)

    def proj(c0, c1):
        return jnp.dot(xn, w_ref[:, c0:c1], preferred_element_type=F32)

    def head_norm(t, gain_ref):
        ss = jnp.dot((t * t).astype(BF16), gsum_ref[...], preferred_element_type=F32)
        return t * lax.rsqrt(ss * (1.0 / HEAD_DIM_A) + EPS) * gain_ref[...]

    def put_residue(ref, val):
        for c in range(A_WIDTH // LANES):
            perm_scr[c] = val[:, c * LANES:(c + 1) * LANES]
        for r in range(N_RES):
            for c in range(A_WIDTH // LANES):
                ref[0, r, :, c * LANES:(c + 1) * LANES] = perm_scr[c, _residue_rows(r), :].astype(ref.dtype)

    put_residue(qa_ref, head_norm(proj(C_Q0, C_K0), gq_ref))
    put_residue(ka_ref, head_norm(proj(C_K0, C_V0), gk_ref))
    put_residue(va_ref, proj(C_V0, C_CQ0))
    cq_ref[0] = (_rms(proj(C_CQ0, C_CKV0), Q_LORA) * gcq_ref[...]).astype(BF16)
    ckv_ref[0] = (_rms(proj(C_CKV0, C_KR0), KV_LORA) * gckv_ref[...]).astype(BF16)
    kr_ref[0] = proj(C_KR0, IN_COLS_PAD)


def _inproj(x, g, w_in_p, gsum, gq, gk, gcq, gckv):
    B = x.shape[0]
    res = lambda: (jax.ShapeDtypeStruct((B, N_RES, M_SUB, A_WIDTH), BF16),
                   pl.BlockSpec((1, N_RES, M_STEP, A_WIDTH), lambda b, j: (b, 0, j, 0)))
    nat = lambda c, dt: (jax.ShapeDtypeStruct((B, SEQ, c), dt),
                         pl.BlockSpec((1, TM, c), lambda b, j: (b, j, 0)))
    outs = [res(), res(), res(), nat(Q_LORA, BF16), nat(KV_LORA, BF16), nat(LANES, F32)]
    return pl.pallas_call(
        _inproj_kernel,
        grid=(B, N_ROW_STEPS),
        in_specs=[pl.BlockSpec((1, TM, D_MODEL), lambda b, j: (b, j, 0)),
                  _full(g), _full(w_in_p), _full(gsum), _full(gq), _full(gk), _full(gcq), _full(gckv)],
        out_specs=[o[1] for o in outs],
        out_shape=[o[0] for o in outs],
        scratch_shapes=[pltpu.VMEM((A_WIDTH // LANES, TM, LANES), F32)],
        compiler_params=_cparams(("parallel", "parallel")),
        name="inproj",
    )(x, g, w_in_p, gsum, gq, gk, gcq, gckv)


def _mla_prep_kernel(cq_ref, ckv_ref, kr_ref, wq_ref, wkv_ref, gsq_ref, gsk_ref, icq_ref,
                     gq_ref, gk_ref, gkr_ref, cos_ref, sa_ref, sb_ref, q_ref, k_ref, v_ref):
    cos, sa, sb = cos_ref[...], sa_ref[...], sb_ref[...]
    lane = lax.broadcasted_iota(jnp.int32, (TM, LANES), 1)

    def rope(t):
        return t * cos + pltpu.roll(t, 16, 1) * sa + pltpu.roll(t, LANES - 16, 1) * sb

    q = jnp.dot(cq_ref[0], wq_ref[...], preferred_element_type=F32)
    kv = jnp.dot(ckv_ref[0], wkv_ref[...], preferred_element_type=F32)
    k_rope = rope(_rms(kr_ref[0], ROPE_DIM) * gkr_ref[...])
    for h in range(N_HEADS_M):
        sl = slice(h * HEAD_SLOT, (h + 1) * HEAD_SLOT)
        qh = q[:, sl]
        ss = jnp.dot((qh * qh).astype(BF16), gsq_ref[...], preferred_element_type=F32)
        q_ref[0, :, sl] = rope(qh * lax.rsqrt(ss * icq_ref[...] + EPS) * gq_ref[...]).astype(BF16)
        kh = kv[:, sl]
        ssk = jnp.dot((kh * kh).astype(BF16), gsk_ref[...], preferred_element_type=F32)
        kn = kh * lax.rsqrt(ssk * (1.0 / NOPE_DIM) + EPS) * gk_ref[...] + k_rope
        k_ref[0, :, sl] = kn.astype(BF16)
    for hp in range(N_HEADS_M // 2):
        v_pair = kv[:, N_HEADS_M * HEAD_SLOT + hp * LANES:N_HEADS_M * HEAD_SLOT + (hp + 1) * LANES]
        v_ref[0, :, (2 * hp) * HEAD_SLOT:(2 * hp + 1) * HEAD_SLOT] = jnp.where(lane < V_DIM, v_pair, 1.0).astype(BF16)
        v_ref[0, :, (2 * hp + 1) * HEAD_SLOT:(2 * hp + 2) * HEAD_SLOT] = jnp.where(lane < V_DIM, 1.0, v_pair).astype(BF16)


def _mla_prep(cq, ckv, kr, wq_p, wkv_p, gsq, gsk, icq, gq, gk, gkr, cos, sa, sb):
    B = cq.shape[0]
    rows = lambda c: pl.BlockSpec((1, TM, c), lambda b, j: (b, j, 0))
    tab = pl.BlockSpec((TM, LANES), lambda b, j: (j, 0))
    wide = N_HEADS_M * HEAD_SLOT
    return pl.pallas_call(
        _mla_prep_kernel,
        grid=(B, N_ROW_STEPS),
        in_specs=[rows(Q_LORA), rows(KV_LORA), rows(LANES), _full(wq_p), _full(wkv_p), _full(gsq),
                  _full(gsk), _full(icq), _full(gq), _full(gk), _full(gkr), tab, tab, tab],
        out_specs=[rows(wide)] * 3,
        out_shape=[jax.ShapeDtypeStruct((B, SEQ, wide), BF16)] * 3,
        compiler_params=_cparams(("parallel", "parallel")),
        name="mla_prep",
    )(cq, ckv, kr, wq_p, wkv_p, gsq, gsk, icq, gq, gk, gkr, cos, sa, sb)


def _mla_attn_kernel(q_ref, k_ref, v_ref, o_ref, s_scr, mx_scr, acc_scr):
    i = pl.program_id(2)
    pair = MLA_TQ // MLA_TK
    row = lax.broadcasted_iota(jnp.int32, (MLA_TQ, MLA_TK), 0)
    col = lax.broadcasted_iota(jnp.int32, (MLA_TQ, MLA_TK), 1)
    lane = lax.broadcasted_iota(jnp.int32, (MLA_TQ, LANES), 1)
    heads = [slice(hh * HEAD_SLOT, (hh + 1) * HEAD_SLOT) for hh in range(2)]

    def keys(j):
        return pl.ds(pl.multiple_of(j * MLA_TK, MLA_TK), MLA_TK)

    def scores(hh, j, diag_offset=None):
        s = lax.dot_general(q_ref[0, :, heads[hh]], k_ref[0, keys(j), heads[hh]], NT_DIMS,
                            preferred_element_type=F32)
        if diag_offset is not None:
            s = jnp.where(col + diag_offset <= row, s, NEG)
        s_scr[hh, j] = s
        mx_scr[hh] = jnp.maximum(mx_scr[hh], jnp.maximum(s[:, :LANES], s[:, LANES:]))

    mx_scr[...] = jnp.full(mx_scr.shape, NEG, F32)

    def scores_pair(jj, c):
        for u in range(pair):
            for hh in range(2):
                scores(hh, pair * jj + u)
        return c

    lax.fori_loop(0, i, scores_pair, 0)
    for u in range(pair):
        for hh in range(2):
            scores(hh, pair * i + u, diag_offset=u * MLA_TK)

    row_max = [jnp.max(mx_scr[hh], axis=-1, keepdims=True) for hh in range(2)]
    acc_scr[...] = jnp.zeros(acc_scr.shape, F32)

    def values_pair(jj, c):
        for u in range(pair):
            for hh in range(2):
                j = pair * jj + u
                p = jnp.exp(s_scr[hh, j] - row_max[hh]).astype(BF16)
                acc_scr[hh] += jnp.dot(p, v_ref[0, keys(j), heads[hh]], preferred_element_type=F32)
        return c

    lax.fori_loop(0, i + 1, values_pair, 0)
    num = jnp.where(lane < V_DIM, acc_scr[0], acc_scr[1])
    den = pltpu.roll(jnp.where(lane < V_DIM, acc_scr[1], acc_scr[0]), V_DIM, 1)
    o_ref[0] = (num / den).astype(BF16)


def _mla_attn(q, k, v):
    B = q.shape[0]
    return pl.pallas_call(
        _mla_attn_kernel,
        grid=(B, N_HEADS_M // 2, SEQ // MLA_TQ),
        in_specs=[pl.BlockSpec((1, MLA_TQ, 2 * HEAD_SLOT), lambda b, h, i: (b, i, h)),
                  pl.BlockSpec((1, SEQ, 2 * HEAD_SLOT), lambda b, h, i: (b, 0, h)),
                  pl.BlockSpec((1, SEQ, 2 * HEAD_SLOT), lambda b, h, i: (b, 0, h))],
        out_specs=pl.BlockSpec((1, MLA_TQ, 2 * V_DIM), lambda b, h, i: (b, i, h)),
        out_shape=jax.ShapeDtypeStruct((B, SEQ, M_WIDTH), BF16),
        scratch_shapes=[pltpu.VMEM((2, SEQ // MLA_TK, MLA_TQ, MLA_TK), F32), pltpu.VMEM((2, MLA_TQ, LANES), F32),
                        pltpu.VMEM((2, MLA_TQ, LANES), F32)],
        compiler_params=_cparams(("parallel", "parallel", "arbitrary")),
        name="mla_attn",
    )(q, k, v)


N_BLK = SEQ // BAND_BLOCK


def _dilated_kernel(q_ref, k_ref, v_ref, t1_ref, t2_ref, t3_ref, o_ref,
                    q1, k1, v1, q2, k2, v2, ob1, ls1, ob2, ls2, ob3, ls3):
    bb = BAND_BLOCK
    lane = lax.broadcasted_iota(jnp.int32, (bb, LANES), 1)
    lane_row = lax.broadcasted_iota(jnp.int32, (1, LANES), 1)
    own = [(lane_row < HEAD_DIM_A).astype(BF16), (lane_row >= HEAD_DIM_A).astype(BF16)]

    for src, d1, d2 in ((q_ref, q1, q2), (k_ref, k1, k2), (v_ref, v1, v2)):
        for n2 in range(N_BLK // 2):
            pieces = [src[0, r, 16 * n2:16 * n2 + 16, :].astype(F32) for r in range(N_RES)]
            d1[2 * n2] = jnp.concatenate([p[0:8] for p in pieces], axis=0).astype(BF16)
            d1[2 * n2 + 1] = jnp.concatenate([p[8:16] for p in pieces], axis=0).astype(BF16)
        for n in range(4):
            for r4 in range(4):
                d2[n * 4 + r4] = jnp.concatenate(
                    [src[0, r4 + 4 * c, 32 * n:32 * n + 32, :] for c in range(4)], axis=0)

    def attend(qs, ks, vs, t_ref, o_dst, l_dst, idx, prev_idx):
        qb = qs[idx]
        kc, vc = ks[idx], vs[idx]
        if prev_idx is not None:
            kp, vp = ks[prev_idx], vs[prev_idx]
        acc, mrow = [], []
        for hh in range(2):
            qh = qb * own[hh]
            s_c = lax.dot_general(qh, kc, NT_DIMS, preferred_element_type=F32) + t_ref[hh, :, bb:2 * bb]
            if prev_idx is None:
                m = jnp.max(s_c, axis=-1, keepdims=True)
            else:
                s_p = lax.dot_general(qh, kp, NT_DIMS, preferred_element_type=F32) + t_ref[hh, :, 0:bb]
                m = jnp.max(jnp.maximum(s_c, s_p), axis=-1, keepdims=True)
            a = jnp.dot(jnp.exp(s_c - m).astype(BF16), vc * own[hh] + own[1 - hh],
                        preferred_element_type=F32)
            if prev_idx is not None:
                a = a + jnp.dot(jnp.exp(s_p - m).astype(BF16), vp * own[hh] + own[1 - hh],
                                preferred_element_type=F32)
            acc.append(a)
            mrow.append(m)
        num = jnp.where(lane < HEAD_DIM_A, acc[0], acc[1])
        den = pltpu.roll(jnp.where(lane < HEAD_DIM_A, acc[1], acc[0]), HEAD_DIM_A, 1)
        o_dst[idx] = num / den
        l_dst[idx] = jnp.where(lane < HEAD_DIM_A, mrow[0], mrow[1]) + jnp.log(den)

    def run(qs, ks, vs, t_ref, o_dst, l_dst, n_first, prev_off):
        args = (qs, ks, vs, t_ref, o_dst, l_dst)
        n_static = max(n_first, ATTN_GROUP) if n_first < N_BLK else 0
        for idx in range(n_static):
            attend(*args, idx, None if idx < n_first else idx - prev_off)

        def group(g, c):
            for u in range(ATTN_GROUP):
                idx = g * ATTN_GROUP + u
                attend(*args, idx, None if n_first == N_BLK else idx - prev_off)
            return c

        lax.fori_loop(n_static // ATTN_GROUP, N_BLK // ATTN_GROUP, group, 0)

    run(q1, k1, v1, t1_ref, ob1, ls1, 1, 1)
    run(q2, k2, v2, t2_ref, ob2, ls2, 4, 4)
    run(q_ref.at[0], k_ref.at[0], v_ref.at[0], t3_ref, ob3, ls3, N_BLK, 0)

    for r in range(N_RES):
        r4, c = r % 4, r // 4
        gather1 = lambda ref: jnp.concatenate([ref[n, 8 * r:8 * r + 8, :] for n in range(N_BLK)], axis=0)
        gather2 = lambda ref: jnp.concatenate(
            [ref[n * 4 + r4, 32 * c:32 * c + 32, :] for n in range(4)], axis=0)
        o_b = [gather1(ob1), gather2(ob2), ob3[r]]
        l_b = [gather1(ls1), gather2(ls2), ls3[r]]
        top = jnp.maximum(jnp.maximum(l_b[0], l_b[1]), l_b[2])
        w_b = [jnp.exp(l - top) for l in l_b]
        num = w_b[0] * o_b[0] + w_b[1] * o_b[1] + w_b[2] * o_b[2]
        o_ref[0, r] = (num / (w_b[0] + w_b[1] + w_b[2])).astype(BF16)


def _dilated(qa, ka, va, t1, t2, t3):
    B = qa.shape[0]
    blk = pl.BlockSpec((1, N_RES, M_SUB, LANES), lambda b, h: (b, 0, 0, h))
    tab = lambda t: pl.BlockSpec((2,) + t.shape[1:], lambda b, h: (h, 0, 0))
    blocked_bf16 = pltpu.VMEM((N_BLK, BAND_BLOCK, LANES), BF16)
    blocked_f32 = pltpu.VMEM((N_BLK, BAND_BLOCK, LANES), F32)
    return pl.pallas_call(
        _dilated_kernel,
        grid=(B, N_HEADS_A // 2),
        in_specs=[blk, blk, blk, tab(t1), tab(t2), tab(t3)],
        out_specs=blk,
        out_shape=jax.ShapeDtypeStruct((B, N_RES, M_SUB, A_WIDTH), BF16),
        scratch_shapes=[blocked_bf16] * 6 + [blocked_f32] * 6,
        compiler_params=_cparams(("parallel", "parallel")),
        name="dilated_attn",
    )(qa, ka, va, t1, t2, t3)


R_W1, R_W2, R_E1, R_E2, R_RANK1, R_RANK2 = range(6)


def _outproj_kernel(oa_ref, om_ref, x_ref, wo_ref, g_ref, wrh_ref, wrl_ref, br_ref, tri_ref,
                    h_ref, xn_ref, route_ref, cnt_ref, perm_scr, carry_scr):
    @pl.when((pl.program_id(0) == 0) & (pl.program_id(1) == 0))
    def _():
        carry_scr[...] = jnp.zeros(carry_scr.shape, F32)

    n_chunks = A_WIDTH // LANES
    for r in range(N_RES):
        for c in range(n_chunks):
            perm_scr[c, _residue_rows(r), :] = oa_ref[0, r, :, c * LANES:(c + 1) * LANES].astype(F32)
    oa = jnp.concatenate([perm_scr[c] for c in range(n_chunks)], axis=1).astype(BF16)
    h = (x_ref[0] + jnp.dot(oa, wo_ref[0:A_WIDTH, :], preferred_element_type=F32)
         + jnp.dot(om_ref[0], wo_ref[A_WIDTH:, :], preferred_element_type=F32))
    h_ref[0] = h
    xn = _rms(h, D_MODEL) * g_ref[...]
    xn_ref[0] = xn.astype(BF16)
    hi = xn.astype(BF16)
    lo = (xn - hi.astype(F32)).astype(BF16)
    lg = (jnp.dot(hi, wrh_ref[...], preferred_element_type=F32)
          + jnp.dot(lo, wrh_ref[...], preferred_element_type=F32)
          + jnp.dot(hi, wrl_ref[...], preferred_element_type=F32)) + br_ref[...]

    lane = lax.broadcasted_iota(jnp.int32, (TM, LANES), 1).astype(F32)
    rmax = lambda t: jnp.max(t, axis=-1, keepdims=True)
    rmin = lambda t: jnp.min(t, axis=-1, keepdims=True)
    rsum = lambda t: jnp.sum(t, axis=-1, keepdims=True)
    none = float(LANES)

    gmask = lane < N_GROUPS
    gl = jnp.where(gmask, lg, NEG)
    ge = jnp.where(gmask, jnp.exp(gl - rmax(gl)), 0.0)
    gsum = rsum(ge)
    gprob = ge / gsum
    g_gate = 1.0 / gsum
    g_idx = rmin(jnp.where(gmask & (gprob == g_gate), lane, none))
    lo_lane = ROUTE_LANE0 + EXPERTS_PER_GROUP * g_idx
    emask = (lane >= lo_lane) & (lane < lo_lane + EXPERTS_PER_GROUP)
    el = jnp.where(emask, lg, NEG)
    ee = jnp.where(emask, jnp.exp(el - rmax(el)), 0.0)
    esum = rsum(ee)
    eprob = ee / esum
    p1 = 1.0 / esum
    i1 = rmin(jnp.where(emask & (eprob == p1), lane, none))
    rest = jnp.where(emask & (lane != i1), eprob, -1.0)
    p2 = rmax(rest)
    i2 = rmin(jnp.where(rest == p2, lane, none))
    den = p1 + p2
    w1 = g_gate * (p1 / den)
    w2 = g_gate * (p2 / den)
    onehot = ((lane == i1) | (lane == i2)).astype(F32)
    before = jnp.dot(tri_ref[...], onehot.astype(BF16), preferred_element_type=F32) + carry_scr[0:1, :]
    rank1 = rsum(jnp.where(lane == i1, before, 0.0))
    rank2 = rsum(jnp.where(lane == i2, before, 0.0))
    carry_scr[...] = carry_scr[...] + jnp.sum(onehot, axis=0, keepdims=True)
    cnt_ref[...] = carry_scr[...]

    cols = [(R_W1, w1), (R_W2, w2), (R_E1, i1 - ROUTE_LANE0), (R_E2, i2 - ROUTE_LANE0),
            (R_RANK1, rank1), (R_RANK2, rank2)]
    slab = jnp.zeros((TM, LANES), F32)
    for c, val in cols:
        slab = jnp.where(lane == c, val, slab)
    route_ref[0] = slab


def _outproj(oa, om, x, wo, g, wrh, wrl, br, tri):
    B = oa.shape[0]
    rows = lambda c: pl.BlockSpec((1, TM, c), lambda b, j: (b, j, 0))
    shp = lambda c, dt: jax.ShapeDtypeStruct((B, SEQ, c), dt)
    return pl.pallas_call(
        _outproj_kernel,
        grid=(B, N_ROW_STEPS),
        in_specs=[pl.BlockSpec((1, N_RES, M_STEP, A_WIDTH), lambda b, j: (b, 0, j, 0)),
                  rows(M_WIDTH), rows(D_MODEL), _full(wo), _full(g), _full(wrh), _full(wrl),
                  _full(br), _full(tri)],
        out_specs=[rows(D_MODEL), rows(D_MODEL), rows(LANES), pl.BlockSpec((8, LANES), lambda b, j: (0, 0))],
        out_shape=[shp(D_MODEL, F32), shp(D_MODEL, BF16), shp(LANES, F32),
                   jax.ShapeDtypeStruct((8, LANES), F32)],
        scratch_shapes=[pltpu.VMEM((A_WIDTH // LANES, TM, LANES), F32), pltpu.VMEM((8, LANES), F32)],
        compiler_params=_cparams(("arbitrary", "arbitrary")),
        name="outproj_router",
    )(oa, om, x, wo, g, wrh, wrl, br, tri)


def _expert_kernel(blk_e_ref, n_used_ref, x_ref, wg_ref, wu_ref, wd_ref, y_ref):
    @pl.when(pl.program_id(0) < n_used_ref[0])
    def _():
        x = x_ref[...]
        gate = jnp.dot(x, wg_ref[0], preferred_element_type=F32)
        up = jnp.dot(x, wu_ref[0], preferred_element_type=F32)
        hdn = (gate * jax.nn.sigmoid(gate) * up).astype(BF16)
        y_ref[...] = jnp.dot(hdn, wd_ref[0], preferred_element_type=F32).astype(y_ref.dtype)


def _experts(blk_e, n_used, xs, wg, wu, wd):
    n_blocks = xs.shape[0] // MOE_TM
    row = lambda i, be, nu: (jnp.minimum(i, nu[0] - 1), 0)
    wsel = lambda i, be, nu: (be[i], 0, 0)
    return pl.pallas_call(
        _expert_kernel,
        grid_spec=pltpu.PrefetchScalarGridSpec(
            num_scalar_prefetch=2,
            grid=(n_blocks,),
            in_specs=[pl.BlockSpec((MOE_TM, D_MODEL), row),
                      pl.BlockSpec((1, D_MODEL, EXPERT_FF), wsel),
                      pl.BlockSpec((1, D_MODEL, EXPERT_FF), wsel),
                      pl.BlockSpec((1, EXPERT_FF, D_MODEL), wsel)],
            out_specs=pl.BlockSpec((MOE_TM, D_MODEL), row)),
        out_shape=jax.ShapeDtypeStruct(xs.shape, BF16),
        compiler_params=_cparams(("arbitrary",)),
        name="expert_ffn",
    )(blk_e, n_used, xs, wg, wu, wd)


def _ple_kernel(h_ref, y1_ref, y2_ref, route_ref, p_ref, wp_ref, gp_ref, wg_ref, bg_ref, o_ref):
    route = route_ref[0]
    h = (h_ref[0] + route[:, R_W1:R_W1 + 1] * y1_ref[...].astype(F32)
         + route[:, R_W2:R_W2 + 1] * y2_ref[...].astype(F32))
    e = _rms(jnp.dot(p_ref[0].astype(BF16), wp_ref[...], preferred_element_type=F32), D_MODEL) * gp_ref[...]
    g = jax.nn.sigmoid(jnp.dot(h.astype(BF16), wg_ref[...], preferred_element_type=F32) + bg_ref[...])
    o_ref[0] = h + g * e


def _ple(h, y1, y2, route, p, wp, gp, wg, bg):
    B = h.shape[0]
    rows = lambda c: pl.BlockSpec((1, TM, c), lambda b, j: (b, j, 0))
    flat = pl.BlockSpec((TM, D_MODEL), lambda b, j: (b * N_ROW_STEPS + j, 0))
    return pl.pallas_call(
        _ple_kernel,
        grid=(B, N_ROW_STEPS),
        in_specs=[rows(D_MODEL), flat, flat, rows(LANES), rows(PLE_DIM), _full(wp), _full(gp),
                  _full(wg), _full(bg)],
        out_specs=rows(D_MODEL),
        out_shape=jax.ShapeDtypeStruct((B, SEQ, D_MODEL), F32),
        compiler_params=_cparams(("parallel", "parallel")),
        name="ple_gate",
    )(h, y1, y2, route, p, wp, gp, wg, bg)


def _t5_bucket(dist):
    max_exact = REL_BUCKETS // 2
    n = jnp.maximum(dist, 0)
    nf = jnp.maximum(n, 1).astype(F32)
    large = max_exact + (jnp.log(nf / max_exact) / math.log(REL_MAX_DISTANCE / max_exact)
                         * (REL_BUCKETS - max_exact)).astype(jnp.int32)
    large = jnp.minimum(large, REL_BUCKETS - 1)
    return jnp.where(n < max_exact, n, large)


def _bias_table(rel_bias, local_index, dilation, with_prev):
    loc = np.asarray(local_index)
    delta = loc[:, None] - loc[None, :]
    if with_prev:
        delta = np.concatenate([delta + BAND_BLOCK, delta], axis=1)
    ok = (delta >= 0) & (delta <= BAND_BLOCK)
    bias = rel_bias.astype(F32)[_t5_bucket(jnp.asarray(delta * dilation, jnp.int32))]
    return jnp.where(jnp.asarray(ok)[None], bias.transpose(2, 0, 1), NEG)


def _block_diag_ones(sizes, total):
    g = np.zeros((total, total), np.float32)
    o = 0
    for s, on in sizes:
        if on:
            g[o:o + s, o:o + s] = 1.0
        o += s
    return jnp.asarray(g, BF16)


def _dispatch_plan(route, counts, n_tokens):
    e = route[:, R_E1:R_E2 + 1].astype(jnp.int32)
    rank = route[:, R_RANK1:R_RANK2 + 1].astype(jnp.int32)
    pcounts = (counts + MOE_TM - 1) // MOE_TM * MOE_TM
    pend = jnp.cumsum(pcounts)
    pstart = pend - pcounts
    ids = jnp.arange(N_EXPERTS, dtype=jnp.int32)
    pos = rank + jnp.sum(jnp.where(e[..., None] == ids, pstart, 0), axis=-1)
    n_assign = n_tokens * TOP_K
    shift = int(math.ceil(math.log2(n_assign)))
    keys = (e.reshape(-1) << shift) | jnp.arange(n_assign, dtype=jnp.int32)
    tok_sorted = (jnp.sort(keys) & ((1 << shift) - 1)) // TOP_K
    n_rows = n_assign + N_EXPERTS * MOE_TM
    rows = jnp.arange(n_rows, dtype=jnp.int32)
    row_e = jnp.sum(rows[:, None] >= pend[None, :], axis=-1)
    row_e = jnp.minimum(row_e, N_EXPERTS - 1)
    pick = lambda tbl: jnp.sum(jnp.where(row_e[:, None] == ids, tbl, 0), axis=-1)
    within = rows - pick(pstart)
    src = jnp.clip(pick(jnp.cumsum(counts) - counts) + within, 0, n_assign - 1)
    row_tok = jnp.where(within < pick(counts), tok_sorted[src], 0)
    n_blocks = n_rows // MOE_TM
    n_used = (pend[-1] // MOE_TM).astype(jnp.int32)
    blk_start = jnp.minimum(jnp.arange(n_blocks, dtype=jnp.int32), n_used - 1) * MOE_TM
    blk_e = jnp.minimum(jnp.sum(blk_start[:, None] >= pend[None, :], axis=-1), N_EXPERTS - 1).astype(jnp.int32)
    return row_tok, pos, blk_e, n_used.reshape(1)


def kernel(x, p, rel_bias, norm_mix_gain, w_in, qn_a_gain, kn_a_gain, q_a_gain, w_q_up, kv_a_gain, w_kv_up, qn_nope_gain, qn_rope_gain, kn_nope_gain, kn_rope_gain, w_out, norm_ffn_gain, w_router_group, b_router_group, w_router_expert, b_router_expert, w_exp_gate, w_exp_up, w_exp_down, w_ple_proj, ple_norm_gain, w_ple_gate, b_ple_gate):
    B, S, D = x.shape
    assert (S, D) == (SEQ, D_MODEL) and p.shape[0] == 1
    n_tokens = B * S
    row = lambda a: a.reshape(1, -1).astype(F32)
    zeros = lambda *s: jnp.zeros(s, F32)

    w_in_p = jnp.concatenate([w_in[0, :, :C_KR0], zeros(D, 64), w_in[0, :, C_KR0:], zeros(D, 32)],
                             axis=1).astype(BF16)
    gsum_a = _block_diag_ones([(HEAD_DIM_A, True)] * N_HEADS_A, A_WIDTH)
    gq_a = row(jnp.tile(qn_a_gain[0], N_HEADS_A)) * (HEAD_DIM_A ** -0.5)
    gk_a = row(jnp.tile(kn_a_gain[0], N_HEADS_A))

    wq_p = jnp.pad(w_q_up[0].reshape(Q_LORA, N_HEADS_M, NOPE_DIM + ROPE_DIM),
                   ((0, 0), (0, 0), (0, HEAD_SLOT - NOPE_DIM - ROPE_DIM))).reshape(Q_LORA, -1).astype(BF16)
    wkv = w_kv_up[0].reshape(KV_LORA, N_HEADS_M, NOPE_DIM + V_DIM)
    wk_p = jnp.pad(wkv[..., :NOPE_DIM], ((0, 0), (0, 0), (0, HEAD_SLOT - NOPE_DIM))).reshape(KV_LORA, -1)
    wkv_p = jnp.concatenate([wk_p, wkv[..., NOPE_DIM:].reshape(KV_LORA, -1)], axis=1).astype(BF16)
    gs_q = _block_diag_ones([(NOPE_DIM, True), (ROPE_DIM, True), (32, False)], HEAD_SLOT)
    gs_k = _block_diag_ones([(NOPE_DIM, True), (64, False)], HEAD_SLOT)
    inv_cnt_q = jnp.asarray(np.concatenate([np.full(64, 1 / NOPE_DIM), np.full(32, 1 / ROPE_DIM),
                                            np.ones(32)]).astype(np.float32)).reshape(1, HEAD_SLOT)
    mla_scale = (NOPE_DIM + ROPE_DIM) ** -0.5
    gq_m = row(jnp.concatenate([qn_nope_gain[0], qn_rope_gain[0], zeros(32)])) * mla_scale
    gk_m = row(jnp.concatenate([kn_nope_gain[0], zeros(64)]))
    gkr_m = row(jnp.concatenate([zeros(64), kn_rope_gain[0], zeros(32)]))

    half = ROPE_DIM // 2
    inv = 1.0 / (ROPE_THETA ** (jnp.arange(half, dtype=F32) * 2.0 / ROPE_DIM))
    ang = jnp.arange(S, dtype=jnp.int32).astype(F32)[:, None] * inv[None, :]
    cosv, sinv = jnp.cos(ang), jnp.sin(ang)
    cos_t = jnp.concatenate([jnp.ones((S, 64), F32), cosv, cosv, jnp.ones((S, 32), F32)], -1)
    sin_a = jnp.concatenate([zeros(S, 80), sinv, zeros(S, 32)], -1)
    sin_b = jnp.concatenate([zeros(S, 64), -sinv, zeros(S, 48)], -1)

    loc1 = [16 * a + r for r in range(16) for a in range(8)]
    loc2 = [4 * a + c for c in range(4) for a in range(32)]
    loc3 = list(range(BAND_BLOCK))
    t1 = _bias_table(rel_bias, loc1, 1, True)
    t2 = _bias_table(rel_bias, loc2, 4, True)
    t3 = jnp.concatenate([jnp.full((N_HEADS_A, BAND_BLOCK, BAND_BLOCK), NEG, F32),
                          _bias_table(rel_bias, loc3, 16, False)], axis=-1)

    w_r = jnp.concatenate([w_router_group[0], w_router_expert[0],
                           zeros(D, LANES - N_GROUPS - N_EXPERTS)], axis=1)
    w_r_hi = w_r.astype(BF16)
    w_r_lo = (w_r - w_r_hi.astype(F32)).astype(BF16)
    b_r = row(jnp.concatenate([b_router_group[0], b_router_expert[0], zeros(LANES - N_GROUPS - N_EXPERTS)]))
    tri = jnp.asarray(np.tril(np.ones((TM, TM), np.float32), -1), BF16)

    qa, ka, va, cq, ckv, kr = _inproj(x, row(norm_mix_gain[0]), w_in_p, gsum_a, gq_a, gk_a,
                                      row(q_a_gain[0]), row(kv_a_gain[0]))
    o_a = _dilated(qa, ka, va, t1, t2, t3)
    q_m, k_m, v_m = _mla_prep(cq, ckv, kr, wq_p, wkv_p, gs_q, gs_k, inv_cnt_q, gq_m, gk_m, gkr_m,
                              cos_t, sin_a, sin_b)
    o_m = _mla_attn(q_m, k_m, v_m)
    h1, xn2, route, cnt = _outproj(o_a, o_m, x, w_out[0].astype(BF16), row(norm_ffn_gain[0]),
                                   w_r_hi, w_r_lo, b_r, tri)

    route_flat = route.reshape(n_tokens, LANES)
    counts = cnt[0, ROUTE_LANE0:ROUTE_LANE0 + N_EXPERTS].astype(jnp.int32)
    row_tok, pos, blk_e, n_used = _dispatch_plan(route_flat, counts, n_tokens)
    xs = xn2.reshape(n_tokens, D)[row_tok]
    y = _experts(blk_e, n_used, xs, w_exp_gate[0].astype(BF16), w_exp_up[0].astype(BF16),
                 w_exp_down[0].astype(BF16))
    return _ple(h1, y[pos[:, 0]], y[pos[:, 1]], route, p[0], w_ple_proj[0].astype(BF16),
                row(ple_norm_gain[0]), w_ple_gate[0].astype(BF16), row(b_ple_gate[0]))
```

```python
import functools
import math

import jax
import jax.numpy as jnp
import numpy as np
from jax import lax
from jax.experimental import pallas as pl
from jax.experimental.pallas import tpu as pltpu

F32 = jnp.float32
BF16 = jnp.bfloat16

D_MODEL = 1024
SEQ = 2048
PLE_DIM = 256
EPS = 1e-6
NEG = -1e30
HEAD_DIM_A = 64
A_WIDTH = 512
N_HEADS_A = 8
BAND_BLOCK = 128
REL_BUCKETS = 32
REL_MAX_DISTANCE = 2048
M_WIDTH = 512
V_DIM = 64
N_HEADS_M = 8
Q_LORA = 384
KV_LORA = 256
NOPE_DIM = 64
ROPE_DIM = 32
ROPE_THETA = 10000.0
N_GROUPS = 4
EXPERTS_PER_GROUP = 8
N_EXPERTS = 32
TOP_K = 2
EXPERT_FF = 512

LANES = 128
N_RES = 16
M_SUB = SEQ // N_RES
TM = 512
N_ROW_STEPS = SEQ // TM
M_STEP = TM // N_RES
HEAD_SLOT = LANES
MLA_TQ = 512
MLA_TK = 256
MOE_TM = 512
ATTN_GROUP = 4
VMEM_LIMIT = 48 * 1024 * 1024

IN_COLS_PAD = 3 * A_WIDTH + Q_LORA + KV_LORA + LANES
C_Q0, C_K0, C_V0, C_CQ0, C_CKV0, C_KR0 = 0, 512, 1024, 1536, 1920, 2176
ROUTE_LANE0 = N_GROUPS

NT_DIMS = (((1,), (1,)), ((), ()))


def _cparams(sem):
    return pltpu.CompilerParams(dimension_semantics=sem, vmem_limit_bytes=VMEM_LIMIT)


def _full(a):
    return pl.BlockSpec(a.shape, lambda *_: (0,) * a.ndim)


def _rms(x, n):
    return x * lax.rsqrt(jnp.sum(x * x, axis=-1, keepdims=True) * (1.0 / n) + EPS)


def _residue_rows(r):
    return pl.ds(r, M_STEP, stride=N_RES)


def _inproj_kernel(x_ref, g_ref, w_ref, gsum_ref, gq_ref, gk_ref, gcq_ref, gckv_ref,
                   qa_ref, ka_ref, va_ref, cq_ref, ckv_ref, kr_ref, perm_scr):
    xn = (_rms(x_ref[0], D_MODEL) * g_ref[...]).astype(BF16)

    def proj(c0, c1):
        return jnp.dot(xn, w_ref[:, c0:c1], preferred_element_type=F32)

    def head_norm(t, gain_ref):
        ss = jnp.dot((t * t).astype(BF16), gsum_ref[...], preferred_element_type=F32)
        return t * lax.rsqrt(ss * (1.0 / HEAD_DIM_A) + EPS) * gain_ref[...]

    def put_residue(ref, val):
        for c in range(A_WIDTH // LANES):
            perm_scr[c] = val[:, c * LANES:(c + 1) * LANES]
        for r in range(N_RES):
            for c in range(A_WIDTH // LANES):
                ref[0, r, :, c * LANES:(c + 1) * LANES] = perm_scr[c, _residue_rows(r), :].astype(ref.dtype)

    put_residue(qa_ref, head_norm(proj(C_Q0, C_K0), gq_ref))
    put_residue(ka_ref, head_norm(proj(C_K0, C_V0), gk_ref))
    put_residue(va_ref, proj(C_V0, C_CQ0))
    cq_ref[0] = (_rms(proj(C_CQ0, C_CKV0), Q_LORA) * gcq_ref[...]).astype(BF16)
    ckv_ref[0] = (_rms(proj(C_CKV0, C_KR0), KV_LORA) * gckv_ref[...]).astype(BF16)
    kr_ref[0] = proj(C_KR0, IN_COLS_PAD)


def _inproj(x, g, w_in_p, gsum, gq, gk, gcq, gckv):
    B = x.shape[0]
    res = lambda: (jax.ShapeDtypeStruct((B, N_RES, M_SUB, A_WIDTH), BF16),
                   pl.BlockSpec((1, N_RES, M_STEP, A_WIDTH), lambda b, j: (b, 0, j, 0)))
    nat = lambda c, dt: (jax.ShapeDtypeStruct((B, SEQ, c), dt),
                         pl.BlockSpec((1, TM, c), lambda b, j: (b, j, 0)))
    outs = [res(), res(), res(), nat(Q_LORA, BF16), nat(KV_LORA, BF16), nat(LANES, F32)]
    return pl.pallas_call(
        _inproj_kernel,
        grid=(B, N_ROW_STEPS),
        in_specs=[pl.BlockSpec((1, TM, D_MODEL), lambda b, j: (b, j, 0)),
                  _full(g), _full(w_in_p), _full(gsum), _full(gq), _full(gk), _full(gcq), _full(gckv)],
        out_specs=[o[1] for o in outs],
        out_shape=[o[0] for o in outs],
        scratch_shapes=[pltpu.VMEM((A_WIDTH // LANES, TM, LANES), F32)],
        compiler_params=_cparams(("parallel", "parallel")),
        name="inproj",
    )(x, g, w_in_p, gsum, gq, gk, gcq, gckv)


def _mla_prep_kernel(cq_ref, ckv_ref, kr_ref, wq_ref, wkv_ref, gsq_ref, gsk_ref, icq_ref,
                     gq_ref, gk_ref, gkr_ref, cos_ref, sa_ref, sb_ref, q_ref, k_ref, v_ref):
    cos, sa, sb = cos_ref[...], sa_ref[...], sb_ref[...]
    lane = lax.broadcasted_iota(jnp.int32, (TM, LANES), 1)

    def rope(t):
        return t * cos + pltpu.roll(t, 16, 1) * sa + pltpu.roll(t, LANES - 16, 1) * sb

    q = jnp.dot(cq_ref[0], wq_ref[...], preferred_element_type=F32)
    kv = jnp.dot(ckv_ref[0], wkv_ref[...], preferred_element_type=F32)
    k_rope = rope(_rms(kr_ref[0], ROPE_DIM) * gkr_ref[...])
    for h in range(N_HEADS_M):
        sl = slice(h * HEAD_SLOT, (h + 1) * HEAD_SLOT)
        qh = q[:, sl]
        ss = jnp.dot((qh * qh).astype(BF16), gsq_ref[...], preferred_element_type=F32)
        q_ref[0, :, sl] = rope(qh * lax.rsqrt(ss * icq_ref[...] + EPS) * gq_ref[...]).astype(BF16)
        kh = kv[:, sl]
        ssk = jnp.dot((kh * kh).astype(BF16), gsk_ref[...], preferred_element_type=F32)
        kn = kh * lax.rsqrt(ssk * (1.0 / NOPE_DIM) + EPS) * gk_ref[...] + k_rope
        k_ref[0, :, sl] = kn.astype(BF16)
    for hp in range(N_HEADS_M // 2):
        v_pair = kv[:, N_HEADS_M * HEAD_SLOT + hp * LANES:N_HEADS_M * HEAD_SLOT + (hp + 1) * LANES]
        v_ref[0, :, (2 * hp) * HEAD_SLOT:(2 * hp + 1) * HEAD_SLOT] = jnp.where(lane < V_DIM, v_pair, 1.0).astype(BF16)
        v_ref[0, :, (2 * hp + 1) * HEAD_SLOT:(2 * hp + 2) * HEAD_SLOT] = jnp.where(lane < V_DIM, 1.0, v_pair).astype(BF16)


def _mla_prep(cq, ckv, kr, wq_p, wkv_p, gsq, gsk, icq, gq, gk, gkr, cos, sa, sb):
    B = cq.shape[0]
    rows = lambda c: pl.BlockSpec((1, TM, c), lambda b, j: (b, j, 0))
    tab = pl.BlockSpec((TM, LANES), lambda b, j: (j, 0))
    wide = N_HEADS_M * HEAD_SLOT
    return pl.pallas_call(
        _mla_prep_kernel,
        grid=(B, N_ROW_STEPS),
        in_specs=[rows(Q_LORA), rows(KV_LORA), rows(LANES), _full(wq_p), _full(wkv_p), _full(gsq),
                  _full(gsk), _full(icq), _full(gq), _full(gk), _full(gkr), tab, tab, tab],
        out_specs=[rows(wide)] * 3,
        out_shape=[jax.ShapeDtypeStruct((B, SEQ, wide), BF16)] * 3,
        compiler_params=_cparams(("parallel", "parallel")),
        name="mla_prep",
    )(cq, ckv, kr, wq_p, wkv_p, gsq, gsk, icq, gq, gk, gkr, cos, sa, sb)


def _mla_attn_kernel(q_ref, k_ref, v_ref, o_ref, s_scr, mx_scr, acc_scr):
    i = pl.program_id(2)
    pair = MLA_TQ // MLA_TK
    row = lax.broadcasted_iota(jnp.int32, (MLA_TQ, MLA_TK), 0)
    col = lax.broadcasted_iota(jnp.int32, (MLA_TQ, MLA_TK), 1)
    lane = lax.broadcasted_iota(jnp.int32, (MLA_TQ, LANES), 1)
    heads = [slice(hh * HEAD_SLOT, (hh + 1) * HEAD_SLOT) for hh in range(2)]

    def keys(j):
        return pl.ds(pl.multiple_of(j * MLA_TK, MLA_TK), MLA_TK)

    def scores(hh, j, diag_offset=None):
        s = lax.dot_general(q_ref[0, :, heads[hh]], k_ref[0, keys(j), heads[hh]], NT_DIMS,
                            preferred_element_type=F32)
        if diag_offset is not None:
            s = jnp.where(col + diag_offset <= row, s, NEG)
        s_scr[hh, j] = s
        mx_scr[hh] = jnp.maximum(mx_scr[hh], jnp.maximum(s[:, :LANES], s[:, LANES:]))

    mx_scr[...] = jnp.full(mx_scr.shape, NEG, F32)

    def scores_pair(jj, c):
        for u in range(pair):
            for hh in range(2):
                scores(hh, pair * jj + u)
        return c

    lax.fori_loop(0, i, scores_pair, 0)
    for u in range(pair):
        for hh in range(2):
            scores(hh, pair * i + u, diag_offset=u * MLA_TK)

    row_max = [jnp.max(mx_scr[hh], axis=-1, keepdims=True) for hh in range(2)]
    acc_scr[...] = jnp.zeros(acc_scr.shape, F32)

    def values_pair(jj, c):
        for u in range(pair):
            for hh in range(2):
                j = pair * jj + u
                p = jnp.exp(s_scr[hh, j] - row_max[hh]).astype(BF16)
                acc_scr[hh] += jnp.dot(p, v_ref[0, keys(j), heads[hh]], preferred_element_type=F32)
        return c

    lax.fori_loop(0, i + 1, values_pair, 0)
    num = jnp.where(lane < V_DIM, acc_scr[0], acc_scr[1])
    den = pltpu.roll(jnp.where(lane < V_DIM, acc_scr[1], acc_scr[0]), V_DIM, 1)
    o_ref[0] = (num / den).astype(BF16)


def _mla_attn(q, k, v):
    B = q.shape[0]
    return pl.pallas_call(
        _mla_attn_kernel,
        grid=(B, N_HEADS_M // 2, SEQ // MLA_TQ),
        in_specs=[pl.BlockSpec((1, MLA_TQ, 2 * HEAD_SLOT), lambda b, h, i: (b, i, h)),
                  pl.BlockSpec((1, SEQ, 2 * HEAD_SLOT), lambda b, h, i: (b, 0, h)),
                  pl.BlockSpec((1, SEQ, 2 * HEAD_SLOT), lambda b, h, i: (b, 0, h))],
        out_specs=pl.BlockSpec((1, MLA_TQ, 2 * V_DIM), lambda b, h, i: (b, i, h)),
        out_shape=jax.ShapeDtypeStruct((B, SEQ, M_WIDTH), BF16),
        scratch_shapes=[pltpu.VMEM((2, SEQ // MLA_TK, MLA_TQ, MLA_TK), F32), pltpu.VMEM((2, MLA_TQ, LANES), F32),
                        pltpu.VMEM((2, MLA_TQ, LANES), F32)],
        compiler_params=_cparams(("parallel", "parallel", "arbitrary")),
        name="mla_attn",
    )(q, k, v)


N_BLK = SEQ // BAND_BLOCK


def _dilated_kernel(q_ref, k_ref, v_ref, t1_ref, t2_ref, t3_ref, o_ref,
                    q1, k1, v1, q2, k2, v2, ob1, ls1, ob2, ls2, ob3, ls3):
    bb = BAND_BLOCK
    lane = lax.broadcasted_iota(jnp.int32, (bb, LANES), 1)
    lane_row = lax.broadcasted_iota(jnp.int32, (1, LANES), 1)
    own = [(lane_row < HEAD_DIM_A).astype(BF16), (lane_row >= HEAD_DIM_A).astype(BF16)]

    for src, d1, d2 in ((q_ref, q1, q2), (k_ref, k1, k2), (v_ref, v1, v2)):
        for n2 in range(N_BLK // 2):
            pieces = [src[0, r, 16 * n2:16 * n2 + 16, :].astype(F32) for r in range(N_RES)]
            d1[2 * n2] = jnp.concatenate([p[0:8] for p in pieces], axis=0).astype(BF16)
            d1[2 * n2 + 1] = jnp.concatenate([p[8:16] for p in pieces], axis=0).astype(BF16)
        for n in range(4):
            for r4 in range(4):
                d2[n * 4 + r4] = jnp.concatenate(
                    [src[0, r4 + 4 * c, 32 * n:32 * n + 32, :] for c in range(4)], axis=0)

    def attend(qs, ks, vs, t_ref, o_dst, l_dst, blocks):
        units = [(idx, prev, hh) for idx, prev in blocks for hh in range(2)]
        scores = []
        for idx, prev, hh in units:
            qh = qs[idx] * own[hh]
            s_c = lax.dot_general(qh, ks[idx], NT_DIMS, preferred_element_type=F32) + t_ref[hh, :, bb:2 * bb]
            s_p = None
            if prev is not None:
                s_p = lax.dot_general(qh, ks[prev], NT_DIMS, preferred_element_type=F32) + t_ref[hh, :, 0:bb]
            scores.append((s_c, s_p))
        probs = []
        for s_c, s_p in scores:
            m = jnp.max(s_c if s_p is None else jnp.maximum(s_c, s_p), axis=-1, keepdims=True)
            e_p = None if s_p is None else jnp.exp(s_p - m).astype(BF16)
            probs.append((m, jnp.exp(s_c - m).astype(BF16), e_p))
        accs = []
        for (idx, prev, hh), (m, e_c, e_p) in zip(units, probs):
            a = jnp.dot(e_c, vs[idx] * own[hh] + own[1 - hh], preferred_element_type=F32)
            if prev is not None:
                a = a + jnp.dot(e_p, vs[prev] * own[hh] + own[1 - hh], preferred_element_type=F32)
            accs.append(a)
        for b, (idx, _) in enumerate(blocks):
            a0, a1 = accs[2 * b], accs[2 * b + 1]
            num = jnp.where(lane < HEAD_DIM_A, a0, a1)
            den = pltpu.roll(jnp.where(lane < HEAD_DIM_A, a1, a0), HEAD_DIM_A, 1)
            o_dst[idx] = num / den
            l_dst[idx] = jnp.where(lane < HEAD_DIM_A, probs[2 * b][0], probs[2 * b + 1][0]) + jnp.log(den)

    def run(qs, ks, vs, t_ref, o_dst, l_dst, n_first, prev_off):
        args = (qs, ks, vs, t_ref, o_dst, l_dst)
        n_static = max(n_first, ATTN_GROUP) if n_first < N_BLK else 0
        for g in range(n_static // ATTN_GROUP):
            attend(*args, [(idx, None if idx < n_first else idx - prev_off)
                           for idx in range(g * ATTN_GROUP, (g + 1) * ATTN_GROUP)])

        def group(g, c):
            ids = [g * ATTN_GROUP + u for u in range(ATTN_GROUP)]
            attend(*args, [(idx, None if n_first == N_BLK else idx - prev_off) for idx in ids])
            return c

        lax.fori_loop(n_static // ATTN_GROUP, N_BLK // ATTN_GROUP, group, 0)

    run(q1, k1, v1, t1_ref, ob1, ls1, 1, 1)
    run(q2, k2, v2, t2_ref, ob2, ls2, 4, 4)
    run(q_ref.at[0], k_ref.at[0], v_ref.at[0], t3_ref, ob3, ls3, N_BLK, 0)

    for r in range(N_RES):
        r4, c = r % 4, r // 4
        gather1 = lambda ref: jnp.concatenate([ref[n, 8 * r:8 * r + 8, :] for n in range(N_BLK)], axis=0)
        gather2 = lambda ref: jnp.concatenate(
            [ref[n * 4 + r4, 32 * c:32 * c + 32, :] for n in range(4)], axis=0)
        o_b = [gather1(ob1), gather2(ob2), ob3[r]]
        l_b = [gather1(ls1), gather2(ls2), ls3[r]]
        top = jnp.maximum(jnp.maximum(l_b[0], l_b[1]), l_b[2])
        w_b = [jnp.exp(l - top) for l in l_b]
        num = w_b[0] * o_b[0] + w_b[1] * o_b[1] + w_b[2] * o_b[2]
        o_ref[0, r] = (num / (w_b[0] + w_b[1] + w_b[2])).astype(BF16)


def _dilated(qa, ka, va, t1, t2, t3):
    B = qa.shape[0]
    blk = pl.BlockSpec((1, N_RES, M_SUB, LANES), lambda b, h: (b, 0, 0, h))
    tab = lambda t: pl.BlockSpec((2,) + t.shape[1:], lambda b, h: (h, 0, 0))
    blocked_bf16 = pltpu.VMEM((N_BLK, BAND_BLOCK, LANES), BF16)
    blocked_f32 = pltpu.VMEM((N_BLK, BAND_BLOCK, LANES), F32)
    return pl.pallas_call(
        _dilated_kernel,
        grid=(B, N_HEADS_A // 2),
        in_specs=[blk, blk, blk, tab(t1), tab(t2), tab(t3)],
        out_specs=blk,
        out_shape=jax.ShapeDtypeStruct((B, N_RES, M_SUB, A_WIDTH), BF16),
        scratch_shapes=[blocked_bf16] * 6 + [blocked_f32] * 6,
        compiler_params=_cparams(("parallel", "parallel")),
        name="dilated_attn",
    )(qa, ka, va, t1, t2, t3)


R_W1, R_W2, R_E1, R_E2, R_RANK1, R_RANK2 = range(6)


def _outproj_kernel(oa_ref, om_ref, x_ref, wo_ref, g_ref, wrh_ref, wrl_ref, br_ref, tri_ref,
                    h_ref, xn_ref, route_ref, cnt_ref, perm_scr, carry_scr):
    @pl.when((pl.program_id(0) == 0) & (pl.program_id(1) == 0))
    def _():
        carry_scr[...] = jnp.zeros(carry_scr.shape, F32)

    n_chunks = A_WIDTH // LANES
    for r in range(N_RES):
        for c in range(n_chunks):
            perm_scr[c, _residue_rows(r), :] = oa_ref[0, r, :, c * LANES:(c + 1) * LANES].astype(F32)
    oa = jnp.concatenate([perm_scr[c] for c in range(n_chunks)], axis=1).astype(BF16)
    h = (x_ref[0] + jnp.dot(oa, wo_ref[0:A_WIDTH, :], preferred_element_type=F32)
         + jnp.dot(om_ref[0], wo_ref[A_WIDTH:, :], preferred_element_type=F32))
    h_ref[0] = h
    xn = _rms(h, D_MODEL) * g_ref[...]
    xn_ref[0] = xn.astype(BF16)
    hi = xn.astype(BF16)
    lo = (xn - hi.astype(F32)).astype(BF16)
    lg = (jnp.dot(hi, wrh_ref[...], preferred_element_type=F32)
          + jnp.dot(lo, wrh_ref[...], preferred_element_type=F32)
          + jnp.dot(hi, wrl_ref[...], preferred_element_type=F32)) + br_ref[...]

    lane = lax.broadcasted_iota(jnp.int32, (TM, LANES), 1).astype(F32)
    rmax = lambda t: jnp.max(t, axis=-1, keepdims=True)
    rmin = lambda t: jnp.min(t, axis=-1, keepdims=True)
    rsum = lambda t: jnp.sum(t, axis=-1, keepdims=True)
    none = float(LANES)

    gmask = lane < N_GROUPS
    gl = jnp.where(gmask, lg, NEG)
    ge = jnp.where(gmask, jnp.exp(gl - rmax(gl)), 0.0)
    gsum = rsum(ge)
    gprob = ge / gsum
    g_gate = 1.0 / gsum
    g_idx = rmin(jnp.where(gmask & (gprob == g_gate), lane, none))
    lo_lane = ROUTE_LANE0 + EXPERTS_PER_GROUP * g_idx
    emask = (lane >= lo_lane) & (lane < lo_lane + EXPERTS_PER_GROUP)
    el = jnp.where(emask, lg, NEG)
    ee = jnp.where(emask, jnp.exp(el - rmax(el)), 0.0)
    esum = rsum(ee)
    eprob = ee / esum
    p1 = 1.0 / esum
    i1 = rmin(jnp.where(emask & (eprob == p1), lane, none))
    rest = jnp.where(emask & (lane != i1), eprob, -1.0)
    p2 = rmax(rest)
    i2 = rmin(jnp.where(rest == p2, lane, none))
    den = p1 + p2
    w1 = g_gate * (p1 / den)
    w2 = g_gate * (p2 / den)
    onehot = ((lane == i1) | (lane == i2)).astype(F32)
    before = jnp.dot(tri_ref[...], onehot.astype(BF16), preferred_element_type=F32) + carry_scr[0:1, :]
    rank1 = rsum(jnp.where(lane == i1, before, 0.0))
    rank2 = rsum(jnp.where(lane == i2, before, 0.0))
    carry_scr[...] = carry_scr[...] + jnp.sum(onehot, axis=0, keepdims=True)
    cnt_ref[...] = carry_scr[...]

    cols = [(R_W1, w1), (R_W2, w2), (R_E1, i1 - ROUTE_LANE0), (R_E2, i2 - ROUTE_LANE0),
            (R_RANK1, rank1), (R_RANK2, rank2)]
    slab = jnp.zeros((TM, LANES), F32)
    for c, val in cols:
        slab = jnp.where(lane == c, val, slab)
    route_ref[0] = slab


def _outproj(oa, om, x, wo, g, wrh, wrl, br, tri):
    B = oa.shape[0]
    rows = lambda c: pl.BlockSpec((1, TM, c), lambda b, j: (b, j, 0))
    shp = lambda c, dt: jax.ShapeDtypeStruct((B, SEQ, c), dt)
    return pl.pallas_call(
        _outproj_kernel,
        grid=(B, N_ROW_STEPS),
        in_specs=[pl.BlockSpec((1, N_RES, M_STEP, A_WIDTH), lambda b, j: (b, 0, j, 0)),
                  rows(M_WIDTH), rows(D_MODEL), _full(wo), _full(g), _full(wrh), _full(wrl),
                  _full(br), _full(tri)],
        out_specs=[rows(D_MODEL), rows(D_MODEL), rows(LANES), pl.BlockSpec((8, LANES), lambda b, j: (0, 0))],
        out_shape=[shp(D_MODEL, F32), shp(D_MODEL, BF16), shp(LANES, F32),
                   jax.ShapeDtypeStruct((8, LANES), F32)],
        scratch_shapes=[pltpu.VMEM((A_WIDTH // LANES, TM, LANES), F32), pltpu.VMEM((8, LANES), F32)],
        compiler_params=_cparams(("arbitrary", "arbitrary")),
        name="outproj_router",
    )(oa, om, x, wo, g, wrh, wrl, br, tri)


def _expert_kernel(blk_e_ref, n_used_ref, x_ref, wg_ref, wu_ref, wd_ref, y_ref, wg_s, wu_s, wd_s):
    i = pl.program_id(0)

    @pl.when(i < n_used_ref[0])
    def _():
        @pl.when((i == 0) | (blk_e_ref[i] != blk_e_ref[jnp.maximum(i - 1, 0)]))
        def _():
            wg_s[...] = wg_ref[0].astype(BF16)
            wu_s[...] = wu_ref[0].astype(BF16)
            wd_s[...] = wd_ref[0].astype(BF16)

        x = x_ref[...]
        gate = jnp.dot(x, wg_s[...], preferred_element_type=F32)
        up = jnp.dot(x, wu_s[...], preferred_element_type=F32)
        hdn = (gate * jax.nn.sigmoid(gate) * up).astype(BF16)
        y_ref[...] = jnp.dot(hdn, wd_s[...], preferred_element_type=F32).astype(y_ref.dtype)

    @pl.when(i >= n_used_ref[0])
    def _():
        y_ref[...] = jnp.zeros(y_ref.shape, y_ref.dtype)


def _experts(blk_e, n_used, xs, wg, wu, wd):
    n_blocks = xs.shape[0] // MOE_TM
    row_in = lambda i, be, nu: (jnp.minimum(i, nu[0] - 1), 0)
    wsel = lambda i, be, nu: (be[i], 0, 0)
    return pl.pallas_call(
        _expert_kernel,
        grid_spec=pltpu.PrefetchScalarGridSpec(
            num_scalar_prefetch=2,
            grid=(n_blocks,),
            in_specs=[pl.BlockSpec((MOE_TM, D_MODEL), row_in),
                      pl.BlockSpec((1, D_MODEL, EXPERT_FF), wsel),
                      pl.BlockSpec((1, D_MODEL, EXPERT_FF), wsel),
                      pl.BlockSpec((1, EXPERT_FF, D_MODEL), wsel)],
            out_specs=pl.BlockSpec((MOE_TM, D_MODEL), lambda i, be, nu: (i, 0)),
            scratch_shapes=[pltpu.VMEM((D_MODEL, EXPERT_FF), BF16), pltpu.VMEM((D_MODEL, EXPERT_FF), BF16),
                            pltpu.VMEM((EXPERT_FF, D_MODEL), BF16)]),
        out_shape=jax.ShapeDtypeStruct(xs.shape, BF16),
        compiler_params=_cparams(("arbitrary",)),
        name="expert_ffn",
    )(blk_e, n_used, xs, wg, wu, wd)


def _ple_kernel(h_ref, y1_ref, y2_ref, route_ref, p_ref, wp_ref, gp_ref, wg_ref, bg_ref, o_ref):
    route = route_ref[0]
    h = (h_ref[0] + route[:, R_W1:R_W1 + 1] * y1_ref[...].astype(F32)
         + route[:, R_W2:R_W2 + 1] * y2_ref[...].astype(F32))
    e = _rms(jnp.dot(p_ref[0].astype(BF16), wp_ref[...], preferred_element_type=F32), D_MODEL) * gp_ref[...]
    g = jax.nn.sigmoid(jnp.dot(h.astype(BF16), wg_ref[...], preferred_element_type=F32) + bg_ref[...])
    o_ref[0] = h + g * e


def _ple(h, y1, y2, route, p, wp, gp, wg, bg):
    B = h.shape[0]
    rows = lambda c: pl.BlockSpec((1, TM, c), lambda b, j: (b, j, 0))
    flat = pl.BlockSpec((TM, D_MODEL), lambda b, j: (b * N_ROW_STEPS + j, 0))
    return pl.pallas_call(
        _ple_kernel,
        grid=(B, N_ROW_STEPS),
        in_specs=[rows(D_MODEL), flat, flat, rows(LANES), rows(PLE_DIM), _full(wp), _full(gp),
                  _full(wg), _full(bg)],
        out_specs=rows(D_MODEL),
        out_shape=jax.ShapeDtypeStruct((B, SEQ, D_MODEL), F32),
        compiler_params=_cparams(("parallel", "parallel")),
        name="ple_gate",
    )(h, y1, y2, route, p, wp, gp, wg, bg)


def _t5_bucket(dist):
    max_exact = REL_BUCKETS // 2
    n = jnp.maximum(dist, 0)
    nf = jnp.maximum(n, 1).astype(F32)
    large = max_exact + (jnp.log(nf / max_exact) / math.log(REL_MAX_DISTANCE / max_exact)
                         * (REL_BUCKETS - max_exact)).astype(jnp.int32)
    large = jnp.minimum(large, REL_BUCKETS - 1)
    return jnp.where(n < max_exact, n, large)


def _bias_table(rel_bias, local_index, dilation, with_prev):
    loc = np.asarray(local_index)
    delta = loc[:, None] - loc[None, :]
    if with_prev:
        delta = np.concatenate([delta + BAND_BLOCK, delta], axis=1)
    ok = (delta >= 0) & (delta <= BAND_BLOCK)
    bias = rel_bias.astype(F32)[_t5_bucket(jnp.asarray(delta * dilation, jnp.int32))]
    return jnp.where(jnp.asarray(ok)[None], bias.transpose(2, 0, 1), NEG)


def _block_diag_ones(sizes, total):
    g = np.zeros((total, total), np.float32)
    o = 0
    for s, on in sizes:
        if on:
            g[o:o + s, o:o + s] = 1.0
        o += s
    return jnp.asarray(g, BF16)


def _dispatch_plan(route, counts, n_tokens):
    e = route[:, R_E1:R_E2 + 1].astype(jnp.int32)
    rank = route[:, R_RANK1:R_RANK2 + 1].astype(jnp.int32)
    pcounts = (counts + MOE_TM - 1) // MOE_TM * MOE_TM
    pend = jnp.cumsum(pcounts)
    pstart = pend - pcounts
    ids = jnp.arange(N_EXPERTS, dtype=jnp.int32)
    pos = rank + jnp.sum(jnp.where(e[..., None] == ids, pstart, 0), axis=-1)
    n_assign = n_tokens * TOP_K
    shift = int(math.ceil(math.log2(n_assign)))
    keys = (e.reshape(-1) << shift) | jnp.arange(n_assign, dtype=jnp.int32)
    tok_sorted = (jnp.sort(keys) & ((1 << shift) - 1)) // TOP_K
    n_rows = n_assign + N_EXPERTS * MOE_TM
    rows = jnp.arange(n_rows, dtype=jnp.int32)
    row_e = jnp.sum(rows[:, None] >= pend[None, :], axis=-1)
    row_e = jnp.minimum(row_e, N_EXPERTS - 1)
    pick = lambda tbl: jnp.sum(jnp.where(row_e[:, None] == ids, tbl, 0), axis=-1)
    within = rows - pick(pstart)
    src = jnp.clip(pick(jnp.cumsum(counts) - counts) + within, 0, n_assign - 1)
    row_tok = jnp.where(within < pick(counts), tok_sorted[src], rows % n_tokens)
    n_blocks = n_rows // MOE_TM
    n_used = (pend[-1] // MOE_TM).astype(jnp.int32)
    blk_start = jnp.minimum(jnp.arange(n_blocks, dtype=jnp.int32), n_used - 1) * MOE_TM
    blk_e = jnp.minimum(jnp.sum(blk_start[:, None] >= pend[None, :], axis=-1), N_EXPERTS - 1).astype(jnp.int32)
    return row_tok, pos, blk_e, n_used.reshape(1)


def kernel(x, p, rel_bias, norm_mix_gain, w_in, qn_a_gain, kn_a_gain, q_a_gain, w_q_up, kv_a_gain, w_kv_up, qn_nope_gain, qn_rope_gain, kn_nope_gain, kn_rope_gain, w_out, norm_ffn_gain, w_router_group, b_router_group, w_router_expert, b_router_expert, w_exp_gate, w_exp_up, w_exp_down, w_ple_proj, ple_norm_gain, w_ple_gate, b_ple_gate):
    B, S, D = x.shape
    assert (S, D) == (SEQ, D_MODEL) and p.shape[0] == 1
    n_tokens = B * S
    row = lambda a: a.reshape(1, -1).astype(F32)
    zeros = lambda *s: jnp.zeros(s, F32)

    w_in_p = jnp.concatenate([w_in[0, :, :C_KR0], zeros(D, 64), w_in[0, :, C_KR0:], zeros(D, 32)],
                             axis=1).astype(BF16)
    gsum_a = _block_diag_ones([(HEAD_DIM_A, True)] * N_HEADS_A, A_WIDTH)
    gq_a = row(jnp.tile(qn_a_gain[0], N_HEADS_A)) * (HEAD_DIM_A ** -0.5)
    gk_a = row(jnp.tile(kn_a_gain[0], N_HEADS_A))

    wq_p = jnp.pad(w_q_up[0].reshape(Q_LORA, N_HEADS_M, NOPE_DIM + ROPE_DIM),
                   ((0, 0), (0, 0), (0, HEAD_SLOT - NOPE_DIM - ROPE_DIM))).reshape(Q_LORA, -1).astype(BF16)
    wkv = w_kv_up[0].reshape(KV_LORA, N_HEADS_M, NOPE_DIM + V_DIM)
    wk_p = jnp.pad(wkv[..., :NOPE_DIM], ((0, 0), (0, 0), (0, HEAD_SLOT - NOPE_DIM))).reshape(KV_LORA, -1)
    wkv_p = jnp.concatenate([wk_p, wkv[..., NOPE_DIM:].reshape(KV_LORA, -1)], axis=1).astype(BF16)
    gs_q = _block_diag_ones([(NOPE_DIM, True), (ROPE_DIM, True), (32, False)], HEAD_SLOT)
    gs_k = _block_diag_ones([(NOPE_DIM, True), (64, False)], HEAD_SLOT)
    inv_cnt_q = jnp.asarray(np.concatenate([np.full(64, 1 / NOPE_DIM), np.full(32, 1 / ROPE_DIM),
                                            np.ones(32)]).astype(np.float32)).reshape(1, HEAD_SLOT)
    mla_scale = (NOPE_DIM + ROPE_DIM) ** -0.5
    gq_m = row(jnp.concatenate([qn_nope_gain[0], qn_rope_gain[0], zeros(32)])) * mla_scale
    gk_m = row(jnp.concatenate([kn_nope_gain[0], zeros(64)]))
    gkr_m = row(jnp.concatenate([zeros(64), kn_rope_gain[0], zeros(32)]))

    half = ROPE_DIM // 2
    inv = 1.0 / (ROPE_THETA ** (jnp.arange(half, dtype=F32) * 2.0 / ROPE_DIM))
    ang = jnp.arange(S, dtype=jnp.int32).astype(F32)[:, None] * inv[None, :]
    cosv, sinv = jnp.cos(ang), jnp.sin(ang)
    cos_t = jnp.concatenate([jnp.ones((S, 64), F32), cosv, cosv, jnp.ones((S, 32), F32)], -1)
    sin_a = jnp.concatenate([zeros(S, 80), sinv, zeros(S, 32)], -1)
    sin_b = jnp.concatenate([zeros(S, 64), -sinv, zeros(S, 48)], -1)

    loc1 = [16 * a + r for r in range(16) for a in range(8)]
    loc2 = [4 * a + c for c in range(4) for a in range(32)]
    loc3 = list(range(BAND_BLOCK))
    t1 = _bias_table(rel_bias, loc1, 1, True)
    t2 = _bias_table(rel_bias, loc2, 4, True)
    t3 = jnp.concatenate([jnp.full((N_HEADS_A, BAND_BLOCK, BAND_BLOCK), NEG, F32),
                          _bias_table(rel_bias, loc3, 16, False)], axis=-1)

    w_r = jnp.concatenate([w_router_group[0], w_router_expert[0],
                           zeros(D, LANES - N_GROUPS - N_EXPERTS)], axis=1)
    w_r_hi = w_r.astype(BF16)
    w_r_lo = (w_r - w_r_hi.astype(F32)).astype(BF16)
    b_r = row(jnp.concatenate([b_router_group[0], b_router_expert[0], zeros(LANES - N_GROUPS - N_EXPERTS)]))
    tri = jnp.asarray(np.tril(np.ones((TM, TM), np.float32), -1), BF16)

    qa, ka, va, cq, ckv, kr = _inproj(x, row(norm_mix_gain[0]), w_in_p, gsum_a, gq_a, gk_a,
                                      row(q_a_gain[0]), row(kv_a_gain[0]))
    o_a = _dilated(qa, ka, va, t1, t2, t3)
    q_m, k_m, v_m = _mla_prep(cq, ckv, kr, wq_p, wkv_p, gs_q, gs_k, inv_cnt_q, gq_m, gk_m, gkr_m,
                              cos_t, sin_a, sin_b)
    o_m = _mla_attn(q_m, k_m, v_m)
    h1, xn2, route, cnt = _outproj(o_a, o_m, x, w_out[0].astype(BF16), row(norm_ffn_gain[0]),
                                   w_r_hi, w_r_lo, b_r, tri)

    route_flat = route.reshape(n_tokens, LANES)
    counts = cnt[0, ROUTE_LANE0:ROUTE_LANE0 + N_EXPERTS].astype(jnp.int32)
    row_tok, pos, blk_e, n_used = _dispatch_plan(route_flat, counts, n_tokens)
    xs = xn2.reshape(n_tokens, D)[row_tok]
    y = _experts(blk_e, n_used, xs, w_exp_gate[0], w_exp_up[0], w_exp_down[0])
    return _ple(h1, y[pos[:, 0]], y[pos[:, 1]], route, p[0], w_ple_proj[0].astype(BF16),
                row(ple_norm_gain[0]), w_ple_gate[0].astype(BF16), row(b_ple_gate[0]))
```

```python
import functools
import math

import jax
import jax.numpy as jnp
import numpy as np
from jax import lax
from jax.experimental import pallas as pl
from jax.experimental.pallas import tpu as pltpu

F32 = jnp.float32
BF16 = jnp.bfloat16

D_MODEL = 1024
SEQ = 2048
PLE_DIM = 256
EPS = 1e-6
NEG = -1e30
HEAD_DIM_A = 64
A_WIDTH = 512
N_HEADS_A = 8
BAND_BLOCK = 128
REL_BUCKETS = 32
REL_MAX_DISTANCE = 2048
M_WIDTH = 512
V_DIM = 64
N_HEADS_M = 8
Q_LORA = 384
KV_LORA = 256
NOPE_DIM = 64
ROPE_DIM = 32
ROPE_THETA = 10000.0
N_GROUPS = 4
EXPERTS_PER_GROUP = 8
N_EXPERTS = 32
TOP_K = 2
EXPERT_FF = 512

LANES = 128
N_RES = 16
M_SUB = SEQ // N_RES
TM = 512
N_ROW_STEPS = SEQ // TM
M_STEP = TM // N_RES
HEAD_SLOT = LANES
MLA_TQ = 512
MOE_TM = 512
ATTN_SKEW = 2
VMEM_LIMIT = 48 * 1024 * 1024

IN_COLS_PAD = 3 * A_WIDTH + Q_LORA + KV_LORA + LANES
C_Q0, C_K0, C_V0, C_CQ0, C_CKV0, C_KR0 = 0, 512, 1024, 1536, 1920, 2176
ROUTE_LANE0 = N_GROUPS

NT_DIMS = (((1,), (1,)), ((), ()))


def _cparams(sem):
    return pltpu.CompilerParams(dimension_semantics=sem, vmem_limit_bytes=VMEM_LIMIT)


def _full(a):
    return pl.BlockSpec(a.shape, lambda *_: (0,) * a.ndim)


def _rms(x, n):
    return x * lax.rsqrt(jnp.sum(x * x, axis=-1, keepdims=True) * (1.0 / n) + EPS)


def _residue_rows(r):
    return pl.ds(r, M_STEP, stride=N_RES)


def _inproj_kernel(x_ref, g_ref, w_ref, gsum_ref, gq_ref, gk_ref, gcq_ref, gckv_ref,
                   qa_ref, ka_ref, va_ref, cq_ref, ckv_ref, kr_ref, perm_scr):
    xn = (_rms(x_ref[0], D_MODEL) * g_ref[...]).astype(BF16)

    def proj(c0, c1):
        return jnp.dot(xn, w_ref[:, c0:c1], preferred_element_type=F32)

    def head_norm(t, gain_ref):
        ss = jnp.dot((t * t).astype(BF16), gsum_ref[...], preferred_element_type=F32)
        return t * lax.rsqrt(ss * (1.0 / HEAD_DIM_A) + EPS) * gain_ref[...]

    def put_residue(ref, val):
        for c in range(A_WIDTH // LANES):
            perm_scr[c] = val[:, c * LANES:(c + 1) * LANES]
        for r in range(N_RES):
            for c in range(A_WIDTH // LANES):
                ref[0, r, :, c * LANES:(c + 1) * LANES] = perm_scr[c, _residue_rows(r), :].astype(ref.dtype)

    put_residue(qa_ref, head_norm(proj(C_Q0, C_K0), gq_ref))
    put_residue(ka_ref, head_norm(proj(C_K0, C_V0), gk_ref))
    put_residue(va_ref, proj(C_V0, C_CQ0))
    cq_ref[0] = (_rms(proj(C_CQ0, C_CKV0), Q_LORA) * gcq_ref[...]).astype(BF16)
    ckv_ref[0] = (_rms(proj(C_CKV0, C_KR0), KV_LORA) * gckv_ref[...]).astype(BF16)
    kr_ref[0] = proj(C_KR0, IN_COLS_PAD)


def _inproj(x, g, w_in_p, gsum, gq, gk, gcq, gckv):
    B = x.shape[0]
    res = lambda: (jax.ShapeDtypeStruct((B, N_RES, M_SUB, A_WIDTH), BF16),
                   pl.BlockSpec((1, N_RES, M_STEP, A_WIDTH), lambda b, j: (b, 0, j, 0)))
    nat = lambda c, dt: (jax.ShapeDtypeStruct((B, SEQ, c), dt),
                         pl.BlockSpec((1, TM, c), lambda b, j: (b, j, 0)))
    outs = [res(), res(), res(), nat(Q_LORA, BF16), nat(KV_LORA, BF16), nat(LANES, F32)]
    return pl.pallas_call(
        _inproj_kernel,
        grid=(B, N_ROW_STEPS),
        in_specs=[pl.BlockSpec((1, TM, D_MODEL), lambda b, j: (b, j, 0)),
                  _full(g), _full(w_in_p), _full(gsum), _full(gq), _full(gk), _full(gcq), _full(gckv)],
        out_specs=[o[1] for o in outs],
        out_shape=[o[0] for o in outs],
        scratch_shapes=[pltpu.VMEM((A_WIDTH // LANES, TM, LANES), F32)],
        compiler_params=_cparams(("parallel", "parallel")),
        name="inproj",
    )(x, g, w_in_p, gsum, gq, gk, gcq, gckv)


def _mla_prep_kernel(cq_ref, ckv_ref, kr_ref, wq_ref, wkv_ref, gsq_ref, gsk_ref, icq_ref,
                     gq_ref, gk_ref, gkr_ref, cos_ref, sa_ref, sb_ref, q_ref, k_ref, v_ref):
    cos, sa, sb = cos_ref[...], sa_ref[...], sb_ref[...]
    lane = lax.broadcasted_iota(jnp.int32, (TM, LANES), 1)

    def rope(t):
        return t * cos + pltpu.roll(t, 16, 1) * sa + pltpu.roll(t, LANES - 16, 1) * sb

    q = jnp.dot(cq_ref[0], wq_ref[...], preferred_element_type=F32)
    kv = jnp.dot(ckv_ref[0], wkv_ref[...], preferred_element_type=F32)
    k_rope = rope(_rms(kr_ref[0], ROPE_DIM) * gkr_ref[...])
    for h in range(N_HEADS_M):
        sl = slice(h * HEAD_SLOT, (h + 1) * HEAD_SLOT)
        qh = q[:, sl]
        ss = jnp.dot((qh * qh).astype(BF16), gsq_ref[...], preferred_element_type=F32)
        q_ref[0, :, sl] = rope(qh * lax.rsqrt(ss * icq_ref[...] + EPS) * gq_ref[...]).astype(BF16)
        kh = kv[:, sl]
        ssk = jnp.dot((kh * kh).astype(BF16), gsk_ref[...], preferred_element_type=F32)
        kn = kh * lax.rsqrt(ssk * (1.0 / NOPE_DIM) + EPS) * gk_ref[...] + k_rope
        k_ref[0, :, sl] = kn.astype(BF16)
    for hp in range(N_HEADS_M // 2):
        v_pair = kv[:, N_HEADS_M * HEAD_SLOT + hp * LANES:N_HEADS_M * HEAD_SLOT + (hp + 1) * LANES]
        v_ref[0, :, (2 * hp) * HEAD_SLOT:(2 * hp + 1) * HEAD_SLOT] = jnp.where(lane < V_DIM, v_pair, 1.0).astype(BF16)
        v_ref[0, :, (2 * hp + 1) * HEAD_SLOT:(2 * hp + 2) * HEAD_SLOT] = jnp.where(lane < V_DIM, 1.0, v_pair).astype(BF16)


def _mla_prep(cq, ckv, kr, wq_p, wkv_p, gsq, gsk, icq, gq, gk, gkr, cos, sa, sb):
    B = cq.shape[0]
    rows = lambda c: pl.BlockSpec((1, TM, c), lambda b, j: (b, j, 0))
    tab = pl.BlockSpec((TM, LANES), lambda b, j: (j, 0))
    wide = N_HEADS_M * HEAD_SLOT
    return pl.pallas_call(
        _mla_prep_kernel,
        grid=(B, N_ROW_STEPS),
        in_specs=[rows(Q_LORA), rows(KV_LORA), rows(LANES), _full(wq_p), _full(wkv_p), _full(gsq),
                  _full(gsk), _full(icq), _full(gq), _full(gk), _full(gkr), tab, tab, tab],
        out_specs=[rows(wide)] * 3,
        out_shape=[jax.ShapeDtypeStruct((B, SEQ, wide), BF16)] * 3,
        compiler_params=_cparams(("parallel", "parallel")),
        name="mla_prep",
    )(cq, ckv, kr, wq_p, wkv_p, gsq, gsk, icq, gq, gk, gkr, cos, sa, sb)


def _mla_attn_kernel(q_ref, k_ref, v_ref, o_ref):
    n_q = SEQ // MLA_TQ
    row = lax.broadcasted_iota(jnp.int32, (MLA_TQ, MLA_TQ), 0)
    col = lax.broadcasted_iota(jnp.int32, (MLA_TQ, MLA_TQ), 1)
    lane = lax.broadcasted_iota(jnp.int32, (MLA_TQ, LANES), 1)
    heads = [slice(hh * HEAD_SLOT, (hh + 1) * HEAD_SLOT) for hh in range(2)]

    def probs(i, hh):
        n_keys = (i + 1) * MLA_TQ
        s = lax.dot_general(q_ref[0, i * MLA_TQ:n_keys, heads[hh]], k_ref[0, 0:n_keys, heads[hh]],
                            NT_DIMS, preferred_element_type=F32)
        diag = jnp.where(col <= row, s[:, n_keys - MLA_TQ:], NEG)
        s = diag if i == 0 else jnp.concatenate([s[:, :n_keys - MLA_TQ], diag], axis=1)
        return jnp.exp(s - jnp.max(s, axis=-1, keepdims=True)).astype(BF16)

    def values(i, hh, p):
        return jnp.dot(p, v_ref[0, 0:(i + 1) * MLA_TQ, heads[hh]], preferred_element_type=F32)

    units = [(i, hh) for i in range(n_q) for hh in range(2)]
    acc, pending = {}, None
    for u in units:
        p = probs(*u)
        if pending is not None:
            acc[pending[0]] = values(*pending[0], pending[1])
        pending = (u, p)
    acc[pending[0]] = values(*pending[0], pending[1])
    for i in range(n_q):
        num = jnp.where(lane < V_DIM, acc[(i, 0)], acc[(i, 1)])
        den = pltpu.roll(jnp.where(lane < V_DIM, acc[(i, 1)], acc[(i, 0)]), V_DIM, 1)
        o_ref[0, i * MLA_TQ:(i + 1) * MLA_TQ, :] = (num / den).astype(BF16)


def _mla_attn(q, k, v):
    B = q.shape[0]
    pair = lambda c: pl.BlockSpec((1, SEQ, c), lambda b, h: (b, 0, h))
    return pl.pallas_call(
        _mla_attn_kernel,
        grid=(B, N_HEADS_M // 2),
        in_specs=[pair(2 * HEAD_SLOT)] * 3,
        out_specs=pair(2 * V_DIM),
        out_shape=jax.ShapeDtypeStruct((B, SEQ, M_WIDTH), BF16),
        compiler_params=_cparams(("parallel", "parallel")),
        name="mla_attn",
    )(q, k, v)


N_BLK = SEQ // BAND_BLOCK


def _dilated_kernel(q_ref, k_ref, v_ref, t1_ref, t2_ref, t3_ref, o_ref,
                    q1, k1, v1, q2, k2, v2, ob1, ls1, ob2, ls2, ob3, ls3):
    bb = BAND_BLOCK
    lane = lax.broadcasted_iota(jnp.int32, (bb, LANES), 1)
    lane_row = lax.broadcasted_iota(jnp.int32, (1, LANES), 1)
    own = [(lane_row < HEAD_DIM_A).astype(BF16), (lane_row >= HEAD_DIM_A).astype(BF16)]
    rows = lambda lo, hi: slice(lo * bb, hi * bb)

    for src, d1, d2 in ((q_ref, q1, q2), (k_ref, k1, k2), (v_ref, v1, v2)):
        for n2 in range(N_BLK // 2):
            pieces = [src[0, r, 16 * n2:16 * n2 + 16, :].astype(F32) for r in range(N_RES)]
            d1[rows(2 * n2, 2 * n2 + 1), :] = jnp.concatenate([p[0:8] for p in pieces], axis=0).astype(BF16)
            d1[rows(2 * n2 + 1, 2 * n2 + 2), :] = jnp.concatenate([p[8:16] for p in pieces], axis=0).astype(BF16)
        for r4 in range(4):
            for n in range(4):
                d2[rows(r4 * 4 + n, r4 * 4 + n + 1), :] = jnp.concatenate(
                    [src[0, r4 + 4 * c, 32 * n:32 * n + 32, :] for c in range(4)], axis=0)

    blocks = []
    for idx in range(N_BLK):
        lo = idx - 1 if idx > 0 else idx
        blocks.append((q1, k1, v1, rows(lo, idx + 1), t1_ref, ob1, ls1, idx))
    for idx in range(N_BLK):
        lo = idx - 1 if idx % 4 else idx
        blocks.append((q2, k2, v2, rows(lo, idx + 1), t2_ref, ob2, ls2, idx))
    for r in range(N_RES):
        blocks.append((None, None, None, r, t3_ref, ob3, ls3, r))

    def scores_and_probs(blk, hh):
        qd, kd, _, kv_rows, t_ref, _, _, idx = blk
        q = qd[rows(idx, idx + 1), :] if qd is not None else q_ref[0, idx]
        keys = kd[kv_rows, :] if kd is not None else k_ref[0, kv_rows]
        n_keys = keys.shape[0]
        s = lax.dot_general(q * own[hh], keys, NT_DIMS, preferred_element_type=F32)
        s = s + t_ref[hh, :, 2 * bb - n_keys:2 * bb]
        m = jnp.max(s, axis=-1, keepdims=True)
        return m, jnp.exp(s - m).astype(BF16)

    def weighted_values(blk, hh, e):
        _, _, vd, kv_rows, _, _, _, _ = blk
        vals = vd[kv_rows, :] if vd is not None else v_ref[0, kv_rows]
        return jnp.dot(e, vals * own[hh] + own[1 - hh], preferred_element_type=F32)

    def finish(blk, m, acc):
        o_dst, l_dst, idx = blk[5], blk[6], blk[7]
        num = jnp.where(lane < HEAD_DIM_A, acc[0], acc[1])
        den = pltpu.roll(jnp.where(lane < HEAD_DIM_A, acc[1], acc[0]), HEAD_DIM_A, 1)
        o_dst[idx] = num / den
        l_dst[idx] = jnp.where(lane < HEAD_DIM_A, m[0], m[1]) + jnp.log(den)

    stage = {}
    for t in range(len(blocks) + ATTN_SKEW):
        if t < len(blocks):
            stage[t] = [scores_and_probs(blocks[t], hh) for hh in range(2)]
        d = t - ATTN_SKEW
        if d >= 0:
            acc = [weighted_values(blocks[d], hh, stage[d][hh][1]) for hh in range(2)]
            finish(blocks[d], [stage[d][hh][0] for hh in range(2)], acc)
            del stage[d]

    for r in range(N_RES):
        r4, c = r % 4, r // 4
        gather1 = lambda ref: jnp.concatenate([ref[n, 8 * r:8 * r + 8, :] for n in range(N_BLK)], axis=0)
        gather2 = lambda ref: jnp.concatenate(
            [ref[r4 * 4 + n, 32 * c:32 * c + 32, :] for n in range(4)], axis=0)
        o_b = [gather1(ob1), gather2(ob2), ob3[r]]
        l_b = [gather1(ls1), gather2(ls2), ls3[r]]
        top = jnp.maximum(jnp.maximum(l_b[0], l_b[1]), l_b[2])
        w_b = [jnp.exp(l - top) for l in l_b]
        num = w_b[0] * o_b[0] + w_b[1] * o_b[1] + w_b[2] * o_b[2]
        o_ref[0, r] = (num / (w_b[0] + w_b[1] + w_b[2])).astype(BF16)


def _dilated(qa, ka, va, t1, t2, t3):
    B = qa.shape[0]
    blk = pl.BlockSpec((1, N_RES, M_SUB, LANES), lambda b, h: (b, 0, 0, h))
    tab = lambda t: pl.BlockSpec((2,) + t.shape[1:], lambda b, h: (h, 0, 0))
    blocked_bf16 = pltpu.VMEM((N_BLK * BAND_BLOCK, LANES), BF16)
    blocked_f32 = pltpu.VMEM((N_BLK, BAND_BLOCK, LANES), F32)
    return pl.pallas_call(
        _dilated_kernel,
        grid=(B, N_HEADS_A // 2),
        in_specs=[blk, blk, blk, tab(t1), tab(t2), tab(t3)],
        out_specs=blk,
        out_shape=jax.ShapeDtypeStruct((B, N_RES, M_SUB, A_WIDTH), BF16),
        scratch_shapes=[blocked_bf16] * 6 + [blocked_f32] * 6,
        compiler_params=_cparams(("parallel", "parallel")),
        name="dilated_attn",
    )(qa, ka, va, t1, t2, t3)


R_W1, R_W2, R_E1, R_E2, R_RANK1, R_RANK2 = range(6)


def _outproj_kernel(oa_ref, om_ref, x_ref, wo_ref, g_ref, wrh_ref, wrl_ref, br_ref, tri_ref,
                    h_ref, xn_ref, route_ref, cnt_ref, perm_scr, carry_scr):
    @pl.when((pl.program_id(0) == 0) & (pl.program_id(1) == 0))
    def _():
        carry_scr[...] = jnp.zeros(carry_scr.shape, F32)

    n_chunks = A_WIDTH // LANES
    for r in range(N_RES):
        for c in range(n_chunks):
            perm_scr[c, _residue_rows(r), :] = oa_ref[0, r, :, c * LANES:(c + 1) * LANES].astype(F32)
    oa = jnp.concatenate([perm_scr[c] for c in range(n_chunks)], axis=1).astype(BF16)
    h = (x_ref[0] + jnp.dot(oa, wo_ref[0:A_WIDTH, :], preferred_element_type=F32)
         + jnp.dot(om_ref[0], wo_ref[A_WIDTH:, :], preferred_element_type=F32))
    h_ref[0] = h
    xn = _rms(h, D_MODEL) * g_ref[...]
    xn_ref[0] = xn.astype(BF16)
    hi = xn.astype(BF16)
    lo = (xn - hi.astype(F32)).astype(BF16)
    lg = (jnp.dot(hi, wrh_ref[...], preferred_element_type=F32)
          + jnp.dot(lo, wrh_ref[...], preferred_element_type=F32)
          + jnp.dot(hi, wrl_ref[...], preferred_element_type=F32)) + br_ref[...]

    lane = lax.broadcasted_iota(jnp.int32, (TM, LANES), 1).astype(F32)
    rmax = lambda t: jnp.max(t, axis=-1, keepdims=True)
    rmin = lambda t: jnp.min(t, axis=-1, keepdims=True)
    rsum = lambda t: jnp.sum(t, axis=-1, keepdims=True)
    none = float(LANES)

    gmask = lane < N_GROUPS
    gl = jnp.where(gmask, lg, NEG)
    ge = jnp.where(gmask, jnp.exp(gl - rmax(gl)), 0.0)
    gsum = rsum(ge)
    gprob = ge / gsum
    g_gate = 1.0 / gsum
    g_idx = rmin(jnp.where(gmask & (gprob == g_gate), lane, none))
    lo_lane = ROUTE_LANE0 + EXPERTS_PER_GROUP * g_idx
    emask = (lane >= lo_lane) & (lane < lo_lane + EXPERTS_PER_GROUP)
    el = jnp.where(emask, lg, NEG)
    ee = jnp.where(emask, jnp.exp(el - rmax(el)), 0.0)
    esum = rsum(ee)
    eprob = ee / esum
    p1 = 1.0 / esum
    i1 = rmin(jnp.where(emask & (eprob == p1), lane, none))
    rest = jnp.where(emask & (lane != i1), eprob, -1.0)
    p2 = rmax(rest)
    i2 = rmin(jnp.where(rest == p2, lane, none))
    den = p1 + p2
    w1 = g_gate * (p1 / den)
    w2 = g_gate * (p2 / den)
    onehot = ((lane == i1) | (lane == i2)).astype(F32)
    before = jnp.dot(tri_ref[...], onehot.astype(BF16), preferred_element_type=F32) + carry_scr[0:1, :]
    rank1 = rsum(jnp.where(lane == i1, before, 0.0))
    rank2 = rsum(jnp.where(lane == i2, before, 0.0))
    carry_scr[...] = carry_scr[...] + jnp.sum(onehot, axis=0, keepdims=True)
    cnt_ref[...] = carry_scr[...]

    cols = [(R_W1, w1), (R_W2, w2), (R_E1, i1 - ROUTE_LANE0), (R_E2, i2 - ROUTE_LANE0),
            (R_RANK1, rank1), (R_RANK2, rank2)]
    slab = jnp.zeros((TM, LANES), F32)
    for c, val in cols:
        slab = jnp.where(lane == c, val, slab)
    route_ref[0] = slab


def _outproj(oa, om, x, wo, g, wrh, wrl, br, tri):
    B = oa.shape[0]
    rows = lambda c: pl.BlockSpec((1, TM, c), lambda b, j: (b, j, 0))
    shp = lambda c, dt: jax.ShapeDtypeStruct((B, SEQ, c), dt)
    return pl.pallas_call(
        _outproj_kernel,
        grid=(B, N_ROW_STEPS),
        in_specs=[pl.BlockSpec((1, N_RES, M_STEP, A_WIDTH), lambda b, j: (b, 0, j, 0)),
                  rows(M_WIDTH), rows(D_MODEL), _full(wo), _full(g), _full(wrh), _full(wrl),
                  _full(br), _full(tri)],
        out_specs=[rows(D_MODEL), rows(D_MODEL), rows(LANES), pl.BlockSpec((8, LANES), lambda b, j: (0, 0))],
        out_shape=[shp(D_MODEL, F32), shp(D_MODEL, BF16), shp(LANES, F32),
                   jax.ShapeDtypeStruct((8, LANES), F32)],
        scratch_shapes=[pltpu.VMEM((A_WIDTH // LANES, TM, LANES), F32), pltpu.VMEM((8, LANES), F32)],
        compiler_params=_cparams(("arbitrary", "arbitrary")),
        name="outproj_router",
    )(oa, om, x, wo, g, wrh, wrl, br, tri)


def _expert_kernel(blk_e_ref, n_used_ref, x_ref, wg_ref, wu_ref, wd_ref, y_ref, wg_s, wu_s, wd_s):
    i = pl.program_id(0)

    @pl.when(i < n_used_ref[0])
    def _():
        @pl.when((i == 0) | (blk_e_ref[i] != blk_e_ref[jnp.maximum(i - 1, 0)]))
        def _():
            wg_s[...] = wg_ref[0].astype(BF16)
            wu_s[...] = wu_ref[0].astype(BF16)
            wd_s[...] = wd_ref[0].astype(BF16)

        x = x_ref[...]
        gate = jnp.dot(x, wg_s[...], preferred_element_type=F32)
        up = jnp.dot(x, wu_s[...], preferred_element_type=F32)
        hdn = (gate * jax.nn.sigmoid(gate) * up).astype(BF16)
        y_ref[...] = jnp.dot(hdn, wd_s[...], preferred_element_type=F32).astype(y_ref.dtype)

    @pl.when(i >= n_used_ref[0])
    def _():
        y_ref[...] = jnp.zeros(y_ref.shape, y_ref.dtype)


def _experts(blk_e, n_used, xs, wg, wu, wd):
    n_blocks = xs.shape[0] // MOE_TM
    row_in = lambda i, be, nu: (jnp.minimum(i, nu[0] - 1), 0)
    wsel = lambda i, be, nu: (be[i], 0, 0)
    return pl.pallas_call(
        _expert_kernel,
        grid_spec=pltpu.PrefetchScalarGridSpec(
            num_scalar_prefetch=2,
            grid=(n_blocks,),
            in_specs=[pl.BlockSpec((MOE_TM, D_MODEL), row_in),
                      pl.BlockSpec((1, D_MODEL, EXPERT_FF), wsel),
                      pl.BlockSpec((1, D_MODEL, EXPERT_FF), wsel),
                      pl.BlockSpec((1, EXPERT_FF, D_MODEL), wsel)],
            out_specs=pl.BlockSpec((MOE_TM, D_MODEL), lambda i, be, nu: (i, 0)),
            scratch_shapes=[pltpu.VMEM((D_MODEL, EXPERT_FF), BF16), pltpu.VMEM((D_MODEL, EXPERT_FF), BF16),
                            pltpu.VMEM((EXPERT_FF, D_MODEL), BF16)]),
        out_shape=jax.ShapeDtypeStruct(xs.shape, BF16),
        compiler_params=_cparams(("arbitrary",)),
        name="expert_ffn",
    )(blk_e, n_used, xs, wg, wu, wd)


def _ple_kernel(h_ref, y1_ref, y2_ref, route_ref, p_ref, wp_ref, gp_ref, wg_ref, bg_ref, o_ref):
    route = route_ref[0]
    h = (h_ref[0] + route[:, R_W1:R_W1 + 1] * y1_ref[...].astype(F32)
         + route[:, R_W2:R_W2 + 1] * y2_ref[...].astype(F32))
    e = _rms(jnp.dot(p_ref[0].astype(BF16), wp_ref[...], preferred_element_type=F32), D_MODEL) * gp_ref[...]
    g = jax.nn.sigmoid(jnp.dot(h.astype(BF16), wg_ref[...], preferred_element_type=F32) + bg_ref[...])
    o_ref[0] = h + g * e


def _ple(h, y1, y2, route, p, wp, gp, wg, bg):
    B = h.shape[0]
    rows = lambda c: pl.BlockSpec((1, TM, c), lambda b, j: (b, j, 0))
    flat = pl.BlockSpec((TM, D_MODEL), lambda b, j: (b * N_ROW_STEPS + j, 0))
    return pl.pallas_call(
        _ple_kernel,
        grid=(B, N_ROW_STEPS),
        in_specs=[rows(D_MODEL), flat, flat, rows(LANES), rows(PLE_DIM), _full(wp), _full(gp),
                  _full(wg), _full(bg)],
        out_specs=rows(D_MODEL),
        out_shape=jax.ShapeDtypeStruct((B, SEQ, D_MODEL), F32),
        compiler_params=_cparams(("parallel", "parallel")),
        name="ple_gate",
    )(h, y1, y2, route, p, wp, gp, wg, bg)


def _t5_bucket(dist):
    max_exact = REL_BUCKETS // 2
    n = jnp.maximum(dist, 0)
    nf = jnp.maximum(n, 1).astype(F32)
    large = max_exact + (jnp.log(nf / max_exact) / math.log(REL_MAX_DISTANCE / max_exact)
                         * (REL_BUCKETS - max_exact)).astype(jnp.int32)
    large = jnp.minimum(large, REL_BUCKETS - 1)
    return jnp.where(n < max_exact, n, large)


def _bias_table(rel_bias, local_index, dilation, with_prev):
    loc = np.asarray(local_index)
    delta = loc[:, None] - loc[None, :]
    if with_prev:
        delta = np.concatenate([delta + BAND_BLOCK, delta], axis=1)
    ok = (delta >= 0) & (delta <= BAND_BLOCK)
    bias = rel_bias.astype(F32)[_t5_bucket(jnp.asarray(delta * dilation, jnp.int32))]
    return jnp.where(jnp.asarray(ok)[None], bias.transpose(2, 0, 1), NEG)


def _block_diag_ones(sizes, total):
    g = np.zeros((total, total), np.float32)
    o = 0
    for s, on in sizes:
        if on:
            g[o:o + s, o:o + s] = 1.0
        o += s
    return jnp.asarray(g, BF16)


def _dispatch_plan(route, counts, n_tokens):
    e = route[:, R_E1:R_E2 + 1].astype(jnp.int32)
    rank = route[:, R_RANK1:R_RANK2 + 1].astype(jnp.int32)
    pcounts = (counts + MOE_TM - 1) // MOE_TM * MOE_TM
    pend = jnp.cumsum(pcounts)
    pstart = pend - pcounts
    ids = jnp.arange(N_EXPERTS, dtype=jnp.int32)
    pos = rank + jnp.sum(jnp.where(e[..., None] == ids, pstart, 0), axis=-1)
    n_assign = n_tokens * TOP_K
    shift = int(math.ceil(math.log2(n_assign)))
    keys = (e.reshape(-1) << shift) | jnp.arange(n_assign, dtype=jnp.int32)
    tok_sorted = (jnp.sort(keys) & ((1 << shift) - 1)) // TOP_K
    n_rows = n_assign + N_EXPERTS * MOE_TM
    rows = jnp.arange(n_rows, dtype=jnp.int32)
    row_e = jnp.sum(rows[:, None] >= pend[None, :], axis=-1)
    row_e = jnp.minimum(row_e, N_EXPERTS - 1)
    pick = lambda tbl: jnp.sum(jnp.where(row_e[:, None] == ids, tbl, 0), axis=-1)
    within = rows - pick(pstart)
    src = jnp.clip(pick(jnp.cumsum(counts) - counts) + within, 0, n_assign - 1)
    row_tok = jnp.where(within < pick(counts), tok_sorted[src], rows % n_tokens)
    n_blocks = n_rows // MOE_TM
    n_used = (pend[-1] // MOE_TM).astype(jnp.int32)
    blk_start = jnp.minimum(jnp.arange(n_blocks, dtype=jnp.int32), n_used - 1) * MOE_TM
    blk_e = jnp.minimum(jnp.sum(blk_start[:, None] >= pend[None, :], axis=-1), N_EXPERTS - 1).astype(jnp.int32)
    return row_tok, pos, blk_e, n_used.reshape(1)


def kernel(x, p, rel_bias, norm_mix_gain, w_in, qn_a_gain, kn_a_gain, q_a_gain, w_q_up, kv_a_gain, w_kv_up, qn_nope_gain, qn_rope_gain, kn_nope_gain, kn_rope_gain, w_out, norm_ffn_gain, w_router_group, b_router_group, w_router_expert, b_router_expert, w_exp_gate, w_exp_up, w_exp_down, w_ple_proj, ple_norm_gain, w_ple_gate, b_ple_gate):
    B, S, D = x.shape
    assert (S, D) == (SEQ, D_MODEL) and p.shape[0] == 1
    n_tokens = B * S
    row = lambda a: a.reshape(1, -1).astype(F32)
    zeros = lambda *s: jnp.zeros(s, F32)

    w_in_p = jnp.concatenate([w_in[0, :, :C_KR0], zeros(D, 64), w_in[0, :, C_KR0:], zeros(D, 32)],
                             axis=1).astype(BF16)
    gsum_a = _block_diag_ones([(HEAD_DIM_A, True)] * N_HEADS_A, A_WIDTH)
    gq_a = row(jnp.tile(qn_a_gain[0], N_HEADS_A)) * (HEAD_DIM_A ** -0.5)
    gk_a = row(jnp.tile(kn_a_gain[0], N_HEADS_A))

    wq_p = jnp.pad(w_q_up[0].reshape(Q_LORA, N_HEADS_M, NOPE_DIM + ROPE_DIM),
                   ((0, 0), (0, 0), (0, HEAD_SLOT - NOPE_DIM - ROPE_DIM))).reshape(Q_LORA, -1).astype(BF16)
    wkv = w_kv_up[0].reshape(KV_LORA, N_HEADS_M, NOPE_DIM + V_DIM)
    wk_p = jnp.pad(wkv[..., :NOPE_DIM], ((0, 0), (0, 0), (0, HEAD_SLOT - NOPE_DIM))).reshape(KV_LORA, -1)
    wkv_p = jnp.concatenate([wk_p, wkv[..., NOPE_DIM:].reshape(KV_LORA, -1)], axis=1).astype(BF16)
    gs_q = _block_diag_ones([(NOPE_DIM, True), (ROPE_DIM, True), (32, False)], HEAD_SLOT)
    gs_k = _block_diag_ones([(NOPE_DIM, True), (64, False)], HEAD_SLOT)
    inv_cnt_q = jnp.asarray(np.concatenate([np.full(64, 1 / NOPE_DIM), np.full(32, 1 / ROPE_DIM),
                                            np.ones(32)]).astype(np.float32)).reshape(1, HEAD_SLOT)
    mla_scale = (NOPE_DIM + ROPE_DIM) ** -0.5
    gq_m = row(jnp.concatenate([qn_nope_gain[0], qn_rope_gain[0], zeros(32)])) * mla_scale
    gk_m = row(jnp.concatenate([kn_nope_gain[0], zeros(64)]))
    gkr_m = row(jnp.concatenate([zeros(64), kn_rope_gain[0], zeros(32)]))

    half = ROPE_DIM // 2
    inv = 1.0 / (ROPE_THETA ** (jnp.arange(half, dtype=F32) * 2.0 / ROPE_DIM))
    ang = jnp.arange(S, dtype=jnp.int32).astype(F32)[:, None] * inv[None, :]
    cosv, sinv = jnp.cos(ang), jnp.sin(ang)
    cos_t = jnp.concatenate([jnp.ones((S, 64), F32), cosv, cosv, jnp.ones((S, 32), F32)], -1)
    sin_a = jnp.concatenate([zeros(S, 80), sinv, zeros(S, 32)], -1)
    sin_b = jnp.concatenate([zeros(S, 64), -sinv, zeros(S, 48)], -1)

    loc1 = [16 * a + r for r in range(16) for a in range(8)]
    loc2 = [4 * a + c for c in range(4) for a in range(32)]
    loc3 = list(range(BAND_BLOCK))
    t1 = _bias_table(rel_bias, loc1, 1, True)
    t2 = _bias_table(rel_bias, loc2, 4, True)
    t3 = jnp.concatenate([jnp.full((N_HEADS_A, BAND_BLOCK, BAND_BLOCK), NEG, F32),
                          _bias_table(rel_bias, loc3, 16, False)], axis=-1)

    w_r = jnp.concatenate([w_router_group[0], w_router_expert[0],
                           zeros(D, LANES - N_GROUPS - N_EXPERTS)], axis=1)
    w_r_hi = w_r.astype(BF16)
    w_r_lo = (w_r - w_r_hi.astype(F32)).astype(BF16)
    b_r = row(jnp.concatenate([b_router_group[0], b_router_expert[0], zeros(LANES - N_GROUPS - N_EXPERTS)]))
    tri = jnp.asarray(np.tril(np.ones((TM, TM), np.float32), -1), BF16)

    qa, ka, va, cq, ckv, kr = _inproj(x, row(norm_mix_gain[0]), w_in_p, gsum_a, gq_a, gk_a,
                                      row(q_a_gain[0]), row(kv_a_gain[0]))
    o_a = _dilated(qa, ka, va, t1, t2, t3)
    q_m, k_m, v_m = _mla_prep(cq, ckv, kr, wq_p, wkv_p, gs_q, gs_k, inv_cnt_q, gq_m, gk_m, gkr_m,
                              cos_t, sin_a, sin_b)
    o_m = _mla_attn(q_m, k_m, v_m)
    h1, xn2, route, cnt = _outproj(o_a, o_m, x, w_out[0].astype(BF16), row(norm_ffn_gain[0]),
                                   w_r_hi, w_r_lo, b_r, tri)

    route_flat = route.reshape(n_tokens, LANES)
    counts = cnt[0, ROUTE_LANE0:ROUTE_LANE0 + N_EXPERTS].astype(jnp.int32)
    row_tok, pos, blk_e, n_used = _dispatch_plan(route_flat, counts, n_tokens)
    xs = xn2.reshape(n_tokens, D)[row_tok]
    y = _experts(blk_e, n_used, xs, w_exp_gate[0], w_exp_up[0], w_exp_down[0])
    return _ple(h1, y[pos[:, 0]], y[pos[:, 1]], route, p[0], w_ple_proj[0].astype(BF16),
                row(ple_norm_gain[0]), w_ple_gate[0].astype(BF16), row(b_ple_gate[0]))
```

```python
import functools
import math

import jax
import jax.numpy as jnp
import numpy as np
from jax import lax
from jax.experimental import pallas as pl
from jax.experimental.pallas import tpu as pltpu

F32 = jnp.float32
BF16 = jnp.bfloat16

D_MODEL = 1024
SEQ = 2048
PLE_DIM = 256
EPS = 1e-6
NEG = -1e30
HEAD_DIM_A = 64
A_WIDTH = 512
N_HEADS_A = 8
BAND_BLOCK = 128
REL_BUCKETS = 32
REL_MAX_DISTANCE = 2048
M_WIDTH = 512
V_DIM = 64
N_HEADS_M = 8
Q_LORA = 384
KV_LORA = 256
NOPE_DIM = 64
ROPE_DIM = 32
ROPE_THETA = 10000.0
N_GROUPS = 4
EXPERTS_PER_GROUP = 8
N_EXPERTS = 32
TOP_K = 2
EXPERT_FF = 512

LANES = 128
N_RES = 16
M_SUB = SEQ // N_RES
TM = 512
N_ROW_STEPS = SEQ // TM
M_STEP = TM // N_RES
HEAD_SLOT = LANES
MLA_TQ = 512
MOE_TM = 512
ATTN_SKEW = 2
VMEM_LIMIT = 48 * 1024 * 1024

IN_COLS_PAD = 3 * A_WIDTH + Q_LORA + KV_LORA + LANES
C_Q0, C_K0, C_V0, C_CQ0, C_CKV0, C_KR0 = 0, 512, 1024, 1536, 1920, 2176

NT_DIMS = (((1,), (1,)), ((), ()))


def _cparams(sem):
    return pltpu.CompilerParams(dimension_semantics=sem, vmem_limit_bytes=VMEM_LIMIT)


def _full(a):
    return pl.BlockSpec(a.shape, lambda *_: (0,) * a.ndim)


def _rms(x, n):
    return x * lax.rsqrt(jnp.sum(x * x, axis=-1, keepdims=True) * (1.0 / n) + EPS)


def _residue_rows(r):
    return pl.ds(r, M_STEP, stride=N_RES)


def _inproj_kernel(x_ref, g_ref, w_ref, gsum_ref, gq_ref, gk_ref, gcq_ref, gckv_ref,
                   qa_ref, ka_ref, va_ref, cq_ref, ckv_ref, kr_ref, perm_scr):
    xn = (_rms(x_ref[0], D_MODEL) * g_ref[...]).astype(BF16)

    def proj(c0, c1):
        return jnp.dot(xn, w_ref[:, c0:c1], preferred_element_type=F32)

    def head_norm(t, gain_ref):
        ss = jnp.dot((t * t).astype(BF16), gsum_ref[...], preferred_element_type=F32)
        return t * lax.rsqrt(ss * (1.0 / HEAD_DIM_A) + EPS) * gain_ref[...]

    def put_residue(ref, val):
        for c in range(A_WIDTH // LANES):
            perm_scr[c] = val[:, c * LANES:(c + 1) * LANES]
        for r in range(N_RES):
            for c in range(A_WIDTH // LANES):
                ref[0, r, :, c * LANES:(c + 1) * LANES] = perm_scr[c, _residue_rows(r), :].astype(ref.dtype)

    put_residue(qa_ref, head_norm(proj(C_Q0, C_K0), gq_ref))
    put_residue(ka_ref, head_norm(proj(C_K0, C_V0), gk_ref))
    put_residue(va_ref, proj(C_V0, C_CQ0))
    cq_ref[0] = (_rms(proj(C_CQ0, C_CKV0), Q_LORA) * gcq_ref[...]).astype(BF16)
    ckv_ref[0] = (_rms(proj(C_CKV0, C_KR0), KV_LORA) * gckv_ref[...]).astype(BF16)
    kr_ref[0] = proj(C_KR0, IN_COLS_PAD)


def _inproj(x, g, w_in_p, gsum, gq, gk, gcq, gckv):
    B = x.shape[0]
    res = lambda: (jax.ShapeDtypeStruct((B, N_RES, M_SUB, A_WIDTH), BF16),
                   pl.BlockSpec((1, N_RES, M_STEP, A_WIDTH), lambda b, j: (b, 0, j, 0)))
    nat = lambda c, dt: (jax.ShapeDtypeStruct((B, SEQ, c), dt),
                         pl.BlockSpec((1, TM, c), lambda b, j: (b, j, 0)))
    outs = [res(), res(), res(), nat(Q_LORA, BF16), nat(KV_LORA, BF16), nat(LANES, F32)]
    return pl.pallas_call(
        _inproj_kernel,
        grid=(B, N_ROW_STEPS),
        in_specs=[pl.BlockSpec((1, TM, D_MODEL), lambda b, j: (b, j, 0)),
                  _full(g), _full(w_in_p), _full(gsum), _full(gq), _full(gk), _full(gcq), _full(gckv)],
        out_specs=[o[1] for o in outs],
        out_shape=[o[0] for o in outs],
        scratch_shapes=[pltpu.VMEM((A_WIDTH // LANES, TM, LANES), F32)],
        compiler_params=_cparams(("parallel", "parallel")),
        name="inproj",
    )(x, g, w_in_p, gsum, gq, gk, gcq, gckv)


def _mla_prep_kernel(cq_ref, ckv_ref, kr_ref, wq_ref, wkv_ref, gsq_ref, gsk_ref, icq_ref,
                     gq_ref, gk_ref, gkr_ref, cos_ref, sa_ref, sb_ref, q_ref, k_ref, v_ref):
    cos, sa, sb = cos_ref[...], sa_ref[...], sb_ref[...]
    lane = lax.broadcasted_iota(jnp.int32, (TM, LANES), 1)

    def rope(t):
        return t * cos + pltpu.roll(t, 16, 1) * sa + pltpu.roll(t, LANES - 16, 1) * sb

    q = jnp.dot(cq_ref[0], wq_ref[...], preferred_element_type=F32)
    kv = jnp.dot(ckv_ref[0], wkv_ref[...], preferred_element_type=F32)
    k_rope = rope(_rms(kr_ref[0], ROPE_DIM) * gkr_ref[...])
    for h in range(N_HEADS_M):
        sl = slice(h * HEAD_SLOT, (h + 1) * HEAD_SLOT)
        qh = q[:, sl]
        ss = jnp.dot((qh * qh).astype(BF16), gsq_ref[...], preferred_element_type=F32)
        q_ref[0, :, sl] = rope(qh * lax.rsqrt(ss * icq_ref[...] + EPS) * gq_ref[...]).astype(BF16)
        kh = kv[:, sl]
        ssk = jnp.dot((kh * kh).astype(BF16), gsk_ref[...], preferred_element_type=F32)
        kn = kh * lax.rsqrt(ssk * (1.0 / NOPE_DIM) + EPS) * gk_ref[...] + k_rope
        k_ref[0, :, sl] = kn.astype(BF16)
    for hp in range(N_HEADS_M // 2):
        v_pair = kv[:, N_HEADS_M * HEAD_SLOT + hp * LANES:N_HEADS_M * HEAD_SLOT + (hp + 1) * LANES]
        v_ref[0, :, (2 * hp) * HEAD_SLOT:(2 * hp + 1) * HEAD_SLOT] = jnp.where(lane < V_DIM, v_pair, 1.0).astype(BF16)
        v_ref[0, :, (2 * hp + 1) * HEAD_SLOT:(2 * hp + 2) * HEAD_SLOT] = jnp.where(lane < V_DIM, 1.0, v_pair).astype(BF16)


def _mla_prep(cq, ckv, kr, wq_p, wkv_p, gsq, gsk, icq, gq, gk, gkr, cos, sa, sb):
    B = cq.shape[0]
    rows = lambda c: pl.BlockSpec((1, TM, c), lambda b, j: (b, j, 0))
    tab = pl.BlockSpec((TM, LANES), lambda b, j: (j, 0))
    wide = N_HEADS_M * HEAD_SLOT
    return pl.pallas_call(
        _mla_prep_kernel,
        grid=(B, N_ROW_STEPS),
        in_specs=[rows(Q_LORA), rows(KV_LORA), rows(LANES), _full(wq_p), _full(wkv_p), _full(gsq),
                  _full(gsk), _full(icq), _full(gq), _full(gk), _full(gkr), tab, tab, tab],
        out_specs=[rows(wide)] * 3,
        out_shape=[jax.ShapeDtypeStruct((B, SEQ, wide), BF16)] * 3,
        compiler_params=_cparams(("parallel", "parallel")),
        name="mla_prep",
    )(cq, ckv, kr, wq_p, wkv_p, gsq, gsk, icq, gq, gk, gkr, cos, sa, sb)


def _mla_attn_kernel(q_ref, k_ref, v_ref, o_ref):
    n_q = SEQ // MLA_TQ
    row = lax.broadcasted_iota(jnp.int32, (MLA_TQ, MLA_TQ), 0)
    col = lax.broadcasted_iota(jnp.int32, (MLA_TQ, MLA_TQ), 1)
    lane = lax.broadcasted_iota(jnp.int32, (MLA_TQ, LANES), 1)
    heads = [slice(hh * HEAD_SLOT, (hh + 1) * HEAD_SLOT) for hh in range(2)]

    def probs(i, hh):
        n_keys = (i + 1) * MLA_TQ
        s = lax.dot_general(q_ref[0, i * MLA_TQ:n_keys, heads[hh]], k_ref[0, 0:n_keys, heads[hh]],
                            NT_DIMS, preferred_element_type=F32)
        diag = jnp.where(col <= row, s[:, n_keys - MLA_TQ:], NEG)
        s = diag if i == 0 else jnp.concatenate([s[:, :n_keys - MLA_TQ], diag], axis=1)
        return jnp.exp(s - jnp.max(s, axis=-1, keepdims=True)).astype(BF16)

    def values(i, hh, p):
        return jnp.dot(p, v_ref[0, 0:(i + 1) * MLA_TQ, heads[hh]], preferred_element_type=F32)

    units = [(i, hh) for i in range(n_q) for hh in range(2)]
    acc, pending = {}, None
    for u in units:
        p = probs(*u)
        if pending is not None:
            acc[pending[0]] = values(*pending[0], pending[1])
        pending = (u, p)
    acc[pending[0]] = values(*pending[0], pending[1])
    for i in range(n_q):
        num = jnp.where(lane < V_DIM, acc[(i, 0)], acc[(i, 1)])
        den = pltpu.roll(jnp.where(lane < V_DIM, acc[(i, 1)], acc[(i, 0)]), V_DIM, 1)
        o_ref[0, i * MLA_TQ:(i + 1) * MLA_TQ, :] = (num / den).astype(BF16)


def _mla_attn(q, k, v):
    B = q.shape[0]
    pair = lambda c: pl.BlockSpec((1, SEQ, c), lambda b, h: (b, 0, h))
    return pl.pallas_call(
        _mla_attn_kernel,
        grid=(B, N_HEADS_M // 2),
        in_specs=[pair(2 * HEAD_SLOT)] * 3,
        out_specs=pair(2 * V_DIM),
        out_shape=jax.ShapeDtypeStruct((B, SEQ, M_WIDTH), BF16),
        compiler_params=_cparams(("parallel", "parallel")),
        name="mla_attn",
    )(q, k, v)


N_BLK = SEQ // BAND_BLOCK


def _dilated_kernel(q_ref, k_ref, v_ref, t1_ref, t2_ref, t3_ref, o_ref,
                    q1, k1, v1, q2, k2, v2, ob1, ls1, ob2, ls2, ob3, ls3):
    bb = BAND_BLOCK
    lane = lax.broadcasted_iota(jnp.int32, (bb, LANES), 1)
    lane_row = lax.broadcasted_iota(jnp.int32, (1, LANES), 1)
    own = [(lane_row < HEAD_DIM_A).astype(BF16), (lane_row >= HEAD_DIM_A).astype(BF16)]
    rows = lambda lo, hi: slice(lo * bb, hi * bb)

    for src, d1, d2 in ((q_ref, q1, q2), (k_ref, k1, k2), (v_ref, v1, v2)):
        for n2 in range(N_BLK // 2):
            pieces = [src[0, r, 16 * n2:16 * n2 + 16, :].astype(F32) for r in range(N_RES)]
            d1[rows(2 * n2, 2 * n2 + 1), :] = jnp.concatenate([p[0:8] for p in pieces], axis=0).astype(BF16)
            d1[rows(2 * n2 + 1, 2 * n2 + 2), :] = jnp.concatenate([p[8:16] for p in pieces], axis=0).astype(BF16)
        for r4 in range(4):
            for n in range(4):
                d2[rows(r4 * 4 + n, r4 * 4 + n + 1), :] = jnp.concatenate(
                    [src[0, r4 + 4 * c, 32 * n:32 * n + 32, :] for c in range(4)], axis=0)

    blocks = []
    for idx in range(N_BLK):
        lo = idx - 1 if idx > 0 else idx
        blocks.append((q1, k1, v1, rows(lo, idx + 1), t1_ref, ob1, ls1, idx))
    for idx in range(N_BLK):
        lo = idx - 1 if idx % 4 else idx
        blocks.append((q2, k2, v2, rows(lo, idx + 1), t2_ref, ob2, ls2, idx))
    for r in range(N_RES):
        blocks.append((None, None, None, r, t3_ref, ob3, ls3, r))

    def scores_and_probs(blk, hh):
        qd, kd, _, kv_rows, t_ref, _, _, idx = blk
        q = qd[rows(idx, idx + 1), :] if qd is not None else q_ref[0, idx]
        keys = kd[kv_rows, :] if kd is not None else k_ref[0, kv_rows]
        n_keys = keys.shape[0]
        s = lax.dot_general(q * own[hh], keys, NT_DIMS, preferred_element_type=F32)
        s = s + t_ref[hh, :, 2 * bb - n_keys:2 * bb]
        m = jnp.max(s, axis=-1, keepdims=True)
        return m, jnp.exp(s - m).astype(BF16)

    def weighted_values(blk, hh, e):
        _, _, vd, kv_rows, _, _, _, _ = blk
        vals = vd[kv_rows, :] if vd is not None else v_ref[0, kv_rows]
        return jnp.dot(e, vals * own[hh] + own[1 - hh], preferred_element_type=F32)

    def finish(blk, m, acc):
        o_dst, l_dst, idx = blk[5], blk[6], blk[7]
        num = jnp.where(lane < HEAD_DIM_A, acc[0], acc[1])
        den = pltpu.roll(jnp.where(lane < HEAD_DIM_A, acc[1], acc[0]), HEAD_DIM_A, 1)
        o_dst[idx] = num / den
        l_dst[idx] = jnp.where(lane < HEAD_DIM_A, m[0], m[1]) + jnp.log(den)

    stage = {}
    for t in range(len(blocks) + ATTN_SKEW):
        if t < len(blocks):
            stage[t] = [scores_and_probs(blocks[t], hh) for hh in range(2)]
        d = t - ATTN_SKEW
        if d >= 0:
            acc = [weighted_values(blocks[d], hh, stage[d][hh][1]) for hh in range(2)]
            finish(blocks[d], [stage[d][hh][0] for hh in range(2)], acc)
            del stage[d]

    for r in range(N_RES):
        r4, c = r % 4, r // 4
        gather1 = lambda ref: jnp.concatenate([ref[n, 8 * r:8 * r + 8, :] for n in range(N_BLK)], axis=0)
        gather2 = lambda ref: jnp.concatenate(
            [ref[r4 * 4 + n, 32 * c:32 * c + 32, :] for n in range(4)], axis=0)
        o_b = [gather1(ob1), gather2(ob2), ob3[r]]
        l_b = [gather1(ls1), gather2(ls2), ls3[r]]
        top = jnp.maximum(jnp.maximum(l_b[0], l_b[1]), l_b[2])
        w_b = [jnp.exp(l - top) for l in l_b]
        num = w_b[0] * o_b[0] + w_b[1] * o_b[1] + w_b[2] * o_b[2]
        o_ref[0, r] = (num / (w_b[0] + w_b[1] + w_b[2])).astype(BF16)


def _dilated(qa, ka, va, t1, t2, t3):
    B = qa.shape[0]
    blk = pl.BlockSpec((1, N_RES, M_SUB, LANES), lambda b, h: (b, 0, 0, h))
    tab = lambda t: pl.BlockSpec((2,) + t.shape[1:], lambda b, h: (h, 0, 0))
    blocked_bf16 = pltpu.VMEM((N_BLK * BAND_BLOCK, LANES), BF16)
    blocked_f32 = pltpu.VMEM((N_BLK, BAND_BLOCK, LANES), F32)
    return pl.pallas_call(
        _dilated_kernel,
        grid=(B, N_HEADS_A // 2),
        in_specs=[blk, blk, blk, tab(t1), tab(t2), tab(t3)],
        out_specs=blk,
        out_shape=jax.ShapeDtypeStruct((B, N_RES, M_SUB, A_WIDTH), BF16),
        scratch_shapes=[blocked_bf16] * 6 + [blocked_f32] * 6,
        compiler_params=_cparams(("parallel", "parallel")),
        name="dilated_attn",
    )(qa, ka, va, t1, t2, t3)


R_W1, R_W2, R_E1, R_E2, R_RANK1, R_RANK2 = range(6)
ROUTE_ROWS = 8
ROUTE_GROUP_ROWS = 8
ROUTE_EXPERT_ROW0 = 8


def _outproj_kernel(oa_ref, om_ref, x_ref, wo_ref, g_ref, wrh_ref, wrl_ref, br_ref, tri_ref,
                    h_ref, xn_ref, route_ref, cnt_ref, perm_scr, carry_scr):
    @pl.when((pl.program_id(0) == 0) & (pl.program_id(1) == 0))
    def _():
        carry_scr[...] = jnp.zeros(carry_scr.shape, F32)

    n_chunks = A_WIDTH // LANES
    for r in range(N_RES):
        for c in range(n_chunks):
            perm_scr[c, _residue_rows(r), :] = oa_ref[0, r, :, c * LANES:(c + 1) * LANES].astype(F32)
    oa = jnp.concatenate([perm_scr[c] for c in range(n_chunks)], axis=1).astype(BF16)
    h = (x_ref[0] + jnp.dot(oa, wo_ref[0:A_WIDTH, :], preferred_element_type=F32)
         + jnp.dot(om_ref[0], wo_ref[A_WIDTH:, :], preferred_element_type=F32))
    h_ref[0] = h
    xn = _rms(h, D_MODEL) * g_ref[...]
    xn_ref[0] = xn.astype(BF16)
    hi = xn.astype(BF16)
    lo = (xn - hi.astype(F32)).astype(BF16)
    lg = (lax.dot_general(wrh_ref[...], hi, NT_DIMS, preferred_element_type=F32)
          + lax.dot_general(wrh_ref[...], lo, NT_DIMS, preferred_element_type=F32)
          + lax.dot_general(wrl_ref[...], hi, NT_DIMS, preferred_element_type=F32)) + br_ref[...]

    sub = lax.broadcasted_iota(jnp.int32, (EXPERTS_PER_GROUP, TM), 0).astype(F32)
    cmax = lambda t: jnp.max(t, axis=0, keepdims=True)
    cmin = lambda t: jnp.min(t, axis=0, keepdims=True)
    csum = lambda t: jnp.sum(t, axis=0, keepdims=True)
    none = float(EXPERTS_PER_GROUP)

    gl = lg[0:ROUTE_GROUP_ROWS]
    ge = jnp.exp(gl - cmax(gl))
    gsum = csum(ge)
    g_gate = 1.0 / gsum
    g_idx = cmin(jnp.where(ge / gsum == g_gate, sub, none))
    el = lg[ROUTE_EXPERT_ROW0:ROUTE_EXPERT_ROW0 + EXPERTS_PER_GROUP]
    for g in range(1, N_GROUPS):
        r0 = ROUTE_EXPERT_ROW0 + g * EXPERTS_PER_GROUP
        el = jnp.where(g_idx == float(g), lg[r0:r0 + EXPERTS_PER_GROUP], el)
    ee = jnp.exp(el - cmax(el))
    esum = csum(ee)
    eprob = ee / esum
    p1 = 1.0 / esum
    i1 = cmin(jnp.where(eprob == p1, sub, none))
    rest = jnp.where(sub == i1, -1.0, eprob)
    p2 = cmax(rest)
    i2 = cmin(jnp.where(rest == p2, sub, none))
    den = p1 + p2
    e1 = g_idx * EXPERTS_PER_GROUP + i1
    e2 = g_idx * EXPERTS_PER_GROUP + i2
    erow = lax.broadcasted_iota(jnp.int32, (N_EXPERTS, TM), 0).astype(F32)
    onehot = ((erow == e1) | (erow == e2)).astype(F32)
    before = jnp.dot(onehot.astype(BF16), tri_ref[...], preferred_element_type=F32) + carry_scr[:, 0:1]
    rank1 = csum(jnp.where(erow == e1, before, 0.0))
    rank2 = csum(jnp.where(erow == e2, before, 0.0))
    carry_scr[...] = carry_scr[...] + jnp.sum(onehot, axis=1, keepdims=True)
    cnt_ref[...] = carry_scr[...]

    record = jnp.zeros((ROUTE_ROWS, TM), F32)
    for i, val in ((R_W1, g_gate * (p1 / den)), (R_W2, g_gate * (p2 / den)), (R_E1, e1), (R_E2, e2),
                   (R_RANK1, rank1), (R_RANK2, rank2)):
        record = jnp.where(sub == float(i), val, record)
    route_ref[0] = record


def _outproj(oa, om, x, wo, g, wrh, wrl, br, tri):
    B = oa.shape[0]
    rows = lambda c: pl.BlockSpec((1, TM, c), lambda b, j: (b, j, 0))
    shp = lambda c, dt: jax.ShapeDtypeStruct((B, SEQ, c), dt)
    return pl.pallas_call(
        _outproj_kernel,
        grid=(B, N_ROW_STEPS),
        in_specs=[pl.BlockSpec((1, N_RES, M_STEP, A_WIDTH), lambda b, j: (b, 0, j, 0)),
                  rows(M_WIDTH), rows(D_MODEL), _full(wo), _full(g), _full(wrh), _full(wrl),
                  _full(br), _full(tri)],
        out_specs=[rows(D_MODEL), rows(D_MODEL),
                   pl.BlockSpec((1, ROUTE_ROWS, TM), lambda b, j: (b * N_ROW_STEPS + j, 0, 0)),
                   pl.BlockSpec((N_EXPERTS, LANES), lambda b, j: (0, 0))],
        out_shape=[shp(D_MODEL, F32), shp(D_MODEL, BF16),
                   jax.ShapeDtypeStruct((B * N_ROW_STEPS, ROUTE_ROWS, TM), F32),
                   jax.ShapeDtypeStruct((N_EXPERTS, LANES), F32)],
        scratch_shapes=[pltpu.VMEM((A_WIDTH // LANES, TM, LANES), F32), pltpu.VMEM((N_EXPERTS, LANES), F32)],
        compiler_params=_cparams(("arbitrary", "arbitrary")),
        name="outproj_router",
    )(oa, om, x, wo, g, wrh, wrl, br, tri)


def _expert_kernel(blk_e_ref, n_used_ref, x_ref, wg_ref, wu_ref, wd_ref, y_ref, wg_s, wu_s, wd_s):
    i = pl.program_id(0)

    @pl.when(i < n_used_ref[0])
    def _():
        @pl.when((i == 0) | (blk_e_ref[i] != blk_e_ref[jnp.maximum(i - 1, 0)]))
        def _():
            wg_s[...] = wg_ref[0].astype(BF16)
            wu_s[...] = wu_ref[0].astype(BF16)
            wd_s[...] = wd_ref[0].astype(BF16)

        x = x_ref[...]
        gate = jnp.dot(x, wg_s[...], preferred_element_type=F32)
        up = jnp.dot(x, wu_s[...], preferred_element_type=F32)
        hdn = (gate * jax.nn.sigmoid(gate) * up).astype(BF16)
        y_ref[...] = jnp.dot(hdn, wd_s[...], preferred_element_type=F32).astype(y_ref.dtype)

    @pl.when(i >= n_used_ref[0])
    def _():
        y_ref[...] = jnp.zeros(y_ref.shape, y_ref.dtype)


def _experts(blk_e, n_used, xs, wg, wu, wd):
    n_blocks = xs.shape[0] // MOE_TM
    row_in = lambda i, be, nu: (jnp.minimum(i, nu[0] - 1), 0)
    wsel = lambda i, be, nu: (be[i], 0, 0)
    return pl.pallas_call(
        _expert_kernel,
        grid_spec=pltpu.PrefetchScalarGridSpec(
            num_scalar_prefetch=2,
            grid=(n_blocks,),
            in_specs=[pl.BlockSpec((MOE_TM, D_MODEL), row_in),
                      pl.BlockSpec((1, D_MODEL, EXPERT_FF), wsel),
                      pl.BlockSpec((1, D_MODEL, EXPERT_FF), wsel),
                      pl.BlockSpec((1, EXPERT_FF, D_MODEL), wsel)],
            out_specs=pl.BlockSpec((MOE_TM, D_MODEL), lambda i, be, nu: (i, 0)),
            scratch_shapes=[pltpu.VMEM((D_MODEL, EXPERT_FF), BF16), pltpu.VMEM((D_MODEL, EXPERT_FF), BF16),
                            pltpu.VMEM((EXPERT_FF, D_MODEL), BF16)]),
        out_shape=jax.ShapeDtypeStruct(xs.shape, BF16),
        compiler_params=_cparams(("arbitrary",)),
        name="expert_ffn",
    )(blk_e, n_used, xs, wg, wu, wd)


def _ple_kernel(h_ref, y_ref, route_ref, p_ref, wp_ref, gp_ref, wg_ref, bg_ref, o_ref):
    rec = jnp.concatenate([route_ref[0], jnp.zeros((LANES - ROUTE_ROWS, TM), F32)], axis=0)
    route = jnp.concatenate([rec[:, c * LANES:(c + 1) * LANES].T for c in range(TM // LANES)], axis=0)
    h = (h_ref[0] + route[:, R_W1:R_W1 + 1] * y_ref[:, :D_MODEL].astype(F32)
         + route[:, R_W2:R_W2 + 1] * y_ref[:, D_MODEL:].astype(F32))
    e = _rms(jnp.dot(p_ref[0].astype(BF16), wp_ref[...], preferred_element_type=F32), D_MODEL) * gp_ref[...]
    g = jax.nn.sigmoid(jnp.dot(h.astype(BF16), wg_ref[...], preferred_element_type=F32) + bg_ref[...])
    o_ref[0] = h + g * e


def _ple(h, y_pair, route, p, wp, gp, wg, bg):
    B = h.shape[0]
    rows = lambda c: pl.BlockSpec((1, TM, c), lambda b, j: (b, j, 0))
    step = lambda b, j: b * N_ROW_STEPS + j
    return pl.pallas_call(
        _ple_kernel,
        grid=(B, N_ROW_STEPS),
        in_specs=[rows(D_MODEL), pl.BlockSpec((TM, TOP_K * D_MODEL), lambda b, j: (step(b, j), 0)),
                  pl.BlockSpec((1, ROUTE_ROWS, TM), lambda b, j: (step(b, j), 0, 0)),
                  rows(PLE_DIM), _full(wp), _full(gp), _full(wg), _full(bg)],
        out_specs=rows(D_MODEL),
        out_shape=jax.ShapeDtypeStruct((B, SEQ, D_MODEL), F32),
        compiler_params=_cparams(("parallel", "parallel")),
        name="ple_gate",
    )(h, y_pair, route, p, wp, gp, wg, bg)


def _t5_bucket(dist):
    max_exact = REL_BUCKETS // 2
    n = jnp.maximum(dist, 0)
    nf = jnp.maximum(n, 1).astype(F32)
    large = max_exact + (jnp.log(nf / max_exact) / math.log(REL_MAX_DISTANCE / max_exact)
                         * (REL_BUCKETS - max_exact)).astype(jnp.int32)
    large = jnp.minimum(large, REL_BUCKETS - 1)
    return jnp.where(n < max_exact, n, large)


def _bias_table(rel_bias, local_index, dilation, with_prev):
    loc = np.asarray(local_index)
    delta = loc[:, None] - loc[None, :]
    if with_prev:
        delta = np.concatenate([delta + BAND_BLOCK, delta], axis=1)
    ok = (delta >= 0) & (delta <= BAND_BLOCK)
    bucket = _t5_bucket(jnp.asarray(delta * dilation, jnp.int32))
    picked = jnp.where(bucket[None, :, :, None] == jnp.arange(REL_BUCKETS, dtype=jnp.int32),
                       rel_bias.astype(F32).T[:, None, None, :], 0.0)
    return jnp.where(jnp.asarray(ok)[None], jnp.sum(picked, axis=-1), NEG)


def _block_diag_ones(sizes, total):
    g = np.zeros((total, total), np.float32)
    o = 0
    for s, on in sizes:
        if on:
            g[o:o + s, o:o + s] = 1.0
        o += s
    return jnp.asarray(g, BF16)


def _dispatch_plan(route, counts, n_tokens):
    fields = route.transpose(1, 0, 2).reshape(ROUTE_ROWS, n_tokens)
    e = fields[R_E1:R_E2 + 1].T.astype(jnp.int32)
    rank = fields[R_RANK1:R_RANK2 + 1].T.astype(jnp.int32)
    pcounts = (counts + MOE_TM - 1) // MOE_TM * MOE_TM
    pend = jnp.cumsum(pcounts)
    pstart = pend - pcounts
    ids = jnp.arange(N_EXPERTS, dtype=jnp.int32)
    pos = rank + jnp.sum(jnp.where(e[..., None] == ids, pstart, 0), axis=-1)
    n_assign = n_tokens * TOP_K
    shift = int(math.ceil(math.log2(n_assign)))
    keys = (e.reshape(-1) << shift) | jnp.arange(n_assign, dtype=jnp.int32)
    tok_sorted = (jnp.sort(keys) & ((1 << shift) - 1)) // TOP_K
    n_rows = n_assign + N_EXPERTS * MOE_TM
    rows = jnp.arange(n_rows, dtype=jnp.int32)
    row_e = jnp.sum(rows[:, None] >= pend[None, :], axis=-1)
    row_e = jnp.minimum(row_e, N_EXPERTS - 1)
    pick = lambda tbl: jnp.sum(jnp.where(row_e[:, None] == ids, tbl, 0), axis=-1)
    within = rows - pick(pstart)
    src = jnp.clip(pick(jnp.cumsum(counts) - counts) + within, 0, n_assign - 1)
    row_tok = jnp.where(within < pick(counts), tok_sorted[src], rows % n_tokens)
    n_blocks = n_rows // MOE_TM
    n_used = (pend[-1] // MOE_TM).astype(jnp.int32)
    blk_start = jnp.minimum(jnp.arange(n_blocks, dtype=jnp.int32), n_used - 1) * MOE_TM
    blk_e = jnp.minimum(jnp.sum(blk_start[:, None] >= pend[None, :], axis=-1), N_EXPERTS - 1).astype(jnp.int32)
    return row_tok, pos, blk_e, n_used.reshape(1)


def kernel(x, p, rel_bias, norm_mix_gain, w_in, qn_a_gain, kn_a_gain, q_a_gain, w_q_up, kv_a_gain, w_kv_up, qn_nope_gain, qn_rope_gain, kn_nope_gain, kn_rope_gain, w_out, norm_ffn_gain, w_router_group, b_router_group, w_router_expert, b_router_expert, w_exp_gate, w_exp_up, w_exp_down, w_ple_proj, ple_norm_gain, w_ple_gate, b_ple_gate):
    B, S, D = x.shape
    assert (S, D) == (SEQ, D_MODEL) and p.shape[0] == 1
    n_tokens = B * S
    row = lambda a: a.reshape(1, -1).astype(F32)
    zeros = lambda *s: jnp.zeros(s, F32)

    w_in_p = jnp.concatenate([w_in[0, :, :C_KR0], zeros(D, 64), w_in[0, :, C_KR0:], zeros(D, 32)],
                             axis=1).astype(BF16)
    gsum_a = _block_diag_ones([(HEAD_DIM_A, True)] * N_HEADS_A, A_WIDTH)
    gq_a = row(jnp.tile(qn_a_gain[0], N_HEADS_A)) * (HEAD_DIM_A ** -0.5)
    gk_a = row(jnp.tile(kn_a_gain[0], N_HEADS_A))

    wq_p = jnp.pad(w_q_up[0].reshape(Q_LORA, N_HEADS_M, NOPE_DIM + ROPE_DIM),
                   ((0, 0), (0, 0), (0, HEAD_SLOT - NOPE_DIM - ROPE_DIM))).reshape(Q_LORA, -1).astype(BF16)
    wkv = w_kv_up[0].reshape(KV_LORA, N_HEADS_M, NOPE_DIM + V_DIM)
    wk_p = jnp.pad(wkv[..., :NOPE_DIM], ((0, 0), (0, 0), (0, HEAD_SLOT - NOPE_DIM))).reshape(KV_LORA, -1)
    wkv_p = jnp.concatenate([wk_p, wkv[..., NOPE_DIM:].reshape(KV_LORA, -1)], axis=1).astype(BF16)
    gs_q = _block_diag_ones([(NOPE_DIM, True), (ROPE_DIM, True), (32, False)], HEAD_SLOT)
    gs_k = _block_diag_ones([(NOPE_DIM, True), (64, False)], HEAD_SLOT)
    inv_cnt_q = jnp.asarray(np.concatenate([np.full(64, 1 / NOPE_DIM), np.full(32, 1 / ROPE_DIM),
                                            np.ones(32)]).astype(np.float32)).reshape(1, HEAD_SLOT)
    mla_scale = (NOPE_DIM + ROPE_DIM) ** -0.5
    gq_m = row(jnp.concatenate([qn_nope_gain[0], qn_rope_gain[0], zeros(32)])) * mla_scale
    gk_m = row(jnp.concatenate([kn_nope_gain[0], zeros(64)]))
    gkr_m = row(jnp.concatenate([zeros(64), kn_rope_gain[0], zeros(32)]))

    half = ROPE_DIM // 2
    inv = 1.0 / (ROPE_THETA ** (jnp.arange(half, dtype=F32) * 2.0 / ROPE_DIM))
    ang = jnp.arange(S, dtype=jnp.int32).astype(F32)[:, None] * inv[None, :]
    cosv, sinv = jnp.cos(ang), jnp.sin(ang)
    cos_t = jnp.concatenate([jnp.ones((S, 64), F32), cosv, cosv, jnp.ones((S, 32), F32)], -1)
    sin_a = jnp.concatenate([zeros(S, 80), sinv, zeros(S, 32)], -1)
    sin_b = jnp.concatenate([zeros(S, 64), -sinv, zeros(S, 48)], -1)

    loc1 = [16 * a + r for r in range(16) for a in range(8)]
    loc2 = [4 * a + c for c in range(4) for a in range(32)]
    loc3 = list(range(BAND_BLOCK))
    t1 = _bias_table(rel_bias, loc1, 1, True)
    t2 = _bias_table(rel_bias, loc2, 4, True)
    t3 = jnp.concatenate([jnp.full((N_HEADS_A, BAND_BLOCK, BAND_BLOCK), NEG, F32),
                          _bias_table(rel_bias, loc3, 16, False)], axis=-1)

    n_pad_g, n_pad_e = ROUTE_GROUP_ROWS - N_GROUPS, LANES - ROUTE_EXPERT_ROW0 - N_EXPERTS
    w_r = jnp.concatenate([w_router_group[0].T, zeros(n_pad_g, D), w_router_expert[0].T, zeros(n_pad_e, D)], axis=0)
    w_r_hi = w_r.astype(BF16)
    w_r_lo = (w_r - w_r_hi.astype(F32)).astype(BF16)
    b_r = jnp.concatenate([b_router_group[0], jnp.full((n_pad_g,), NEG, F32), b_router_expert[0],
                           zeros(n_pad_e)]).reshape(LANES, 1)
    tri = jnp.asarray(np.triu(np.ones((TM, TM), np.float32), 1), BF16)

    qa, ka, va, cq, ckv, kr = _inproj(x, row(norm_mix_gain[0]), w_in_p, gsum_a, gq_a, gk_a,
                                      row(q_a_gain[0]), row(kv_a_gain[0]))
    o_a = _dilated(qa, ka, va, t1, t2, t3)
    q_m, k_m, v_m = _mla_prep(cq, ckv, kr, wq_p, wkv_p, gs_q, gs_k, inv_cnt_q, gq_m, gk_m, gkr_m,
                              cos_t, sin_a, sin_b)
    o_m = _mla_attn(q_m, k_m, v_m)
    h1, xn2, route, cnt = _outproj(o_a, o_m, x, w_out[0].astype(BF16), row(norm_ffn_gain[0]),
                                   w_r_hi, w_r_lo, b_r, tri)

    row_tok, pos, blk_e, n_used = _dispatch_plan(route, cnt[:, 0].astype(jnp.int32), n_tokens)
    xs = xn2.reshape(n_tokens, D)[row_tok]
    y = _experts(blk_e, n_used, xs, w_exp_gate[0], w_exp_up[0], w_exp_down[0])
    y_pair = y[pos.reshape(-1)].reshape(n_tokens, TOP_K * D)
    return _ple(h1, y_pair, route, p[0], w_ple_proj[0].astype(BF16), row(ple_norm_gain[0]),
                w_ple_gate[0].astype(BF16), row(b_ple_gate[0]))
```

```python
import functools
import math

import jax
import jax.numpy as jnp
import numpy as np
from jax import lax
from jax.experimental import pallas as pl
from jax.experimental.pallas import tpu as pltpu

F32 = jnp.float32
BF16 = jnp.bfloat16

D_MODEL = 1024
SEQ = 2048
PLE_DIM = 256
EPS = 1e-6
NEG = -1e30
HEAD_DIM_A = 64
A_WIDTH = 512
N_HEADS_A = 8
BAND_BLOCK = 128
REL_BUCKETS = 32
REL_MAX_DISTANCE = 2048
M_WIDTH = 512
V_DIM = 64
N_HEADS_M = 8
Q_LORA = 384
KV_LORA = 256
NOPE_DIM = 64
ROPE_DIM = 32
ROPE_THETA = 10000.0
N_GROUPS = 4
EXPERTS_PER_GROUP = 8
N_EXPERTS = 32
TOP_K = 2
EXPERT_FF = 512

LANES = 128
N_RES = 16
M_SUB = SEQ // N_RES
TM = 512
N_ROW_STEPS = SEQ // TM
M_STEP = TM // N_RES
HEAD_SLOT = LANES
MLA_TQ = 512
MOE_TM = 512
ATTN_SKEW = 2
VMEM_LIMIT = 48 * 1024 * 1024

IN_COLS_PAD = 3 * A_WIDTH + Q_LORA + KV_LORA + LANES
C_Q0, C_K0, C_V0, C_CQ0, C_CKV0, C_KR0 = 0, 512, 1024, 1536, 1920, 2176

NT_DIMS = (((1,), (1,)), ((), ()))
LOG2E = math.log2(math.e)
LN2 = math.log(2.0)


def _cparams(sem):
    return pltpu.CompilerParams(dimension_semantics=sem, vmem_limit_bytes=VMEM_LIMIT)


def _full(a):
    return pl.BlockSpec(a.shape, lambda *_: (0,) * a.ndim)


def _rms(x, n):
    return x * lax.rsqrt(jnp.sum(x * x, axis=-1, keepdims=True) * (1.0 / n) + EPS)


def _residue_rows(r):
    return pl.ds(r, M_STEP, stride=N_RES)


def _inproj_kernel(x_ref, g_ref, w_ref, gsum_ref, gq_ref, gk_ref, gcq_ref, gckv_ref,
                   qa_ref, ka_ref, va_ref, cq_ref, ckv_ref, kr_ref, perm_scr):
    xn = (_rms(x_ref[0], D_MODEL) * g_ref[...]).astype(BF16)

    def proj(c0, c1):
        return jnp.dot(xn, w_ref[:, c0:c1], preferred_element_type=F32)

    def head_norm(t, gain_ref):
        ss = jnp.dot((t * t).astype(BF16), gsum_ref[...], preferred_element_type=F32)
        return t * lax.rsqrt(ss * (1.0 / HEAD_DIM_A) + EPS) * gain_ref[...]

    def put_residue(ref, val):
        for c in range(A_WIDTH // LANES):
            perm_scr[c] = val[:, c * LANES:(c + 1) * LANES]
        for r in range(N_RES):
            for c in range(A_WIDTH // LANES):
                ref[0, r, :, c * LANES:(c + 1) * LANES] = perm_scr[c, _residue_rows(r), :].astype(ref.dtype)

    put_residue(qa_ref, head_norm(proj(C_Q0, C_K0), gq_ref))
    put_residue(ka_ref, head_norm(proj(C_K0, C_V0), gk_ref))
    put_residue(va_ref, proj(C_V0, C_CQ0))
    cq_ref[0] = (_rms(proj(C_CQ0, C_CKV0), Q_LORA) * gcq_ref[...]).astype(BF16)
    ckv_ref[0] = (_rms(proj(C_CKV0, C_KR0), KV_LORA) * gckv_ref[...]).astype(BF16)
    kr_ref[0] = proj(C_KR0, IN_COLS_PAD)


def _inproj(x, g, w_in_p, gsum, gq, gk, gcq, gckv):
    B = x.shape[0]
    res = lambda: (jax.ShapeDtypeStruct((B, N_RES, M_SUB, A_WIDTH), BF16),
                   pl.BlockSpec((1, N_RES, M_STEP, A_WIDTH), lambda b, j: (b, 0, j, 0)))
    nat = lambda c, dt: (jax.ShapeDtypeStruct((B, SEQ, c), dt),
                         pl.BlockSpec((1, TM, c), lambda b, j: (b, j, 0)))
    outs = [res(), res(), res(), nat(Q_LORA, BF16), nat(KV_LORA, BF16), nat(LANES, F32)]
    return pl.pallas_call(
        _inproj_kernel,
        grid=(B, N_ROW_STEPS),
        in_specs=[pl.BlockSpec((1, TM, D_MODEL), lambda b, j: (b, j, 0)),
                  _full(g), _full(w_in_p), _full(gsum), _full(gq), _full(gk), _full(gcq), _full(gckv)],
        out_specs=[o[1] for o in outs],
        out_shape=[o[0] for o in outs],
        scratch_shapes=[pltpu.VMEM((A_WIDTH // LANES, TM, LANES), F32)],
        compiler_params=_cparams(("parallel", "parallel")),
        name="inproj",
    )(x, g, w_in_p, gsum, gq, gk, gcq, gckv)


def _mla_prep_kernel(cq_ref, ckv_ref, kr_ref, wq_ref, wkv_ref, gsq_ref, gsk_ref, icq_ref,
                     gq_ref, gk_ref, gkr_ref, cos_ref, sa_ref, sb_ref, q_ref, k_ref, v_ref):
    cos, sa, sb = cos_ref[...], sa_ref[...], sb_ref[...]
    lane = lax.broadcasted_iota(jnp.int32, (TM, LANES), 1)

    def rope(t):
        return t * cos + pltpu.roll(t, 16, 1) * sa + pltpu.roll(t, LANES - 16, 1) * sb

    q = jnp.dot(cq_ref[0], wq_ref[...], preferred_element_type=F32)
    kv = jnp.dot(ckv_ref[0], wkv_ref[...], preferred_element_type=F32)
    k_rope = rope(_rms(kr_ref[0], ROPE_DIM) * gkr_ref[...])
    for h in range(N_HEADS_M):
        sl = slice(h * HEAD_SLOT, (h + 1) * HEAD_SLOT)
        qh = q[:, sl]
        ss = jnp.dot((qh * qh).astype(BF16), gsq_ref[...], preferred_element_type=F32)
        q_ref[0, :, sl] = rope(qh * lax.rsqrt(ss * icq_ref[...] + EPS) * gq_ref[...]).astype(BF16)
        kh = kv[:, sl]
        ssk = jnp.dot((kh * kh).astype(BF16), gsk_ref[...], preferred_element_type=F32)
        kn = kh * lax.rsqrt(ssk * (1.0 / NOPE_DIM) + EPS) * gk_ref[...] + k_rope
        k_ref[0, :, sl] = kn.astype(BF16)
    for hp in range(N_HEADS_M // 2):
        v_pair = kv[:, N_HEADS_M * HEAD_SLOT + hp * LANES:N_HEADS_M * HEAD_SLOT + (hp + 1) * LANES]
        v_ref[0, :, (2 * hp) * HEAD_SLOT:(2 * hp + 1) * HEAD_SLOT] = jnp.where(lane < V_DIM, v_pair, 1.0).astype(BF16)
        v_ref[0, :, (2 * hp + 1) * HEAD_SLOT:(2 * hp + 2) * HEAD_SLOT] = jnp.where(lane < V_DIM, 1.0, v_pair).astype(BF16)


def _mla_prep(cq, ckv, kr, wq_p, wkv_p, gsq, gsk, icq, gq, gk, gkr, cos, sa, sb):
    B = cq.shape[0]
    rows = lambda c: pl.BlockSpec((1, TM, c), lambda b, j: (b, j, 0))
    tab = pl.BlockSpec((TM, LANES), lambda b, j: (j, 0))
    wide = N_HEADS_M * HEAD_SLOT
    return pl.pallas_call(
        _mla_prep_kernel,
        grid=(B, N_ROW_STEPS),
        in_specs=[rows(Q_LORA), rows(KV_LORA), rows(LANES), _full(wq_p), _full(wkv_p), _full(gsq),
                  _full(gsk), _full(icq), _full(gq), _full(gk), _full(gkr), tab, tab, tab],
        out_specs=[rows(wide)] * 3,
        out_shape=[jax.ShapeDtypeStruct((B, SEQ, wide), BF16)] * 3,
        compiler_params=_cparams(("parallel", "parallel")),
        name="mla_prep",
    )(cq, ckv, kr, wq_p, wkv_p, gsq, gsk, icq, gq, gk, gkr, cos, sa, sb)


def _mla_attn_kernel(q_ref, k_ref, v_ref, o_ref):
    n_q = SEQ // MLA_TQ
    row = lax.broadcasted_iota(jnp.int32, (MLA_TQ, MLA_TQ), 0)
    col = lax.broadcasted_iota(jnp.int32, (MLA_TQ, MLA_TQ), 1)
    lane = lax.broadcasted_iota(jnp.int32, (MLA_TQ, LANES), 1)
    heads = [slice(hh * HEAD_SLOT, (hh + 1) * HEAD_SLOT) for hh in range(2)]

    def probs(i, hh):
        n_keys = (i + 1) * MLA_TQ
        s = lax.dot_general(q_ref[0, i * MLA_TQ:n_keys, heads[hh]], k_ref[0, 0:n_keys, heads[hh]],
                            NT_DIMS, preferred_element_type=F32)
        diag = jnp.where(col <= row, s[:, n_keys - MLA_TQ:], NEG)
        s = diag if i == 0 else jnp.concatenate([s[:, :n_keys - MLA_TQ], diag], axis=1)
        return jnp.exp2(s - jnp.max(s, axis=-1, keepdims=True)).astype(BF16)

    def values(i, hh, p):
        return jnp.dot(p, v_ref[0, 0:(i + 1) * MLA_TQ, heads[hh]], preferred_element_type=F32)

    units = [(i, hh) for i in range(n_q) for hh in range(2)]
    acc, pending = {}, None
    for u in units:
        p = probs(*u)
        if pending is not None:
            acc[pending[0]] = values(*pending[0], pending[1])
        pending = (u, p)
    acc[pending[0]] = values(*pending[0], pending[1])
    for i in range(n_q):
        num = jnp.where(lane < V_DIM, acc[(i, 0)], acc[(i, 1)])
        den = pltpu.roll(jnp.where(lane < V_DIM, acc[(i, 1)], acc[(i, 0)]), V_DIM, 1)
        o_ref[0, i * MLA_TQ:(i + 1) * MLA_TQ, :] = (num / den).astype(BF16)


def _mla_attn(q, k, v):
    B = q.shape[0]
    pair = lambda c: pl.BlockSpec((1, SEQ, c), lambda b, h: (b, 0, h))
    return pl.pallas_call(
        _mla_attn_kernel,
        grid=(B, N_HEADS_M // 2),
        in_specs=[pair(2 * HEAD_SLOT)] * 3,
        out_specs=pair(2 * V_DIM),
        out_shape=jax.ShapeDtypeStruct((B, SEQ, M_WIDTH), BF16),
        compiler_params=_cparams(("parallel", "parallel")),
        name="mla_attn",
    )(q, k, v)


N_BLK = SEQ // BAND_BLOCK


def _dilated_kernel(q_ref, k_ref, v_ref, t1_ref, t2_ref, t3_ref, o_ref,
                    q1, k1, v1, q2, k2, v2, ob1, ls1, ob2, ls2, ob3, ls3):
    bb = BAND_BLOCK
    lane = lax.broadcasted_iota(jnp.int32, (bb, LANES), 1)
    lane_row = lax.broadcasted_iota(jnp.int32, (1, LANES), 1)
    own = [(lane_row < HEAD_DIM_A).astype(BF16), (lane_row >= HEAD_DIM_A).astype(BF16)]
    rows = lambda lo, hi: slice(lo * bb, hi * bb)

    for src, d1, d2 in ((q_ref, q1, q2), (k_ref, k1, k2), (v_ref, v1, v2)):
        for n2 in range(N_BLK // 2):
            pieces = [src[0, r, 16 * n2:16 * n2 + 16, :].astype(F32) for r in range(N_RES)]
            d1[rows(2 * n2, 2 * n2 + 1), :] = jnp.concatenate([p[0:8] for p in pieces], axis=0).astype(BF16)
            d1[rows(2 * n2 + 1, 2 * n2 + 2), :] = jnp.concatenate([p[8:16] for p in pieces], axis=0).astype(BF16)
        for r4 in range(4):
            for n in range(4):
                d2[rows(r4 * 4 + n, r4 * 4 + n + 1), :] = jnp.concatenate(
                    [src[0, r4 + 4 * c, 32 * n:32 * n + 32, :] for c in range(4)], axis=0)

    blocks = []
    for idx in range(N_BLK):
        lo = idx - 1 if idx > 0 else idx
        blocks.append((q1, k1, v1, rows(lo, idx + 1), t1_ref, ob1, ls1, idx))
    for idx in range(N_BLK):
        lo = idx - 1 if idx % 4 else idx
        blocks.append((q2, k2, v2, rows(lo, idx + 1), t2_ref, ob2, ls2, idx))
    for r in range(N_RES):
        blocks.append((None, None, None, r, t3_ref, ob3, ls3, r))

    def scores_and_probs(blk, hh):
        qd, kd, _, kv_rows, t_ref, _, _, idx = blk
        q = qd[rows(idx, idx + 1), :] if qd is not None else q_ref[0, idx]
        keys = kd[kv_rows, :] if kd is not None else k_ref[0, kv_rows]
        n_keys = keys.shape[0]
        s = lax.dot_general(q * own[hh], keys, NT_DIMS, preferred_element_type=F32)
        s = s + t_ref[hh, :, 2 * bb - n_keys:2 * bb]
        m = jnp.max(s, axis=-1, keepdims=True)
        return m, jnp.exp2(s - m).astype(BF16)

    def weighted_values(blk, hh, e):
        _, _, vd, kv_rows, _, _, _, _ = blk
        vals = vd[kv_rows, :] if vd is not None else v_ref[0, kv_rows]
        return jnp.dot(e, vals * own[hh] + own[1 - hh], preferred_element_type=F32)

    def finish(blk, m, acc):
        o_dst, l_dst, idx = blk[5], blk[6], blk[7]
        num = jnp.where(lane < HEAD_DIM_A, acc[0], acc[1])
        den = pltpu.roll(jnp.where(lane < HEAD_DIM_A, acc[1], acc[0]), HEAD_DIM_A, 1)
        o_dst[idx] = num / den
        l_dst[idx] = jnp.where(lane < HEAD_DIM_A, m[0], m[1]) * LN2 + jnp.log(den)

    stage = {}
    for t in range(len(blocks) + ATTN_SKEW):
        if t < len(blocks):
            stage[t] = [scores_and_probs(blocks[t], hh) for hh in range(2)]
        d = t - ATTN_SKEW
        if d >= 0:
            acc = [weighted_values(blocks[d], hh, stage[d][hh][1]) for hh in range(2)]
            finish(blocks[d], [stage[d][hh][0] for hh in range(2)], acc)
            del stage[d]

    for r in range(N_RES):
        r4, c = r % 4, r // 4
        gather1 = lambda ref: jnp.concatenate([ref[n, 8 * r:8 * r + 8, :] for n in range(N_BLK)], axis=0)
        gather2 = lambda ref: jnp.concatenate(
            [ref[r4 * 4 + n, 32 * c:32 * c + 32, :] for n in range(4)], axis=0)
        o_b = [gather1(ob1), gather2(ob2), ob3[r]]
        l_b = [gather1(ls1), gather2(ls2), ls3[r]]
        top = jnp.maximum(jnp.maximum(l_b[0], l_b[1]), l_b[2])
        w_b = [jnp.exp(l - top) for l in l_b]
        num = w_b[0] * o_b[0] + w_b[1] * o_b[1] + w_b[2] * o_b[2]
        o_ref[0, r] = (num / (w_b[0] + w_b[1] + w_b[2])).astype(BF16)


def _dilated(qa, ka, va, t1, t2, t3):
    B = qa.shape[0]
    blk = pl.BlockSpec((1, N_RES, M_SUB, LANES), lambda b, h: (b, 0, 0, h))
    tab = lambda t: pl.BlockSpec((2,) + t.shape[1:], lambda b, h: (h, 0, 0))
    blocked_bf16 = pltpu.VMEM((N_BLK * BAND_BLOCK, LANES), BF16)
    blocked_f32 = pltpu.VMEM((N_BLK, BAND_BLOCK, LANES), F32)
    return pl.pallas_call(
        _dilated_kernel,
        grid=(B, N_HEADS_A // 2),
        in_specs=[blk, blk, blk, tab(t1), tab(t2), tab(t3)],
        out_specs=blk,
        out_shape=jax.ShapeDtypeStruct((B, N_RES, M_SUB, A_WIDTH), BF16),
        scratch_shapes=[blocked_bf16] * 6 + [blocked_f32] * 6,
        compiler_params=_cparams(("parallel", "parallel")),
        name="dilated_attn",
    )(qa, ka, va, t1, t2, t3)


R_W1, R_W2, R_E1, R_E2, R_RANK1, R_RANK2 = range(6)
ROUTE_ROWS = 8
ROUTE_GROUP_ROWS = 8
ROUTE_EXPERT_ROW0 = 8


def _outproj_kernel(oa_ref, om_ref, x_ref, wo_ref, g_ref, wrh_ref, wrl_ref, br_ref, tri_ref,
                    h_ref, xn_ref, route_ref, cnt_ref, perm_scr, carry_scr):
    @pl.when((pl.program_id(0) == 0) & (pl.program_id(1) == 0))
    def _():
        carry_scr[...] = jnp.zeros(carry_scr.shape, F32)

    n_chunks = A_WIDTH // LANES
    for r in range(N_RES):
        for c in range(n_chunks):
            perm_scr[c, _residue_rows(r), :] = oa_ref[0, r, :, c * LANES:(c + 1) * LANES].astype(F32)
    oa = jnp.concatenate([perm_scr[c] for c in range(n_chunks)], axis=1).astype(BF16)
    h = (x_ref[0] + jnp.dot(oa, wo_ref[0:A_WIDTH, :], preferred_element_type=F32)
         + jnp.dot(om_ref[0], wo_ref[A_WIDTH:, :], preferred_element_type=F32))
    h_ref[0] = h
    xn = _rms(h, D_MODEL) * g_ref[...]
    xn_ref[0] = xn.astype(BF16)
    hi = xn.astype(BF16)
    lo = (xn - hi.astype(F32)).astype(BF16)
    lg = (lax.dot_general(wrh_ref[...], hi, NT_DIMS, preferred_element_type=F32)
          + lax.dot_general(wrh_ref[...], lo, NT_DIMS, preferred_element_type=F32)
          + lax.dot_general(wrl_ref[...], hi, NT_DIMS, preferred_element_type=F32)) + br_ref[...]

    sub = lax.broadcasted_iota(jnp.int32, (EXPERTS_PER_GROUP, TM), 0).astype(F32)
    cmax = lambda t: jnp.max(t, axis=0, keepdims=True)
    cmin = lambda t: jnp.min(t, axis=0, keepdims=True)
    csum = lambda t: jnp.sum(t, axis=0, keepdims=True)
    none = float(EXPERTS_PER_GROUP)

    gl = lg[0:ROUTE_GROUP_ROWS]
    ge = jnp.exp(gl - cmax(gl))
    gsum = csum(ge)
    g_gate = 1.0 / gsum
    g_idx = cmin(jnp.where(ge / gsum == g_gate, sub, none))
    el = lg[ROUTE_EXPERT_ROW0:ROUTE_EXPERT_ROW0 + EXPERTS_PER_GROUP]
    for g in range(1, N_GROUPS):
        r0 = ROUTE_EXPERT_ROW0 + g * EXPERTS_PER_GROUP
        el = jnp.where(g_idx == float(g), lg[r0:r0 + EXPERTS_PER_GROUP], el)
    ee = jnp.exp(el - cmax(el))
    esum = csum(ee)
    eprob = ee / esum
    p1 = 1.0 / esum
    i1 = cmin(jnp.where(eprob == p1, sub, none))
    rest = jnp.where(sub == i1, -1.0, eprob)
    p2 = cmax(rest)
    i2 = cmin(jnp.where(rest == p2, sub, none))
    den = p1 + p2
    e1 = g_idx * EXPERTS_PER_GROUP + i1
    e2 = g_idx * EXPERTS_PER_GROUP + i2
    erow = lax.broadcasted_iota(jnp.int32, (N_EXPERTS, TM), 0).astype(F32)
    onehot = ((erow == e1) | (erow == e2)).astype(F32)
    before = jnp.dot(onehot.astype(BF16), tri_ref[...], preferred_element_type=F32) + carry_scr[:, 0:1]
    rank1 = csum(jnp.where(erow == e1, before, 0.0))
    rank2 = csum(jnp.where(erow == e2, before, 0.0))
    carry_scr[...] = carry_scr[...] + jnp.sum(onehot, axis=1, keepdims=True)
    cnt_ref[...] = carry_scr[...]

    record = jnp.zeros((ROUTE_ROWS, TM), F32)
    for i, val in ((R_W1, g_gate * (p1 / den)), (R_W2, g_gate * (p2 / den)), (R_E1, e1), (R_E2, e2),
                   (R_RANK1, rank1), (R_RANK2, rank2)):
        record = jnp.where(sub == float(i), val, record)
    route_ref[0] = record


def _outproj(oa, om, x, wo, g, wrh, wrl, br, tri):
    B = oa.shape[0]
    rows = lambda c: pl.BlockSpec((1, TM, c), lambda b, j: (b, j, 0))
    shp = lambda c, dt: jax.ShapeDtypeStruct((B, SEQ, c), dt)
    return pl.pallas_call(
        _outproj_kernel,
        grid=(B, N_ROW_STEPS),
        in_specs=[pl.BlockSpec((1, N_RES, M_STEP, A_WIDTH), lambda b, j: (b, 0, j, 0)),
                  rows(M_WIDTH), rows(D_MODEL), _full(wo), _full(g), _full(wrh), _full(wrl),
                  _full(br), _full(tri)],
        out_specs=[rows(D_MODEL), rows(D_MODEL),
                   pl.BlockSpec((1, ROUTE_ROWS, TM), lambda b, j: (b * N_ROW_STEPS + j, 0, 0)),
                   pl.BlockSpec((N_EXPERTS, LANES), lambda b, j: (0, 0))],
        out_shape=[shp(D_MODEL, F32), shp(D_MODEL, BF16),
                   jax.ShapeDtypeStruct((B * N_ROW_STEPS, ROUTE_ROWS, TM), F32),
                   jax.ShapeDtypeStruct((N_EXPERTS, LANES), F32)],
        scratch_shapes=[pltpu.VMEM((A_WIDTH // LANES, TM, LANES), F32), pltpu.VMEM((N_EXPERTS, LANES), F32)],
        compiler_params=_cparams(("arbitrary", "arbitrary")),
        name="outproj_router",
    )(oa, om, x, wo, g, wrh, wrl, br, tri)


def _expert_kernel(blk_e_ref, n_used_ref, x_ref, wg_ref, wu_ref, wd_ref, y_ref, wg_s, wu_s, wd_s):
    i = pl.program_id(0)

    @pl.when(i < n_used_ref[0])
    def _():
        @pl.when((i == 0) | (blk_e_ref[i] != blk_e_ref[jnp.maximum(i - 1, 0)]))
        def _():
            wg_s[...] = wg_ref[0].astype(BF16)
            wu_s[...] = wu_ref[0].astype(BF16)
            wd_s[...] = wd_ref[0].astype(BF16)

        x = x_ref[...]
        gate = jnp.dot(x, wg_s[...], preferred_element_type=F32)
        up = jnp.dot(x, wu_s[...], preferred_element_type=F32)
        hdn = (gate * jax.nn.sigmoid(gate) * up).astype(BF16)
        y_ref[...] = jnp.dot(hdn, wd_s[...], preferred_element_type=F32).astype(y_ref.dtype)

    @pl.when(i >= n_used_ref[0])
    def _():
        y_ref[...] = jnp.zeros(y_ref.shape, y_ref.dtype)


def _experts(blk_e, n_used, xs, wg, wu, wd):
    n_blocks = xs.shape[0] // MOE_TM
    row_in = lambda i, be, nu: (jnp.minimum(i, nu[0] - 1), 0)
    wsel = lambda i, be, nu: (be[i], 0, 0)
    return pl.pallas_call(
        _expert_kernel,
        grid_spec=pltpu.PrefetchScalarGridSpec(
            num_scalar_prefetch=2,
            grid=(n_blocks,),
            in_specs=[pl.BlockSpec((MOE_TM, D_MODEL), row_in),
                      pl.BlockSpec((1, D_MODEL, EXPERT_FF), wsel),
                      pl.BlockSpec((1, D_MODEL, EXPERT_FF), wsel),
                      pl.BlockSpec((1, EXPERT_FF, D_MODEL), wsel)],
            out_specs=pl.BlockSpec((MOE_TM, D_MODEL), lambda i, be, nu: (i, 0)),
            scratch_shapes=[pltpu.VMEM((D_MODEL, EXPERT_FF), BF16), pltpu.VMEM((D_MODEL, EXPERT_FF), BF16),
                            pltpu.VMEM((EXPERT_FF, D_MODEL), BF16)]),
        out_shape=jax.ShapeDtypeStruct(xs.shape, BF16),
        compiler_params=_cparams(("arbitrary",)),
        name="expert_ffn",
    )(blk_e, n_used, xs, wg, wu, wd)


def _ple_kernel(h_ref, y1_ref, y2_ref, route_ref, p_ref, wp_ref, gp_ref, wg_ref, bg_ref, o_ref):
    rec = jnp.concatenate([route_ref[0], jnp.zeros((LANES - ROUTE_ROWS, TM), F32)], axis=0)
    route = jnp.concatenate([rec[:, c * LANES:(c + 1) * LANES].T for c in range(TM // LANES)], axis=0)
    h = (h_ref[0] + route[:, R_W1:R_W1 + 1] * y1_ref[0].astype(F32)
         + route[:, R_W2:R_W2 + 1] * y2_ref[0].astype(F32))
    e = _rms(jnp.dot(p_ref[0].astype(BF16), wp_ref[...], preferred_element_type=F32), D_MODEL) * gp_ref[...]
    g = jax.nn.sigmoid(jnp.dot(h.astype(BF16), wg_ref[...], preferred_element_type=F32) + bg_ref[...])
    o_ref[0] = h + g * e


def _ple(h, y_tok, route, p, wp, gp, wg, bg):
    B = h.shape[0]
    rows = lambda c: pl.BlockSpec((1, TM, c), lambda b, j: (b, j, 0))
    step = lambda b, j: b * N_ROW_STEPS + j
    y_k = lambda k: pl.BlockSpec((1, TM, D_MODEL), lambda b, j: (k, step(b, j), 0))
    return pl.pallas_call(
        _ple_kernel,
        grid=(B, N_ROW_STEPS),
        in_specs=[rows(D_MODEL), y_k(0), y_k(1),
                  pl.BlockSpec((1, ROUTE_ROWS, TM), lambda b, j: (step(b, j), 0, 0)),
                  rows(PLE_DIM), _full(wp), _full(gp), _full(wg), _full(bg)],
        out_specs=rows(D_MODEL),
        out_shape=jax.ShapeDtypeStruct((B, SEQ, D_MODEL), F32),
        compiler_params=_cparams(("parallel", "parallel")),
        name="ple_gate",
    )(h, y_tok, y_tok, route, p, wp, gp, wg, bg)


def _t5_bucket(dist):
    max_exact = REL_BUCKETS // 2
    n = jnp.maximum(dist, 0)
    nf = jnp.maximum(n, 1).astype(F32)
    large = max_exact + (jnp.log(nf / max_exact) / math.log(REL_MAX_DISTANCE / max_exact)
                         * (REL_BUCKETS - max_exact)).astype(jnp.int32)
    large = jnp.minimum(large, REL_BUCKETS - 1)
    return jnp.where(n < max_exact, n, large)


def _bias_table(rel_bias, local_index, dilation, with_prev):
    loc = np.asarray(local_index)
    delta = loc[:, None] - loc[None, :]
    if with_prev:
        delta = np.concatenate([delta + BAND_BLOCK, delta], axis=1)
    ok = (delta >= 0) & (delta <= BAND_BLOCK)
    bucket = _t5_bucket(jnp.asarray(delta * dilation, jnp.int32))
    picked = jnp.where(bucket[None, :, :, None] == jnp.arange(REL_BUCKETS, dtype=jnp.int32),
                       rel_bias.astype(F32).T[:, None, None, :], 0.0)
    return jnp.where(jnp.asarray(ok)[None], jnp.sum(picked, axis=-1) * LOG2E, NEG)


def _block_diag_ones(sizes, total):
    g = np.zeros((total, total), np.float32)
    o = 0
    for s, on in sizes:
        if on:
            g[o:o + s, o:o + s] = 1.0
        o += s
    return jnp.asarray(g, BF16)


def _dispatch_plan(route, counts, n_tokens):
    field = lambda i: route[:, i, :].reshape(n_tokens).astype(jnp.int32)
    e = jnp.stack([field(R_E1), field(R_E2)], axis=-1)
    rank = jnp.stack([field(R_RANK1), field(R_RANK2)], axis=-1)
    pcounts = (counts + MOE_TM - 1) // MOE_TM * MOE_TM
    pend = jnp.cumsum(pcounts)
    pstart = pend - pcounts
    ids = jnp.arange(N_EXPERTS, dtype=jnp.int32)
    pos = rank + jnp.sum(jnp.where(e[..., None] == ids, pstart, 0), axis=-1)
    n_assign = n_tokens * TOP_K
    shift = int(math.ceil(math.log2(n_assign)))
    keys = (e.reshape(-1) << shift) | jnp.arange(n_assign, dtype=jnp.int32)
    tok_sorted = (jnp.sort(keys) & ((1 << shift) - 1)) // TOP_K
    n_rows = n_assign + N_EXPERTS * MOE_TM
    rows = jnp.arange(n_rows, dtype=jnp.int32)
    row_e = jnp.sum(rows[:, None] >= pend[None, :], axis=-1)
    row_e = jnp.minimum(row_e, N_EXPERTS - 1)
    pick = lambda tbl: jnp.sum(jnp.where(row_e[:, None] == ids, tbl, 0), axis=-1)
    within = rows - pick(pstart)
    src = jnp.clip(pick(jnp.cumsum(counts) - counts) + within, 0, n_assign - 1)
    row_tok = jnp.where(within < pick(counts), tok_sorted[src], rows % n_tokens)
    n_blocks = n_rows // MOE_TM
    n_used = (pend[-1] // MOE_TM).astype(jnp.int32)
    blk_start = jnp.minimum(jnp.arange(n_blocks, dtype=jnp.int32), n_used - 1) * MOE_TM
    blk_e = jnp.minimum(jnp.sum(blk_start[:, None] >= pend[None, :], axis=-1), N_EXPERTS - 1).astype(jnp.int32)
    return row_tok, pos, blk_e, n_used.reshape(1)


def kernel(x, p, rel_bias, norm_mix_gain, w_in, qn_a_gain, kn_a_gain, q_a_gain, w_q_up, kv_a_gain, w_kv_up, qn_nope_gain, qn_rope_gain, kn_nope_gain, kn_rope_gain, w_out, norm_ffn_gain, w_router_group, b_router_group, w_router_expert, b_router_expert, w_exp_gate, w_exp_up, w_exp_down, w_ple_proj, ple_norm_gain, w_ple_gate, b_ple_gate):
    B, S, D = x.shape
    assert (S, D) == (SEQ, D_MODEL) and p.shape[0] == 1
    n_tokens = B * S
    row = lambda a: a.reshape(1, -1).astype(F32)
    zeros = lambda *s: jnp.zeros(s, F32)

    w_in_p = jnp.concatenate([w_in[0, :, :C_KR0], zeros(D, 64), w_in[0, :, C_KR0:], zeros(D, 32)],
                             axis=1).astype(BF16)
    gsum_a = _block_diag_ones([(HEAD_DIM_A, True)] * N_HEADS_A, A_WIDTH)
    gq_a = row(jnp.tile(qn_a_gain[0], N_HEADS_A)) * (HEAD_DIM_A ** -0.5 * LOG2E)
    gk_a = row(jnp.tile(kn_a_gain[0], N_HEADS_A))

    wq_p = jnp.pad(w_q_up[0].reshape(Q_LORA, N_HEADS_M, NOPE_DIM + ROPE_DIM),
                   ((0, 0), (0, 0), (0, HEAD_SLOT - NOPE_DIM - ROPE_DIM))).reshape(Q_LORA, -1).astype(BF16)
    wkv = w_kv_up[0].reshape(KV_LORA, N_HEADS_M, NOPE_DIM + V_DIM)
    wk_p = jnp.pad(wkv[..., :NOPE_DIM], ((0, 0), (0, 0), (0, HEAD_SLOT - NOPE_DIM))).reshape(KV_LORA, -1)
    wkv_p = jnp.concatenate([wk_p, wkv[..., NOPE_DIM:].reshape(KV_LORA, -1)], axis=1).astype(BF16)
    gs_q = _block_diag_ones([(NOPE_DIM, True), (ROPE_DIM, True), (32, False)], HEAD_SLOT)
    gs_k = _block_diag_ones([(NOPE_DIM, True), (64, False)], HEAD_SLOT)
    inv_cnt_q = jnp.asarray(np.concatenate([np.full(64, 1 / NOPE_DIM), np.full(32, 1 / ROPE_DIM),
                                            np.ones(32)]).astype(np.float32)).reshape(1, HEAD_SLOT)
    mla_scale = (NOPE_DIM + ROPE_DIM) ** -0.5 * LOG2E
    gq_m = row(jnp.concatenate([qn_nope_gain[0], qn_rope_gain[0], zeros(32)])) * mla_scale
    gk_m = row(jnp.concatenate([kn_nope_gain[0], zeros(64)]))
    gkr_m = row(jnp.concatenate([zeros(64), kn_rope_gain[0], zeros(32)]))

    half = ROPE_DIM // 2
    inv = 1.0 / (ROPE_THETA ** (jnp.arange(half, dtype=F32) * 2.0 / ROPE_DIM))
    ang = jnp.arange(S, dtype=jnp.int32).astype(F32)[:, None] * inv[None, :]
    cosv, sinv = jnp.cos(ang), jnp.sin(ang)
    cos_t = jnp.concatenate([jnp.ones((S, 64), F32), cosv, cosv, jnp.ones((S, 32), F32)], -1)
    sin_a = jnp.concatenate([zeros(S, 80), sinv, zeros(S, 32)], -1)
    sin_b = jnp.concatenate([zeros(S, 64), -sinv, zeros(S, 48)], -1)

    loc1 = [16 * a + r for r in range(16) for a in range(8)]
    loc2 = [4 * a + c for c in range(4) for a in range(32)]
    loc3 = list(range(BAND_BLOCK))
    t1 = _bias_table(rel_bias, loc1, 1, True)
    t2 = _bias_table(rel_bias, loc2, 4, True)
    t3 = jnp.concatenate([jnp.full((N_HEADS_A, BAND_BLOCK, BAND_BLOCK), NEG, F32),
                          _bias_table(rel_bias, loc3, 16, False)], axis=-1)

    n_pad_g, n_pad_e = ROUTE_GROUP_ROWS - N_GROUPS, LANES - ROUTE_EXPERT_ROW0 - N_EXPERTS
    w_r = jnp.concatenate([w_router_group[0].T, zeros(n_pad_g, D), w_router_expert[0].T, zeros(n_pad_e, D)], axis=0)
    w_r_hi = w_r.astype(BF16)
    w_r_lo = (w_r - w_r_hi.astype(F32)).astype(BF16)
    b_r = jnp.concatenate([b_router_group[0], jnp.full((n_pad_g,), NEG, F32), b_router_expert[0],
                           zeros(n_pad_e)]).reshape(LANES, 1)
    tri = jnp.asarray(np.triu(np.ones((TM, TM), np.float32), 1), BF16)

    qa, ka, va, cq, ckv, kr = _inproj(x, row(norm_mix_gain[0]), w_in_p, gsum_a, gq_a, gk_a,
                                      row(q_a_gain[0]), row(kv_a_gain[0]))
    o_a = _dilated(qa, ka, va, t1, t2, t3)
    q_m, k_m, v_m = _mla_prep(cq, ckv, kr, wq_p, wkv_p, gs_q, gs_k, inv_cnt_q, gq_m, gk_m, gkr_m,
                              cos_t, sin_a, sin_b)
    o_m = _mla_attn(q_m, k_m, v_m)
    h1, xn2, route, cnt = _outproj(o_a, o_m, x, w_out[0].astype(BF16), row(norm_ffn_gain[0]),
                                   w_r_hi, w_r_lo, b_r, tri)

    row_tok, pos, blk_e, n_used = _dispatch_plan(route, cnt[:, 0].astype(jnp.int32), n_tokens)
    xs = xn2.reshape(n_tokens, D)[row_tok]
    y = _experts(blk_e, n_used, xs, w_exp_gate[0], w_exp_up[0], w_exp_down[0])
    y_tok = y[pos.T.reshape(-1)].reshape(TOP_K, n_tokens, D)
    return _ple(h1, y_tok, route, p[0], w_ple_proj[0].astype(BF16), row(ple_norm_gain[0]),
                w_ple_gate[0].astype(BF16), row(b_ple_gate[0]))
```

```python
import functools
import math

import jax
import jax.numpy as jnp
import numpy as np
from jax import lax
from jax.experimental import pallas as pl
from jax.experimental.pallas import tpu as pltpu

F32 = jnp.float32
BF16 = jnp.bfloat16

D_MODEL = 1024
SEQ = 2048
PLE_DIM = 256
EPS = 1e-6
NEG = -1e30
HEAD_DIM_A = 64
A_WIDTH = 512
N_HEADS_A = 8
BAND_BLOCK = 128
REL_BUCKETS = 32
REL_MAX_DISTANCE = 2048
M_WIDTH = 512
V_DIM = 64
N_HEADS_M = 8
Q_LORA = 384
KV_LORA = 256
NOPE_DIM = 64
ROPE_DIM = 32
ROPE_THETA = 10000.0
N_GROUPS = 4
EXPERTS_PER_GROUP = 8
N_EXPERTS = 32
TOP_K = 2
EXPERT_FF = 512

LANES = 128
N_RES = 16
M_SUB = SEQ // N_RES
TM = 512
N_ROW_STEPS = SEQ // TM
M_STEP = TM // N_RES
HEAD_SLOT = LANES
MLA_TQ = 512
MOE_TM = 512
ATTN_SKEW = 2
BATCH_PARTS = 2
VMEM_LIMIT = 48 * 1024 * 1024

IN_COLS_PAD = 3 * A_WIDTH + Q_LORA + KV_LORA + LANES
C_Q0, C_K0, C_V0, C_CQ0, C_CKV0, C_KR0 = 0, 512, 1024, 1536, 1920, 2176

NT_DIMS = (((1,), (1,)), ((), ()))
LOG2E = math.log2(math.e)
LN2 = math.log(2.0)


def _cparams(sem):
    return pltpu.CompilerParams(dimension_semantics=sem, vmem_limit_bytes=VMEM_LIMIT)


def _full(a):
    return pl.BlockSpec(a.shape, lambda *_: (0,) * a.ndim)


def _rms(x, n):
    return x * lax.rsqrt(jnp.sum(x * x, axis=-1, keepdims=True) * (1.0 / n) + EPS)


def _residue_rows(r):
    return pl.ds(r, M_STEP, stride=N_RES)


def _inproj_kernel(x_ref, g_ref, w_ref, gsum_ref, gq_ref, gk_ref, gcq_ref, gckv_ref,
                   qa_ref, ka_ref, va_ref, cq_ref, ckv_ref, kr_ref, perm_scr):
    xn = (_rms(x_ref[0], D_MODEL) * g_ref[...]).astype(BF16)

    def proj(c0, c1):
        return jnp.dot(xn, w_ref[:, c0:c1], preferred_element_type=F32)

    def head_norm(t, gain_ref):
        ss = jnp.dot((t * t).astype(BF16), gsum_ref[...], preferred_element_type=F32)
        return t * lax.rsqrt(ss * (1.0 / HEAD_DIM_A) + EPS) * gain_ref[...]

    def put_residue(ref, val):
        for c in range(A_WIDTH // LANES):
            perm_scr[c] = val[:, c * LANES:(c + 1) * LANES]
        for r in range(N_RES):
            for c in range(A_WIDTH // LANES):
                ref[0, r, :, c * LANES:(c + 1) * LANES] = perm_scr[c, _residue_rows(r), :].astype(ref.dtype)

    put_residue(qa_ref, head_norm(proj(C_Q0, C_K0), gq_ref))
    put_residue(ka_ref, head_norm(proj(C_K0, C_V0), gk_ref))
    put_residue(va_ref, proj(C_V0, C_CQ0))
    cq_ref[0] = (_rms(proj(C_CQ0, C_CKV0), Q_LORA) * gcq_ref[...]).astype(BF16)
    ckv_ref[0] = (_rms(proj(C_CKV0, C_KR0), KV_LORA) * gckv_ref[...]).astype(BF16)
    kr_ref[0] = proj(C_KR0, IN_COLS_PAD)


def _inproj(x, b0, B, g, w_in_p, gsum, gq, gk, gcq, gckv):
    res = lambda: (jax.ShapeDtypeStruct((B, N_RES, M_SUB, A_WIDTH), BF16),
                   pl.BlockSpec((1, N_RES, M_STEP, A_WIDTH), lambda b, j: (b, 0, j, 0)))
    nat = lambda c, dt: (jax.ShapeDtypeStruct((B, SEQ, c), dt),
                         pl.BlockSpec((1, TM, c), lambda b, j: (b, j, 0)))
    outs = [res(), res(), res(), nat(Q_LORA, BF16), nat(KV_LORA, BF16), nat(LANES, F32)]
    return pl.pallas_call(
        _inproj_kernel,
        grid=(B, N_ROW_STEPS),
        in_specs=[pl.BlockSpec((1, TM, D_MODEL), lambda b, j: (b + b0, j, 0)),
                  _full(g), _full(w_in_p), _full(gsum), _full(gq), _full(gk), _full(gcq), _full(gckv)],
        out_specs=[o[1] for o in outs],
        out_shape=[o[0] for o in outs],
        scratch_shapes=[pltpu.VMEM((A_WIDTH // LANES, TM, LANES), F32)],
        compiler_params=_cparams(("parallel", "parallel")),
        name="inproj",
    )(x, g, w_in_p, gsum, gq, gk, gcq, gckv)


def _mla_prep_kernel(cq_ref, ckv_ref, kr_ref, wq_ref, wkv_ref, gsq_ref, gsk_ref, icq_ref,
                     gq_ref, gk_ref, gkr_ref, cos_ref, sa_ref, sb_ref, q_ref, k_ref, v_ref):
    cos, sa, sb = cos_ref[...], sa_ref[...], sb_ref[...]
    lane = lax.broadcasted_iota(jnp.int32, (TM, LANES), 1)

    def rope(t):
        return t * cos + pltpu.roll(t, 16, 1) * sa + pltpu.roll(t, LANES - 16, 1) * sb

    q = jnp.dot(cq_ref[0], wq_ref[...], preferred_element_type=F32)
    kv = jnp.dot(ckv_ref[0], wkv_ref[...], preferred_element_type=F32)
    k_rope = rope(_rms(kr_ref[0], ROPE_DIM) * gkr_ref[...])
    for h in range(N_HEADS_M):
        sl = slice(h * HEAD_SLOT, (h + 1) * HEAD_SLOT)
        qh = q[:, sl]
        ss = jnp.dot((qh * qh).astype(BF16), gsq_ref[...], preferred_element_type=F32)
        q_ref[0, :, sl] = rope(qh * lax.rsqrt(ss * icq_ref[...] + EPS) * gq_ref[...]).astype(BF16)
        kh = kv[:, sl]
        ssk = jnp.dot((kh * kh).astype(BF16), gsk_ref[...], preferred_element_type=F32)
        kn = kh * lax.rsqrt(ssk * (1.0 / NOPE_DIM) + EPS) * gk_ref[...] + k_rope
        k_ref[0, :, sl] = kn.astype(BF16)
    for hp in range(N_HEADS_M // 2):
        v_pair = kv[:, N_HEADS_M * HEAD_SLOT + hp * LANES:N_HEADS_M * HEAD_SLOT + (hp + 1) * LANES]
        v_ref[0, :, (2 * hp) * HEAD_SLOT:(2 * hp + 1) * HEAD_SLOT] = jnp.where(lane < V_DIM, v_pair, 1.0).astype(BF16)
        v_ref[0, :, (2 * hp + 1) * HEAD_SLOT:(2 * hp + 2) * HEAD_SLOT] = jnp.where(lane < V_DIM, 1.0, v_pair).astype(BF16)


def _mla_prep(cq, ckv, kr, wq_p, wkv_p, gsq, gsk, icq, gq, gk, gkr, cos, sa, sb):
    B = cq.shape[0]
    rows = lambda c: pl.BlockSpec((1, TM, c), lambda b, j: (b, j, 0))
    tab = pl.BlockSpec((TM, LANES), lambda b, j: (j, 0))
    wide = N_HEADS_M * HEAD_SLOT
    return pl.pallas_call(
        _mla_prep_kernel,
        grid=(B, N_ROW_STEPS),
        in_specs=[rows(Q_LORA), rows(KV_LORA), rows(LANES), _full(wq_p), _full(wkv_p), _full(gsq),
                  _full(gsk), _full(icq), _full(gq), _full(gk), _full(gkr), tab, tab, tab],
        out_specs=[rows(wide)] * 3,
        out_shape=[jax.ShapeDtypeStruct((B, SEQ, wide), BF16)] * 3,
        compiler_params=_cparams(("parallel", "parallel")),
        name="mla_prep",
    )(cq, ckv, kr, wq_p, wkv_p, gsq, gsk, icq, gq, gk, gkr, cos, sa, sb)


def _mla_attn_kernel(q_ref, k_ref, v_ref, o_ref):
    n_q = SEQ // MLA_TQ
    row = lax.broadcasted_iota(jnp.int32, (MLA_TQ, MLA_TQ), 0)
    col = lax.broadcasted_iota(jnp.int32, (MLA_TQ, MLA_TQ), 1)
    lane = lax.broadcasted_iota(jnp.int32, (MLA_TQ, LANES), 1)
    heads = [slice(hh * HEAD_SLOT, (hh + 1) * HEAD_SLOT) for hh in range(2)]

    def probs(i, hh):
        n_keys = (i + 1) * MLA_TQ
        s = lax.dot_general(q_ref[0, i * MLA_TQ:n_keys, heads[hh]], k_ref[0, 0:n_keys, heads[hh]],
                            NT_DIMS, preferred_element_type=F32)
        diag = jnp.where(col <= row, s[:, n_keys - MLA_TQ:], NEG)
        s = diag if i == 0 else jnp.concatenate([s[:, :n_keys - MLA_TQ], diag], axis=1)
        return jnp.exp2(s - jnp.max(s, axis=-1, keepdims=True)).astype(BF16)

    def values(i, hh, p):
        return jnp.dot(p, v_ref[0, 0:(i + 1) * MLA_TQ, heads[hh]], preferred_element_type=F32)

    units = [(i, hh) for i in range(n_q) for hh in range(2)]
    acc, pending = {}, None
    for u in units:
        p = probs(*u)
        if pending is not None:
            acc[pending[0]] = values(*pending[0], pending[1])
        pending = (u, p)
    acc[pending[0]] = values(*pending[0], pending[1])
    for i in range(n_q):
        num = jnp.where(lane < V_DIM, acc[(i, 0)], acc[(i, 1)])
        den = pltpu.roll(jnp.where(lane < V_DIM, acc[(i, 1)], acc[(i, 0)]), V_DIM, 1)
        o_ref[0, i * MLA_TQ:(i + 1) * MLA_TQ, :] = (num / den).astype(BF16)


def _mla_attn(q, k, v):
    B = q.shape[0]
    pair = lambda c: pl.BlockSpec((1, SEQ, c), lambda b, h: (b, 0, h))
    return pl.pallas_call(
        _mla_attn_kernel,
        grid=(B, N_HEADS_M // 2),
        in_specs=[pair(2 * HEAD_SLOT)] * 3,
        out_specs=pair(2 * V_DIM),
        out_shape=jax.ShapeDtypeStruct((B, SEQ, M_WIDTH), BF16),
        compiler_params=_cparams(("parallel", "parallel")),
        name="mla_attn",
    )(q, k, v)


N_BLK = SEQ // BAND_BLOCK


def _dilated_kernel(q_ref, k_ref, v_ref, t1_ref, t2_ref, t3_ref, o_ref,
                    q1, k1, v1, q2, k2, v2, ob1, ls1, ob2, ls2, ob3, ls3):
    bb = BAND_BLOCK
    lane = lax.broadcasted_iota(jnp.int32, (bb, LANES), 1)
    lane_row = lax.broadcasted_iota(jnp.int32, (1, LANES), 1)
    own = [(lane_row < HEAD_DIM_A).astype(BF16), (lane_row >= HEAD_DIM_A).astype(BF16)]
    rows = lambda lo, hi: slice(lo * bb, hi * bb)

    for src, d1, d2 in ((q_ref, q1, q2), (k_ref, k1, k2), (v_ref, v1, v2)):
        for n2 in range(N_BLK // 2):
            pieces = [src[0, r, 16 * n2:16 * n2 + 16, :].astype(F32) for r in range(N_RES)]
            d1[rows(2 * n2, 2 * n2 + 1), :] = jnp.concatenate([p[0:8] for p in pieces], axis=0).astype(BF16)
            d1[rows(2 * n2 + 1, 2 * n2 + 2), :] = jnp.concatenate([p[8:16] for p in pieces], axis=0).astype(BF16)
        for r4 in range(4):
            for n in range(4):
                d2[rows(r4 * 4 + n, r4 * 4 + n + 1), :] = jnp.concatenate(
                    [src[0, r4 + 4 * c, 32 * n:32 * n + 32, :] for c in range(4)], axis=0)

    blocks = []
    for idx in range(N_BLK):
        lo = idx - 1 if idx > 0 else idx
        blocks.append((q1, k1, v1, rows(lo, idx + 1), t1_ref, ob1, ls1, idx))
    for idx in range(N_BLK):
        lo = idx - 1 if idx % 4 else idx
        blocks.append((q2, k2, v2, rows(lo, idx + 1), t2_ref, ob2, ls2, idx))
    for r in range(N_RES):
        blocks.append((None, None, None, r, t3_ref, ob3, ls3, r))

    def scores_and_probs(blk, hh):
        qd, kd, _, kv_rows, t_ref, _, _, idx = blk
        q = qd[rows(idx, idx + 1), :] if qd is not None else q_ref[0, idx]
        keys = kd[kv_rows, :] if kd is not None else k_ref[0, kv_rows]
        n_keys = keys.shape[0]
        s = lax.dot_general(q * own[hh], keys, NT_DIMS, preferred_element_type=F32)
        s = s + t_ref[hh, :, 2 * bb - n_keys:2 * bb]
        m = jnp.max(s, axis=-1, keepdims=True)
        return m, jnp.exp2(s - m).astype(BF16)

    def weighted_values(blk, hh, e):
        _, _, vd, kv_rows, _, _, _, _ = blk
        vals = vd[kv_rows, :] if vd is not None else v_ref[0, kv_rows]
        return jnp.dot(e, vals * own[hh] + own[1 - hh], preferred_element_type=F32)

    def finish(blk, m, acc):
        o_dst, l_dst, idx = blk[5], blk[6], blk[7]
        num = jnp.where(lane < HEAD_DIM_A, acc[0], acc[1])
        den = pltpu.roll(jnp.where(lane < HEAD_DIM_A, acc[1], acc[0]), HEAD_DIM_A, 1)
        o_dst[idx] = num / den
        l_dst[idx] = jnp.where(lane < HEAD_DIM_A, m[0], m[1]) * LN2 + jnp.log(den)

    stage = {}
    for t in range(len(blocks) + ATTN_SKEW):
        if t < len(blocks):
            stage[t] = [scores_and_probs(blocks[t], hh) for hh in range(2)]
        d = t - ATTN_SKEW
        if d >= 0:
            acc = [weighted_values(blocks[d], hh, stage[d][hh][1]) for hh in range(2)]
            finish(blocks[d], [stage[d][hh][0] for hh in range(2)], acc)
            del stage[d]

    for r in range(N_RES):
        r4, c = r % 4, r // 4
        gather1 = lambda ref: jnp.concatenate([ref[n, 8 * r:8 * r + 8, :] for n in range(N_BLK)], axis=0)
        gather2 = lambda ref: jnp.concatenate(
            [ref[r4 * 4 + n, 32 * c:32 * c + 32, :] for n in range(4)], axis=0)
        o_b = [gather1(ob1), gather2(ob2), ob3[r]]
        l_b = [gather1(ls1), gather2(ls2), ls3[r]]
        top = jnp.maximum(jnp.maximum(l_b[0], l_b[1]), l_b[2])
        w_b = [jnp.exp(l - top) for l in l_b]
        num = w_b[0] * o_b[0] + w_b[1] * o_b[1] + w_b[2] * o_b[2]
        o_ref[0, r] = (num / (w_b[0] + w_b[1] + w_b[2])).astype(BF16)


def _dilated(qa, ka, va, t1, t2, t3):
    B = qa.shape[0]
    blk = pl.BlockSpec((1, N_RES, M_SUB, LANES), lambda b, h: (b, 0, 0, h))
    tab = lambda t: pl.BlockSpec((2,) + t.shape[1:], lambda b, h: (h, 0, 0))
    blocked_bf16 = pltpu.VMEM((N_BLK * BAND_BLOCK, LANES), BF16)
    blocked_f32 = pltpu.VMEM((N_BLK, BAND_BLOCK, LANES), F32)
    return pl.pallas_call(
        _dilated_kernel,
        grid=(B, N_HEADS_A // 2),
        in_specs=[blk, blk, blk, tab(t1), tab(t2), tab(t3)],
        out_specs=blk,
        out_shape=jax.ShapeDtypeStruct((B, N_RES, M_SUB, A_WIDTH), BF16),
        scratch_shapes=[blocked_bf16] * 6 + [blocked_f32] * 6,
        compiler_params=_cparams(("parallel", "parallel")),
        name="dilated_attn",
    )(qa, ka, va, t1, t2, t3)


R_W1, R_W2, R_E1, R_E2, R_RANK1, R_RANK2 = range(6)
ROUTE_ROWS = 8
ROUTE_GROUP_ROWS = 8
ROUTE_EXPERT_ROW0 = 8


def _outproj_kernel(oa_ref, om_ref, x_ref, wo_ref, g_ref, wrh_ref, wrl_ref, br_ref, tri_ref,
                    h_ref, xn_ref, route_ref, cnt_ref, perm_scr, carry_scr):
    @pl.when((pl.program_id(0) == 0) & (pl.program_id(1) == 0))
    def _():
        carry_scr[...] = jnp.zeros(carry_scr.shape, F32)

    n_chunks = A_WIDTH // LANES
    for r in range(N_RES):
        for c in range(n_chunks):
            perm_scr[c, _residue_rows(r), :] = oa_ref[0, r, :, c * LANES:(c + 1) * LANES].astype(F32)
    oa = jnp.concatenate([perm_scr[c] for c in range(n_chunks)], axis=1).astype(BF16)
    h = (x_ref[0] + jnp.dot(oa, wo_ref[0:A_WIDTH, :], preferred_element_type=F32)
         + jnp.dot(om_ref[0], wo_ref[A_WIDTH:, :], preferred_element_type=F32))
    h_ref[0] = h
    xn = _rms(h, D_MODEL) * g_ref[...]
    xn_ref[0] = xn.astype(BF16)
    hi = xn.astype(BF16)
    lo = (xn - hi.astype(F32)).astype(BF16)
    lg = (lax.dot_general(wrh_ref[...], hi, NT_DIMS, preferred_element_type=F32)
          + lax.dot_general(wrh_ref[...], lo, NT_DIMS, preferred_element_type=F32)
          + lax.dot_general(wrl_ref[...], hi, NT_DIMS, preferred_element_type=F32)) + br_ref[...]

    sub = lax.broadcasted_iota(jnp.int32, (EXPERTS_PER_GROUP, TM), 0).astype(F32)
    cmax = lambda t: jnp.max(t, axis=0, keepdims=True)
    cmin = lambda t: jnp.min(t, axis=0, keepdims=True)
    csum = lambda t: jnp.sum(t, axis=0, keepdims=True)
    none = float(EXPERTS_PER_GROUP)

    gl = lg[0:ROUTE_GROUP_ROWS]
    ge = jnp.exp(gl - cmax(gl))
    gsum = csum(ge)
    g_gate = 1.0 / gsum
    g_idx = cmin(jnp.where(ge / gsum == g_gate, sub, none))
    el = lg[ROUTE_EXPERT_ROW0:ROUTE_EXPERT_ROW0 + EXPERTS_PER_GROUP]
    for g in range(1, N_GROUPS):
        r0 = ROUTE_EXPERT_ROW0 + g * EXPERTS_PER_GROUP
        el = jnp.where(g_idx == float(g), lg[r0:r0 + EXPERTS_PER_GROUP], el)
    ee = jnp.exp(el - cmax(el))
    esum = csum(ee)
    eprob = ee / esum
    p1 = 1.0 / esum
    i1 = cmin(jnp.where(eprob == p1, sub, none))
    rest = jnp.where(sub == i1, -1.0, eprob)
    p2 = cmax(rest)
    i2 = cmin(jnp.where(rest == p2, sub, none))
    den = p1 + p2
    e1 = g_idx * EXPERTS_PER_GROUP + i1
    e2 = g_idx * EXPERTS_PER_GROUP + i2
    erow = lax.broadcasted_iota(jnp.int32, (N_EXPERTS, TM), 0).astype(F32)
    onehot = ((erow == e1) | (erow == e2)).astype(F32)
    before = jnp.dot(onehot.astype(BF16), tri_ref[...], preferred_element_type=F32) + carry_scr[:, 0:1]
    rank1 = csum(jnp.where(erow == e1, before, 0.0))
    rank2 = csum(jnp.where(erow == e2, before, 0.0))
    carry_scr[...] = carry_scr[...] + jnp.sum(onehot, axis=1, keepdims=True)
    cnt_ref[...] = carry_scr[...]

    record = jnp.zeros((ROUTE_ROWS, TM), F32)
    for i, val in ((R_W1, g_gate * (p1 / den)), (R_W2, g_gate * (p2 / den)), (R_E1, e1), (R_E2, e2),
                   (R_RANK1, rank1), (R_RANK2, rank2)):
        record = jnp.where(sub == float(i), val, record)
    route_ref[0] = record


def _outproj(oa, om, x, b0, wo, g, wrh, wrl, br, tri):
    B = oa.shape[0]
    rows = lambda c: pl.BlockSpec((1, TM, c), lambda b, j: (b, j, 0))
    shp = lambda c, dt: jax.ShapeDtypeStruct((B, SEQ, c), dt)
    return pl.pallas_call(
        _outproj_kernel,
        grid=(B, N_ROW_STEPS),
        in_specs=[pl.BlockSpec((1, N_RES, M_STEP, A_WIDTH), lambda b, j: (b, 0, j, 0)),
                  rows(M_WIDTH), pl.BlockSpec((1, TM, D_MODEL), lambda b, j: (b + b0, j, 0)),
                  _full(wo), _full(g), _full(wrh), _full(wrl), _full(br), _full(tri)],
        out_specs=[rows(D_MODEL), rows(D_MODEL),
                   pl.BlockSpec((1, ROUTE_ROWS, TM), lambda b, j: (b * N_ROW_STEPS + j, 0, 0)),
                   pl.BlockSpec((N_EXPERTS, LANES), lambda b, j: (0, 0))],
        out_shape=[shp(D_MODEL, F32), shp(D_MODEL, BF16),
                   jax.ShapeDtypeStruct((B * N_ROW_STEPS, ROUTE_ROWS, TM), F32),
                   jax.ShapeDtypeStruct((N_EXPERTS, LANES), F32)],
        scratch_shapes=[pltpu.VMEM((A_WIDTH // LANES, TM, LANES), F32), pltpu.VMEM((N_EXPERTS, LANES), F32)],
        compiler_params=_cparams(("arbitrary", "arbitrary")),
        name="outproj_router",
    )(oa, om, x, wo, g, wrh, wrl, br, tri)


def _expert_kernel(blk_e_ref, n_used_ref, x_ref, wg_ref, wu_ref, wd_ref, y_ref, wg_s, wu_s, wd_s):
    i = pl.program_id(0)

    @pl.when(i < n_used_ref[0])
    def _():
        @pl.when((i == 0) | (blk_e_ref[i] != blk_e_ref[jnp.maximum(i - 1, 0)]))
        def _():
            wg_s[...] = wg_ref[0].astype(BF16)
            wu_s[...] = wu_ref[0].astype(BF16)
            wd_s[...] = wd_ref[0].astype(BF16)

        x = x_ref[...]
        gate = jnp.dot(x, wg_s[...], preferred_element_type=F32)
        up = jnp.dot(x, wu_s[...], preferred_element_type=F32)
        hdn = (gate * jax.nn.sigmoid(gate) * up).astype(BF16)
        y_ref[...] = jnp.dot(hdn, wd_s[...], preferred_element_type=F32).astype(y_ref.dtype)

    @pl.when(i >= n_used_ref[0])
    def _():
        y_ref[...] = jnp.zeros(y_ref.shape, y_ref.dtype)


def _experts(blk_e, n_used, xs, wg, wu, wd):
    n_blocks = xs.shape[0] // MOE_TM
    row_in = lambda i, be, nu: (jnp.minimum(i, nu[0] - 1), 0)
    wsel = lambda i, be, nu: (be[i], 0, 0)
    return pl.pallas_call(
        _expert_kernel,
        grid_spec=pltpu.PrefetchScalarGridSpec(
            num_scalar_prefetch=2,
            grid=(n_blocks,),
            in_specs=[pl.BlockSpec((MOE_TM, D_MODEL), row_in),
                      pl.BlockSpec((1, D_MODEL, EXPERT_FF), wsel),
                      pl.BlockSpec((1, D_MODEL, EXPERT_FF), wsel),
                      pl.BlockSpec((1, EXPERT_FF, D_MODEL), wsel)],
            out_specs=pl.BlockSpec((MOE_TM, D_MODEL), lambda i, be, nu: (i, 0)),
            scratch_shapes=[pltpu.VMEM((D_MODEL, EXPERT_FF), BF16), pltpu.VMEM((D_MODEL, EXPERT_FF), BF16),
                            pltpu.VMEM((EXPERT_FF, D_MODEL), BF16)]),
        out_shape=jax.ShapeDtypeStruct(xs.shape, BF16),
        compiler_params=_cparams(("arbitrary",)),
        name="expert_ffn",
    )(blk_e, n_used, xs, wg, wu, wd)


def _ple_kernel(n_parts, part_b, *refs):
    parts = [refs[4 * i:4 * i + 4] for i in range(n_parts)]
    p_ref, wp_ref, gp_ref, wg_ref, bg_ref, o_ref = refs[4 * n_parts:]

    def combine_and_gate(h_ref, y1_ref, y2_ref, route_ref):
        rec = jnp.concatenate([route_ref[0], jnp.zeros((LANES - ROUTE_ROWS, TM), F32)], axis=0)
        route = jnp.concatenate([rec[:, c * LANES:(c + 1) * LANES].T for c in range(TM // LANES)], axis=0)
        h = (h_ref[0] + route[:, R_W1:R_W1 + 1] * y1_ref[0].astype(F32)
             + route[:, R_W2:R_W2 + 1] * y2_ref[0].astype(F32))
        e = _rms(jnp.dot(p_ref[0].astype(BF16), wp_ref[...], preferred_element_type=F32), D_MODEL) * gp_ref[...]
        g = jax.nn.sigmoid(jnp.dot(h.astype(BF16), wg_ref[...], preferred_element_type=F32) + bg_ref[...])
        o_ref[0] = h + g * e

    for i, part in enumerate(parts):
        pl.when(pl.program_id(0) // part_b == i)(functools.partial(combine_and_gate, *part))


def _ple(parts, p, wp, gp, wg, bg):
    n_parts, part_b = len(parts), parts[0][0].shape[0]
    rows = lambda c: pl.BlockSpec((1, TM, c), lambda b, j: (b, j, 0))
    in_specs, args = [], []
    for i, (h, y_tok, route) in enumerate(parts):
        local = lambda b, i=i: jnp.clip(b - i * part_b, 0, part_b - 1)
        step = lambda b, j, local=local: local(b) * N_ROW_STEPS + j
        in_specs += [pl.BlockSpec((1, TM, D_MODEL), lambda b, j, local=local: (local(b), j, 0)),
                     pl.BlockSpec((1, TM, D_MODEL), lambda b, j, step=step: (0, step(b, j), 0)),
                     pl.BlockSpec((1, TM, D_MODEL), lambda b, j, step=step: (1, step(b, j), 0)),
                     pl.BlockSpec((1, ROUTE_ROWS, TM), lambda b, j, step=step: (step(b, j), 0, 0))]
        args += [h, y_tok, y_tok, route]
    return pl.pallas_call(
        functools.partial(_ple_kernel, n_parts, part_b),
        grid=(n_parts * part_b, N_ROW_STEPS),
        in_specs=in_specs + [rows(PLE_DIM), _full(wp), _full(gp), _full(wg), _full(bg)],
        out_specs=rows(D_MODEL),
        out_shape=jax.ShapeDtypeStruct((n_parts * part_b, SEQ, D_MODEL), F32),
        compiler_params=_cparams(("parallel", "parallel")),
        name="ple_gate",
    )(*args, p, wp, gp, wg, bg)


def _t5_bucket(dist):
    max_exact = REL_BUCKETS // 2
    n = jnp.maximum(dist, 0)
    nf = jnp.maximum(n, 1).astype(F32)
    large = max_exact + (jnp.log(nf / max_exact) / math.log(REL_MAX_DISTANCE / max_exact)
                         * (REL_BUCKETS - max_exact)).astype(jnp.int32)
    large = jnp.minimum(large, REL_BUCKETS - 1)
    return jnp.where(n < max_exact, n, large)


def _bias_table(rel_bias, local_index, dilation, with_prev):
    loc = np.asarray(local_index)
    delta = loc[:, None] - loc[None, :]
    if with_prev:
        delta = np.concatenate([delta + BAND_BLOCK, delta], axis=1)
    ok = (delta >= 0) & (delta <= BAND_BLOCK)
    bucket = _t5_bucket(jnp.asarray(delta * dilation, jnp.int32))
    picked = jnp.where(bucket[None, :, :, None] == jnp.arange(REL_BUCKETS, dtype=jnp.int32),
                       rel_bias.astype(F32).T[:, None, None, :], 0.0)
    return jnp.where(jnp.asarray(ok)[None], jnp.sum(picked, axis=-1) * LOG2E, NEG)


def _block_diag_ones(sizes, total):
    g = np.zeros((total, total), np.float32)
    o = 0
    for s, on in sizes:
        if on:
            g[o:o + s, o:o + s] = 1.0
        o += s
    return jnp.asarray(g, BF16)


def _dispatch_plan(route, counts, n_tokens):
    field = lambda i: route[:, i, :].reshape(n_tokens).astype(jnp.int32)
    e = jnp.stack([field(R_E1), field(R_E2)], axis=-1)
    rank = jnp.stack([field(R_RANK1), field(R_RANK2)], axis=-1)
    pcounts = (counts + MOE_TM - 1) // MOE_TM * MOE_TM
    pend = jnp.cumsum(pcounts)
    pstart = pend - pcounts
    ids = jnp.arange(N_EXPERTS, dtype=jnp.int32)
    pos = rank + jnp.sum(jnp.where(e[..., None] == ids, pstart, 0), axis=-1)
    n_assign = n_tokens * TOP_K
    shift = int(math.ceil(math.log2(n_assign)))
    keys = (e.reshape(-1) << shift) | jnp.arange(n_assign, dtype=jnp.int32)
    tok_sorted = (jnp.sort(keys) & ((1 << shift) - 1)) // TOP_K
    n_rows = n_assign + N_EXPERTS * MOE_TM
    rows = jnp.arange(n_rows, dtype=jnp.int32)
    row_e = jnp.sum(rows[:, None] >= pend[None, :], axis=-1)
    row_e = jnp.minimum(row_e, N_EXPERTS - 1)
    pick = lambda tbl: jnp.sum(jnp.where(row_e[:, None] == ids, tbl, 0), axis=-1)
    within = rows - pick(pstart)
    src = jnp.clip(pick(jnp.cumsum(counts) - counts) + within, 0, n_assign - 1)
    row_tok = jnp.where(within < pick(counts), tok_sorted[src], rows % n_tokens)
    n_blocks = n_rows // MOE_TM
    n_used = (pend[-1] // MOE_TM).astype(jnp.int32)
    blk_start = jnp.minimum(jnp.arange(n_blocks, dtype=jnp.int32), n_used - 1) * MOE_TM
    blk_e = jnp.minimum(jnp.sum(blk_start[:, None] >= pend[None, :], axis=-1), N_EXPERTS - 1).astype(jnp.int32)
    return row_tok, pos, blk_e, n_used.reshape(1)


def kernel(x, p, rel_bias, norm_mix_gain, w_in, qn_a_gain, kn_a_gain, q_a_gain, w_q_up, kv_a_gain, w_kv_up, qn_nope_gain, qn_rope_gain, kn_nope_gain, kn_rope_gain, w_out, norm_ffn_gain, w_router_group, b_router_group, w_router_expert, b_router_expert, w_exp_gate, w_exp_up, w_exp_down, w_ple_proj, ple_norm_gain, w_ple_gate, b_ple_gate):
    B, S, D = x.shape
    assert (S, D) == (SEQ, D_MODEL) and p.shape[0] == 1
    row = lambda a: a.reshape(1, -1).astype(F32)
    zeros = lambda *s: jnp.zeros(s, F32)

    w_in_p = jnp.concatenate([w_in[0, :, :C_KR0], zeros(D, 64), w_in[0, :, C_KR0:], zeros(D, 32)],
                             axis=1).astype(BF16)
    gsum_a = _block_diag_ones([(HEAD_DIM_A, True)] * N_HEADS_A, A_WIDTH)
    gq_a = row(jnp.tile(qn_a_gain[0], N_HEADS_A)) * (HEAD_DIM_A ** -0.5 * LOG2E)
    gk_a = row(jnp.tile(kn_a_gain[0], N_HEADS_A))

    wq_p = jnp.pad(w_q_up[0].reshape(Q_LORA, N_HEADS_M, NOPE_DIM + ROPE_DIM),
                   ((0, 0), (0, 0), (0, HEAD_SLOT - NOPE_DIM - ROPE_DIM))).reshape(Q_LORA, -1).astype(BF16)
    wkv = w_kv_up[0].reshape(KV_LORA, N_HEADS_M, NOPE_DIM + V_DIM)
    wk_p = jnp.pad(wkv[..., :NOPE_DIM], ((0, 0), (0, 0), (0, HEAD_SLOT - NOPE_DIM))).reshape(KV_LORA, -1)
    wkv_p = jnp.concatenate([wk_p, wkv[..., NOPE_DIM:].reshape(KV_LORA, -1)], axis=1).astype(BF16)
    gs_q = _block_diag_ones([(NOPE_DIM, True), (ROPE_DIM, True), (32, False)], HEAD_SLOT)
    gs_k = _block_diag_ones([(NOPE_DIM, True), (64, False)], HEAD_SLOT)
    inv_cnt_q = jnp.asarray(np.concatenate([np.full(64, 1 / NOPE_DIM), np.full(32, 1 / ROPE_DIM),
                                            np.ones(32)]).astype(np.float32)).reshape(1, HEAD_SLOT)
    mla_scale = (NOPE_DIM + ROPE_DIM) ** -0.5 * LOG2E
    gq_m = row(jnp.concatenate([qn_nope_gain[0], qn_rope_gain[0], zeros(32)])) * mla_scale
    gk_m = row(jnp.concatenate([kn_nope_gain[0], zeros(64)]))
    gkr_m = row(jnp.concatenate([zeros(64), kn_rope_gain[0], zeros(32)]))

    half = ROPE_DIM // 2
    inv = 1.0 / (ROPE_THETA ** (jnp.arange(half, dtype=F32) * 2.0 / ROPE_DIM))
    ang = jnp.arange(S, dtype=jnp.int32).astype(F32)[:, None] * inv[None, :]
    cosv, sinv = jnp.cos(ang), jnp.sin(ang)
    cos_t = jnp.concatenate([jnp.ones((S, 64), F32), cosv, cosv, jnp.ones((S, 32), F32)], -1)
    sin_a = jnp.concatenate([zeros(S, 80), sinv, zeros(S, 32)], -1)
    sin_b = jnp.concatenate([zeros(S, 64), -sinv, zeros(S, 48)], -1)

    loc1 = [16 * a + r for r in range(16) for a in range(8)]
    loc2 = [4 * a + c for c in range(4) for a in range(32)]
    loc3 = list(range(BAND_BLOCK))
    t1 = _bias_table(rel_bias, loc1, 1, True)
    t2 = _bias_table(rel_bias, loc2, 4, True)
    t3 = jnp.concatenate([jnp.full((N_HEADS_A, BAND_BLOCK, BAND_BLOCK), NEG, F32),
                          _bias_table(rel_bias, loc3, 16, False)], axis=-1)

    n_pad_g, n_pad_e = ROUTE_GROUP_ROWS - N_GROUPS, LANES - ROUTE_EXPERT_ROW0 - N_EXPERTS
    w_r = jnp.concatenate([w_router_group[0].T, zeros(n_pad_g, D), w_router_expert[0].T, zeros(n_pad_e, D)], axis=0)
    w_r_hi = w_r.astype(BF16)
    w_r_lo = (w_r - w_r_hi.astype(F32)).astype(BF16)
    b_r = jnp.concatenate([b_router_group[0], jnp.full((n_pad_g,), NEG, F32), b_router_expert[0],
                           zeros(n_pad_e)]).reshape(LANES, 1)
    tri = jnp.asarray(np.triu(np.ones((TM, TM), np.float32), 1), BF16)

    w_out_b = w_out[0].astype(BF16)
    part_b = B // BATCH_PARTS
    n_tokens = part_b * S
    parts = []
    for b0 in range(0, B, part_b):
        qa, ka, va, cq, ckv, kr = _inproj(x, b0, part_b, row(norm_mix_gain[0]), w_in_p, gsum_a, gq_a, gk_a,
                                          row(q_a_gain[0]), row(kv_a_gain[0]))
        o_a = _dilated(qa, ka, va, t1, t2, t3)
        q_m, k_m, v_m = _mla_prep(cq, ckv, kr, wq_p, wkv_p, gs_q, gs_k, inv_cnt_q, gq_m, gk_m, gkr_m,
                                  cos_t, sin_a, sin_b)
        o_m = _mla_attn(q_m, k_m, v_m)
        h1, xn2, route, cnt = _outproj(o_a, o_m, x, b0, w_out_b, row(norm_ffn_gain[0]),
                                       w_r_hi, w_r_lo, b_r, tri)
        row_tok, pos, blk_e, n_used = _dispatch_plan(route, cnt[:, 0].astype(jnp.int32), n_tokens)
        xs = xn2.reshape(n_tokens, D)[row_tok]
        y = _experts(blk_e, n_used, xs, w_exp_gate[0], w_exp_up[0], w_exp_down[0])
        y_tok = y[pos.T.reshape(-1)].reshape(TOP_K, n_tokens, D)
        parts.append((h1, y_tok, route))
    return _ple(parts, p[0], w_ple_proj[0].astype(BF16), row(ple_norm_gain[0]),
                w_ple_gate[0].astype(BF16), row(b_ple_gate[0]))
```

```python
import functools
import math

import jax
import jax.numpy as jnp
import numpy as np
from jax import lax
from jax.experimental import pallas as pl
from jax.experimental.pallas import tpu as pltpu

F32 = jnp.float32
BF16 = jnp.bfloat16

D_MODEL = 1024
SEQ = 2048
PLE_DIM = 256
EPS = 1e-6
NEG = -1e30
HEAD_DIM_A = 64
A_WIDTH = 512
N_HEADS_A = 8
BAND_BLOCK = 128
REL_BUCKETS = 32
REL_MAX_DISTANCE = 2048
M_WIDTH = 512
V_DIM = 64
N_HEADS_M = 8
Q_LORA = 384
KV_LORA = 256
NOPE_DIM = 64
ROPE_DIM = 32
ROPE_THETA = 10000.0
N_GROUPS = 4
EXPERTS_PER_GROUP = 8
N_EXPERTS = 32
TOP_K = 2
EXPERT_FF = 512

LANES = 128
N_RES = 16
M_SUB = SEQ // N_RES
TM = 512
N_ROW_STEPS = SEQ // TM
M_STEP = TM // N_RES
HEAD_SLOT = LANES
MLA_TQ = 256
MOE_TM = 512
ATTN_SKEW = 2
BATCH_PARTS = 1
SUB_TILES = 2
VMEM_LIMIT = 48 * 1024 * 1024

IN_COLS_PAD = 3 * A_WIDTH + Q_LORA + KV_LORA + LANES
C_Q0, C_K0, C_V0, C_CQ0, C_CKV0, C_KR0 = 0, 512, 1024, 1536, 1920, 2176

NT_DIMS = (((1,), (1,)), ((), ()))
LOG2E = math.log2(math.e)
LN2 = math.log(2.0)


def _cparams(sem):
    return pltpu.CompilerParams(dimension_semantics=sem, vmem_limit_bytes=VMEM_LIMIT)


def _full(a):
    return pl.BlockSpec(a.shape, lambda *_: (0,) * a.ndim)


def _rms(x, n):
    return x * lax.rsqrt(jnp.sum(x * x, axis=-1, keepdims=True) * (1.0 / n) + EPS)


def _residue_rows(r):
    return pl.ds(r, M_STEP, stride=N_RES)


def _inproj_kernel(x_ref, g_ref, w_ref, gsum_ref, gq_ref, gk_ref, gcq_ref, gckv_ref,
                   qa_ref, ka_ref, va_ref, cq_ref, ckv_ref, kr_ref, perm_scr, xn_scr):
    n_chunks = A_WIDTH // LANES
    for s in range(SUB_TILES):
        xn_scr[s] = (_rms(x_ref[0, s * TM:(s + 1) * TM], D_MODEL) * g_ref[...]).astype(BF16)

    def head_norm(t, gain_ref):
        ss = jnp.dot((t * t).astype(BF16), gsum_ref[...], preferred_element_type=F32)
        return t * lax.rsqrt(ss * (1.0 / HEAD_DIM_A) + EPS) * gain_ref[...]

    for s in range(SUB_TILES):
        rows = slice(s * TM, (s + 1) * TM)
        proj = lambda c0, c1, s=s: jnp.dot(xn_scr[s], w_ref[:, c0:c1], preferred_element_type=F32)

        def put_residue(ref, val, t, s=s):
            for c in range(n_chunks):
                perm_scr[s, t, c] = val[:, c * LANES:(c + 1) * LANES]
            for r in range(N_RES):
                for c in range(n_chunks):
                    ref[0, r, s * M_STEP:(s + 1) * M_STEP, c * LANES:(c + 1) * LANES] = (
                        perm_scr[s, t, c, _residue_rows(r), :].astype(ref.dtype))

        put_residue(qa_ref, head_norm(proj(C_Q0, C_K0), gq_ref), 0)
        put_residue(ka_ref, head_norm(proj(C_K0, C_V0), gk_ref), 1)
        put_residue(va_ref, proj(C_V0, C_CQ0), 2)
        cq_ref[0, rows] = (_rms(proj(C_CQ0, C_CKV0), Q_LORA) * gcq_ref[...]).astype(BF16)
        ckv_ref[0, rows] = (_rms(proj(C_CKV0, C_KR0), KV_LORA) * gckv_ref[...]).astype(BF16)
        kr_ref[0, rows] = proj(C_KR0, IN_COLS_PAD)


def _inproj(x, b0, B, g, w_in_p, gsum, gq, gk, gcq, gckv):
    step_rows = SUB_TILES * TM
    res = lambda: (jax.ShapeDtypeStruct((B, N_RES, M_SUB, A_WIDTH), BF16),
                   pl.BlockSpec((1, N_RES, SUB_TILES * M_STEP, A_WIDTH), lambda b, j: (b, 0, j, 0)))
    nat = lambda c, dt: (jax.ShapeDtypeStruct((B, SEQ, c), dt),
                         pl.BlockSpec((1, step_rows, c), lambda b, j: (b, j, 0)))
    outs = [res(), res(), res(), nat(Q_LORA, BF16), nat(KV_LORA, BF16), nat(LANES, F32)]
    return pl.pallas_call(
        _inproj_kernel,
        grid=(B, SEQ // step_rows),
        in_specs=[pl.BlockSpec((1, step_rows, D_MODEL), lambda b, j: (b + b0, j, 0)),
                  _full(g), _full(w_in_p), _full(gsum), _full(gq), _full(gk), _full(gcq), _full(gckv)],
        out_specs=[o[1] for o in outs],
        out_shape=[o[0] for o in outs],
        scratch_shapes=[pltpu.VMEM((SUB_TILES, 3, A_WIDTH // LANES, TM, LANES), F32),
                        pltpu.VMEM((SUB_TILES, TM, D_MODEL), BF16)],
        compiler_params=_cparams(("parallel", "parallel")),
        name="inproj",
    )(x, g, w_in_p, gsum, gq, gk, gcq, gckv)


def _mla_prep_kernel(cq_ref, ckv_ref, kr_ref, wq_ref, wkv_ref, gsq_ref, gsk_ref, icq_ref,
                     gq_ref, gk_ref, gkr_ref, cos_ref, sa_ref, sb_ref, q_ref, k_ref, v_ref):
    sub = TM // SUB_TILES
    lane = lax.broadcasted_iota(jnp.int32, (sub, LANES), 1)

    for t in range(SUB_TILES):
        rows = slice(t * sub, (t + 1) * sub)
        cos, sa, sb = cos_ref[rows, :], sa_ref[rows, :], sb_ref[rows, :]

        def rope(x, cos=cos, sa=sa, sb=sb):
            return x * cos + pltpu.roll(x, 16, 1) * sa + pltpu.roll(x, LANES - 16, 1) * sb

        q = jnp.dot(cq_ref[0, rows, :], wq_ref[...], preferred_element_type=F32)
        kv = jnp.dot(ckv_ref[0, rows, :], wkv_ref[...], preferred_element_type=F32)
        k_rope = rope(_rms(kr_ref[0, rows, :], ROPE_DIM) * gkr_ref[...])
        for h in range(N_HEADS_M):
            sl = slice(h * HEAD_SLOT, (h + 1) * HEAD_SLOT)
            qh = q[:, sl]
            ss = jnp.dot((qh * qh).astype(BF16), gsq_ref[...], preferred_element_type=F32)
            q_ref[0, rows, sl] = rope(qh * lax.rsqrt(ss * icq_ref[...] + EPS) * gq_ref[...]).astype(BF16)
            kh = kv[:, sl]
            ssk = jnp.dot((kh * kh).astype(BF16), gsk_ref[...], preferred_element_type=F32)
            kn = kh * lax.rsqrt(ssk * (1.0 / NOPE_DIM) + EPS) * gk_ref[...] + k_rope
            k_ref[0, rows, sl] = kn.astype(BF16)
        for hp in range(N_HEADS_M // 2):
            v_pair = kv[:, N_HEADS_M * HEAD_SLOT + hp * LANES:N_HEADS_M * HEAD_SLOT + (hp + 1) * LANES]
            v_ref[0, rows, (2 * hp) * HEAD_SLOT:(2 * hp + 1) * HEAD_SLOT] = jnp.where(lane < V_DIM, v_pair, 1.0).astype(BF16)
            v_ref[0, rows, (2 * hp + 1) * HEAD_SLOT:(2 * hp + 2) * HEAD_SLOT] = jnp.where(lane < V_DIM, 1.0, v_pair).astype(BF16)


def _mla_prep(cq, ckv, kr, wq_p, wkv_p, gsq, gsk, icq, gq, gk, gkr, cos, sa, sb):
    B = cq.shape[0]
    rows = lambda c: pl.BlockSpec((1, TM, c), lambda b, j: (b, j, 0))
    tab = pl.BlockSpec((TM, LANES), lambda b, j: (j, 0))
    wide = N_HEADS_M * HEAD_SLOT
    return pl.pallas_call(
        _mla_prep_kernel,
        grid=(B, N_ROW_STEPS),
        in_specs=[rows(Q_LORA), rows(KV_LORA), rows(LANES), _full(wq_p), _full(wkv_p), _full(gsq),
                  _full(gsk), _full(icq), _full(gq), _full(gk), _full(gkr), tab, tab, tab],
        out_specs=[rows(wide)] * 3,
        out_shape=[jax.ShapeDtypeStruct((B, SEQ, wide), BF16)] * 3,
        compiler_params=_cparams(("parallel", "parallel")),
        name="mla_prep",
    )(cq, ckv, kr, wq_p, wkv_p, gsq, gsk, icq, gq, gk, gkr, cos, sa, sb)


def _mla_attn_kernel(q_ref, k_ref, v_ref, o_ref):
    n_q = SEQ // MLA_TQ
    row = lax.broadcasted_iota(jnp.int32, (MLA_TQ, MLA_TQ), 0)
    col = lax.broadcasted_iota(jnp.int32, (MLA_TQ, MLA_TQ), 1)
    lane = lax.broadcasted_iota(jnp.int32, (MLA_TQ, LANES), 1)
    heads = [slice(hh * HEAD_SLOT, (hh + 1) * HEAD_SLOT) for hh in range(2)]

    def probs(i, hh):
        n_keys = (i + 1) * MLA_TQ
        s = lax.dot_general(q_ref[0, i * MLA_TQ:n_keys, heads[hh]], k_ref[0, 0:n_keys, heads[hh]],
                            NT_DIMS, preferred_element_type=F32)
        diag = jnp.where(col <= row, s[:, n_keys - MLA_TQ:], NEG)
        s = diag if i == 0 else jnp.concatenate([s[:, :n_keys - MLA_TQ], diag], axis=1)
        return jnp.exp2(s - jnp.max(s, axis=-1, keepdims=True)).astype(BF16)

    def values(i, hh, p):
        return jnp.dot(p, v_ref[0, 0:(i + 1) * MLA_TQ, heads[hh]], preferred_element_type=F32)

    units = [(i, hh) for i in range(n_q) for hh in range(2)]
    acc, pending = {}, None
    for u in units:
        p = probs(*u)
        if pending is not None:
            acc[pending[0]] = values(*pending[0], pending[1])
        pending = (u, p)
    acc[pending[0]] = values(*pending[0], pending[1])
    for i in range(n_q):
        num = jnp.where(lane < V_DIM, acc[(i, 0)], acc[(i, 1)])
        den = pltpu.roll(jnp.where(lane < V_DIM, acc[(i, 1)], acc[(i, 0)]), V_DIM, 1)
        o_ref[0, i * MLA_TQ:(i + 1) * MLA_TQ, :] = (num / den).astype(BF16)


def _mla_attn(q, k, v):
    B = q.shape[0]
    pair = lambda c: pl.BlockSpec((1, SEQ, c), lambda b, h: (b, 0, h))
    return pl.pallas_call(
        _mla_attn_kernel,
        grid=(B, N_HEADS_M // 2),
        in_specs=[pair(2 * HEAD_SLOT)] * 3,
        out_specs=pair(2 * V_DIM),
        out_shape=jax.ShapeDtypeStruct((B, SEQ, M_WIDTH), BF16),
        compiler_params=_cparams(("parallel", "parallel")),
        name="mla_attn",
    )(q, k, v)


N_BLK = SEQ // BAND_BLOCK


def _dilated_kernel(q_ref, k_ref, v_ref, t1_ref, t2_ref, t3_ref, ones_ref, o_ref,
                    q1, k1, v1, q2, k2, v2, ob1, ls1, ob2, ls2, ob3, ls3):
    bb = BAND_BLOCK
    lane = lax.broadcasted_iota(jnp.int32, (bb, LANES), 1)
    lane_row = lax.broadcasted_iota(jnp.int32, (1, LANES), 1)
    own = [(lane_row < HEAD_DIM_A).astype(BF16), (lane_row >= HEAD_DIM_A).astype(BF16)]
    rows = lambda lo, hi: slice(lo * bb, hi * bb)

    for src, d1, d2 in ((q_ref, q1, q2), (k_ref, k1, k2), (v_ref, v1, v2)):
        for n2 in range(N_BLK // 2):
            pieces = [src[0, r, 16 * n2:16 * n2 + 16, :].astype(F32) for r in range(N_RES)]
            d1[rows(2 * n2, 2 * n2 + 1), :] = jnp.concatenate([p[0:8] for p in pieces], axis=0).astype(BF16)
            d1[rows(2 * n2 + 1, 2 * n2 + 2), :] = jnp.concatenate([p[8:16] for p in pieces], axis=0).astype(BF16)
        for r4 in range(4):
            for n in range(4):
                d2[rows(r4 * 4 + n, r4 * 4 + n + 1), :] = jnp.concatenate(
                    [src[0, r4 + 4 * c, 32 * n:32 * n + 32, :] for c in range(4)], axis=0)

    blocks = []
    for idx in range(N_BLK):
        lo = idx - 1 if idx > 0 else idx
        blocks.append((q1, k1, v1, rows(lo, idx + 1), t1_ref, ob1, ls1, idx))
    for idx in range(N_BLK):
        lo = idx - 1 if idx % 4 else idx
        blocks.append((q2, k2, v2, rows(lo, idx + 1), t2_ref, ob2, ls2, idx))
    for r in range(N_RES):
        blocks.append((None, None, None, r, t3_ref, ob3, ls3, r))

    def scores_and_probs(blk, hh):
        qd, kd, _, kv_rows, t_ref, _, _, idx = blk
        q = qd[rows(idx, idx + 1), :] if qd is not None else q_ref[0, idx]
        keys = kd[kv_rows, :] if kd is not None else k_ref[0, kv_rows]
        n_keys = keys.shape[0]
        s = lax.dot_general(q * own[hh], keys, NT_DIMS, preferred_element_type=F32)
        s = s + t_ref[hh, :, 2 * bb - n_keys:2 * bb]
        m = jnp.max(s, axis=-1, keepdims=True)
        return m, jnp.exp2(s - m).astype(BF16)

    def finish(blk, m, e):
        _, _, vd, kv_rows, _, o_dst, l_dst, idx = blk
        vals = vd[kv_rows, :] if vd is not None else v_ref[0, kv_rows]
        n_keys = vals.shape[0]
        num = (jnp.dot(e[0], vals * own[0], preferred_element_type=F32)
               + jnp.dot(e[1], vals * own[1], preferred_element_type=F32))
        den = (jnp.dot(e[0], ones_ref[0, 0:n_keys, :], preferred_element_type=F32)
               + jnp.dot(e[1], ones_ref[1, 0:n_keys, :], preferred_element_type=F32))
        o_dst[idx] = num / den
        l_dst[idx] = jnp.where(lane < HEAD_DIM_A, m[0], m[1]) * LN2 + jnp.log(den)

    stage = {}
    for t in range(len(blocks) + ATTN_SKEW):
        if t < len(blocks):
            stage[t] = [scores_and_probs(blocks[t], hh) for hh in range(2)]
        d = t - ATTN_SKEW
        if d >= 0:
            finish(blocks[d], [stage[d][hh][0] for hh in range(2)], [stage[d][hh][1] for hh in range(2)])
            del stage[d]

    for r in range(N_RES):
        r4, c = r % 4, r // 4
        gather1 = lambda ref: jnp.concatenate([ref[n, 8 * r:8 * r + 8, :] for n in range(N_BLK)], axis=0)
        gather2 = lambda ref: jnp.concatenate(
            [ref[r4 * 4 + n, 32 * c:32 * c + 32, :] for n in range(4)], axis=0)
        o_b = [gather1(ob1), gather2(ob2), ob3[r]]
        l_b = [gather1(ls1), gather2(ls2), ls3[r]]
        top = jnp.maximum(jnp.maximum(l_b[0], l_b[1]), l_b[2])
        w_b = [jnp.exp(l - top) for l in l_b]
        num = w_b[0] * o_b[0] + w_b[1] * o_b[1] + w_b[2] * o_b[2]
        o_ref[0, r] = (num / (w_b[0] + w_b[1] + w_b[2])).astype(BF16)


def _dilated(qa, ka, va, t1, t2, t3):
    B = qa.shape[0]
    head_lanes = np.arange(LANES)[None, None, :] // HEAD_DIM_A == np.arange(2)[:, None, None]
    ones = jnp.asarray(np.broadcast_to(head_lanes, (2, 2 * BAND_BLOCK, LANES)).astype(np.float32), BF16)
    blk = pl.BlockSpec((1, N_RES, M_SUB, LANES), lambda b, h: (b, 0, 0, h))
    tab = lambda t: pl.BlockSpec((2,) + t.shape[1:], lambda b, h: (h, 0, 0))
    blocked_bf16 = pltpu.VMEM((N_BLK * BAND_BLOCK, LANES), BF16)
    blocked_f32 = pltpu.VMEM((N_BLK, BAND_BLOCK, LANES), F32)
    return pl.pallas_call(
        _dilated_kernel,
        grid=(B, N_HEADS_A // 2),
        in_specs=[blk, blk, blk, tab(t1), tab(t2), tab(t3), _full(ones)],
        out_specs=blk,
        out_shape=jax.ShapeDtypeStruct((B, N_RES, M_SUB, A_WIDTH), BF16),
        scratch_shapes=[blocked_bf16] * 6 + [blocked_f32] * 6,
        compiler_params=_cparams(("parallel", "parallel")),
        name="dilated_attn",
    )(qa, ka, va, t1, t2, t3, ones)


R_W1, R_W2, R_E1, R_E2, R_RANK1, R_RANK2 = range(6)
ROUTE_ROWS = 8
ROUTE_GROUP_ROWS = 8
ROUTE_EXPERT_ROW0 = 8


def _outproj_kernel(oa_ref, om_ref, x_ref, wo_ref, g_ref, wrh_ref, wrl_ref, br_ref, tri_ref,
                    h_ref, xn_ref, route_ref, cnt_ref, perm_scr, carry_scr):
    @pl.when((pl.program_id(0) == 0) & (pl.program_id(1) == 0))
    def _():
        carry_scr[...] = jnp.zeros(carry_scr.shape, F32)

    n_chunks = A_WIDTH // LANES
    for r in range(N_RES):
        for c in range(n_chunks):
            perm_scr[c, _residue_rows(r), :] = oa_ref[0, r, :, c * LANES:(c + 1) * LANES].astype(F32)
    oa = jnp.concatenate([perm_scr[c] for c in range(n_chunks)], axis=1).astype(BF16)
    h = (x_ref[0] + jnp.dot(oa, wo_ref[0:A_WIDTH, :], preferred_element_type=F32)
         + jnp.dot(om_ref[0], wo_ref[A_WIDTH:, :], preferred_element_type=F32))
    h_ref[0] = h
    xn = _rms(h, D_MODEL) * g_ref[...]
    xn_ref[0] = xn.astype(BF16)
    hi = xn.astype(BF16)
    lo = (xn - hi.astype(F32)).astype(BF16)
    lg = (lax.dot_general(wrh_ref[...], hi, NT_DIMS, preferred_element_type=F32)
          + lax.dot_general(wrh_ref[...], lo, NT_DIMS, preferred_element_type=F32)
          + lax.dot_general(wrl_ref[...], hi, NT_DIMS, preferred_element_type=F32)) + br_ref[...]

    sub = lax.broadcasted_iota(jnp.int32, (EXPERTS_PER_GROUP, TM), 0).astype(F32)
    cmax = lambda t: jnp.max(t, axis=0, keepdims=True)
    cmin = lambda t: jnp.min(t, axis=0, keepdims=True)
    csum = lambda t: jnp.sum(t, axis=0, keepdims=True)
    none = float(EXPERTS_PER_GROUP)

    gl = lg[0:ROUTE_GROUP_ROWS]
    ge = jnp.exp(gl - cmax(gl))
    gsum = csum(ge)
    g_gate = 1.0 / gsum
    g_idx = cmin(jnp.where(ge / gsum == g_gate, sub, none))
    el = lg[ROUTE_EXPERT_ROW0:ROUTE_EXPERT_ROW0 + EXPERTS_PER_GROUP]
    for g in range(1, N_GROUPS):
        r0 = ROUTE_EXPERT_ROW0 + g * EXPERTS_PER_GROUP
        el = jnp.where(g_idx == float(g), lg[r0:r0 + EXPERTS_PER_GROUP], el)
    ee = jnp.exp(el - cmax(el))
    esum = csum(ee)
    eprob = ee / esum
    p1 = 1.0 / esum
    i1 = cmin(jnp.where(eprob == p1, sub, none))
    rest = jnp.where(sub == i1, -1.0, eprob)
    p2 = cmax(rest)
    i2 = cmin(jnp.where(rest == p2, sub, none))
    den = p1 + p2
    e1 = g_idx * EXPERTS_PER_GROUP + i1
    e2 = g_idx * EXPERTS_PER_GROUP + i2
    erow = lax.broadcasted_iota(jnp.int32, (N_EXPERTS, TM), 0).astype(F32)
    onehot = ((erow == e1) | (erow == e2)).astype(F32)
    before = jnp.dot(onehot.astype(BF16), tri_ref[...], preferred_element_type=F32) + carry_scr[:, 0:1]
    rank1 = csum(jnp.where(erow == e1, before, 0.0))
    rank2 = csum(jnp.where(erow == e2, before, 0.0))
    carry_scr[...] = carry_scr[...] + jnp.sum(onehot, axis=1, keepdims=True)
    cnt_ref[...] = carry_scr[...]

    record = jnp.zeros((ROUTE_ROWS, TM), F32)
    for i, val in ((R_W1, g_gate * (p1 / den)), (R_W2, g_gate * (p2 / den)), (R_E1, e1), (R_E2, e2),
                   (R_RANK1, rank1), (R_RANK2, rank2)):
        record = jnp.where(sub == float(i), val, record)
    route_ref[0] = record


def _outproj(oa, om, x, b0, wo, g, wrh, wrl, br, tri):
    B = oa.shape[0]
    rows = lambda c: pl.BlockSpec((1, TM, c), lambda b, j: (b, j, 0))
    shp = lambda c, dt: jax.ShapeDtypeStruct((B, SEQ, c), dt)
    return pl.pallas_call(
        _outproj_kernel,
        grid=(B, N_ROW_STEPS),
        in_specs=[pl.BlockSpec((1, N_RES, M_STEP, A_WIDTH), lambda b, j: (b, 0, j, 0)),
                  rows(M_WIDTH), pl.BlockSpec((1, TM, D_MODEL), lambda b, j: (b + b0, j, 0)),
                  _full(wo), _full(g), _full(wrh), _full(wrl), _full(br), _full(tri)],
        out_specs=[rows(D_MODEL), rows(D_MODEL),
                   pl.BlockSpec((1, ROUTE_ROWS, TM), lambda b, j: (b * N_ROW_STEPS + j, 0, 0)),
                   pl.BlockSpec((N_EXPERTS, LANES), lambda b, j: (0, 0))],
        out_shape=[shp(D_MODEL, F32), shp(D_MODEL, BF16),
                   jax.ShapeDtypeStruct((B * N_ROW_STEPS, ROUTE_ROWS, TM), F32),
                   jax.ShapeDtypeStruct((N_EXPERTS, LANES), F32)],
        scratch_shapes=[pltpu.VMEM((A_WIDTH // LANES, TM, LANES), F32), pltpu.VMEM((N_EXPERTS, LANES), F32)],
        compiler_params=_cparams(("arbitrary", "arbitrary")),
        name="outproj_router",
    )(oa, om, x, wo, g, wrh, wrl, br, tri)


def _expert_kernel(blk_e_ref, n_used_ref, x_ref, wg_ref, wu_ref, wd_ref, y_ref, wg_s, wu_s, wd_s):
    i = pl.program_id(0)

    @pl.when(i < n_used_ref[0])
    def _():
        @pl.when((i == 0) | (blk_e_ref[i] != blk_e_ref[jnp.maximum(i - 1, 0)]))
        def _():
            wg_s[...] = wg_ref[0].astype(BF16)
            wu_s[...] = wu_ref[0].astype(BF16)
            wd_s[...] = wd_ref[0].astype(BF16)

        half = MOE_TM // SUB_TILES
        gate_up = []
        for s in range(SUB_TILES):
            x = x_ref[s * half:(s + 1) * half, :]
            gate_up.append((jnp.dot(x, wg_s[...], preferred_element_type=F32),
                            jnp.dot(x, wu_s[...], preferred_element_type=F32)))
        for s, (gate, up) in enumerate(gate_up):
            hdn = (gate * jax.nn.sigmoid(gate) * up).astype(BF16)
            y_ref[s * half:(s + 1) * half, :] = jnp.dot(hdn, wd_s[...], preferred_element_type=F32).astype(y_ref.dtype)

    @pl.when(i >= n_used_ref[0])
    def _():
        y_ref[...] = jnp.zeros(y_ref.shape, y_ref.dtype)


def _experts(blk_e, n_used, xs, wg, wu, wd):
    n_blocks = xs.shape[0] // MOE_TM
    row_in = lambda i, be, nu: (jnp.minimum(i, nu[0] - 1), 0)
    wsel = lambda i, be, nu: (be[i], 0, 0)
    return pl.pallas_call(
        _expert_kernel,
        grid_spec=pltpu.PrefetchScalarGridSpec(
            num_scalar_prefetch=2,
            grid=(n_blocks,),
            in_specs=[pl.BlockSpec((MOE_TM, D_MODEL), row_in),
                      pl.BlockSpec((1, D_MODEL, EXPERT_FF), wsel),
                      pl.BlockSpec((1, D_MODEL, EXPERT_FF), wsel),
                      pl.BlockSpec((1, EXPERT_FF, D_MODEL), wsel)],
            out_specs=pl.BlockSpec((MOE_TM, D_MODEL), lambda i, be, nu: (i, 0)),
            scratch_shapes=[pltpu.VMEM((D_MODEL, EXPERT_FF), BF16), pltpu.VMEM((D_MODEL, EXPERT_FF), BF16),
                            pltpu.VMEM((EXPERT_FF, D_MODEL), BF16)]),
        out_shape=jax.ShapeDtypeStruct(xs.shape, BF16),
        compiler_params=_cparams(("arbitrary",)),
        name="expert_ffn",
    )(blk_e, n_used, xs, wg, wu, wd)


def _ple_kernel(n_parts, part_b, *refs):
    parts = [refs[4 * i:4 * i + 4] for i in range(n_parts)]
    p_ref, wp_ref, gp_ref, wg_ref, bg_ref, o_ref = refs[4 * n_parts:]

    def combine_and_gate(h_ref, y1_ref, y2_ref, route_ref):
        rec = jnp.concatenate([route_ref[0], jnp.zeros((LANES - ROUTE_ROWS, TM), F32)], axis=0)
        route = jnp.concatenate([rec[:, c * LANES:(c + 1) * LANES].T for c in range(TM // LANES)], axis=0)
        h = (h_ref[0] + route[:, R_W1:R_W1 + 1] * y1_ref[0].astype(F32)
             + route[:, R_W2:R_W2 + 1] * y2_ref[0].astype(F32))
        e = _rms(jnp.dot(p_ref[0].astype(BF16), wp_ref[...], preferred_element_type=F32), D_MODEL) * gp_ref[...]
        g = jax.nn.sigmoid(jnp.dot(h.astype(BF16), wg_ref[...], preferred_element_type=F32) + bg_ref[...])
        o_ref[0] = h + g * e

    for i, part in enumerate(parts):
        pl.when(pl.program_id(0) // part_b == i)(functools.partial(combine_and_gate, *part))


def _ple(parts, p, wp, gp, wg, bg):
    n_parts, part_b = len(parts), parts[0][0].shape[0]
    rows = lambda c: pl.BlockSpec((1, TM, c), lambda b, j: (b, j, 0))
    in_specs, args = [], []
    for i, (h, y_tok, route) in enumerate(parts):
        local = lambda b, i=i: jnp.clip(b - i * part_b, 0, part_b - 1)
        step = lambda b, j, local=local: local(b) * N_ROW_STEPS + j
        in_specs += [pl.BlockSpec((1, TM, D_MODEL), lambda b, j, local=local: (local(b), j, 0)),
                     pl.BlockSpec((1, TM, D_MODEL), lambda b, j, step=step: (0, step(b, j), 0)),
                     pl.BlockSpec((1, TM, D_MODEL), lambda b, j, step=step: (1, step(b, j), 0)),
                     pl.BlockSpec((1, ROUTE_ROWS, TM), lambda b, j, step=step: (step(b, j), 0, 0))]
        args += [h, y_tok, y_tok, route]
    return pl.pallas_call(
        functools.partial(_ple_kernel, n_parts, part_b),
        grid=(n_parts * part_b, N_ROW_STEPS),
        in_specs=in_specs + [rows(PLE_DIM), _full(wp), _full(gp), _full(wg), _full(bg)],
        out_specs=rows(D_MODEL),
        out_shape=jax.ShapeDtypeStruct((n_parts * part_b, SEQ, D_MODEL), F32),
        compiler_params=_cparams(("parallel", "parallel")),
        name="ple_gate",
    )(*args, p, wp, gp, wg, bg)


def _t5_bucket(dist):
    max_exact = REL_BUCKETS // 2
    n = jnp.maximum(dist, 0)
    nf = jnp.maximum(n, 1).astype(F32)
    large = max_exact + (jnp.log(nf / max_exact) / math.log(REL_MAX_DISTANCE / max_exact)
                         * (REL_BUCKETS - max_exact)).astype(jnp.int32)
    large = jnp.minimum(large, REL_BUCKETS - 1)
    return jnp.where(n < max_exact, n, large)


def _bias_table(rel_bias, local_index, dilation, with_prev):
    loc = np.asarray(local_index)
    delta = loc[:, None] - loc[None, :]
    if with_prev:
        delta = np.concatenate([delta + BAND_BLOCK, delta], axis=1)
    ok = (delta >= 0) & (delta <= BAND_BLOCK)
    bucket = _t5_bucket(jnp.asarray(delta * dilation, jnp.int32))
    picked = jnp.where(bucket[None, :, :, None] == jnp.arange(REL_BUCKETS, dtype=jnp.int32),
                       rel_bias.astype(F32).T[:, None, None, :], 0.0)
    return jnp.where(jnp.asarray(ok)[None], jnp.sum(picked, axis=-1) * LOG2E, NEG)


def _block_diag_ones(sizes, total):
    g = np.zeros((total, total), np.float32)
    o = 0
    for s, on in sizes:
        if on:
            g[o:o + s, o:o + s] = 1.0
        o += s
    return jnp.asarray(g, BF16)


def _dispatch_plan(route, counts, n_tokens):
    field = lambda i: route[:, i, :].reshape(n_tokens).astype(jnp.int32)
    e = jnp.stack([field(R_E1), field(R_E2)], axis=-1)
    rank = jnp.stack([field(R_RANK1), field(R_RANK2)], axis=-1)
    pcounts = (counts + MOE_TM - 1) // MOE_TM * MOE_TM
    pend = jnp.cumsum(pcounts)
    pstart = pend - pcounts
    ids = jnp.arange(N_EXPERTS, dtype=jnp.int32)
    pos = rank + jnp.sum(jnp.where(e[..., None] == ids, pstart, 0), axis=-1)
    n_assign = n_tokens * TOP_K
    shift = int(math.ceil(math.log2(n_assign)))
    keys = (e.reshape(-1) << shift) | jnp.arange(n_assign, dtype=jnp.int32)
    tok_sorted = (jnp.sort(keys) & ((1 << shift) - 1)) // TOP_K
    n_rows = n_assign + N_EXPERTS * MOE_TM
    rows = jnp.arange(n_rows, dtype=jnp.int32)
    row_e = jnp.sum(rows[:, None] >= pend[None, :], axis=-1)
    row_e = jnp.minimum(row_e, N_EXPERTS - 1)
    pick = lambda tbl: jnp.sum(jnp.where(row_e[:, None] == ids, tbl, 0), axis=-1)
    within = rows - pick(pstart)
    src = jnp.clip(pick(jnp.cumsum(counts) - counts) + within, 0, n_assign - 1)
    row_tok = jnp.where(within < pick(counts), tok_sorted[src], rows % n_tokens)
    n_blocks = n_rows // MOE_TM
    n_used = (pend[-1] // MOE_TM).astype(jnp.int32)
    blk_start = jnp.minimum(jnp.arange(n_blocks, dtype=jnp.int32), n_used - 1) * MOE_TM
    blk_e = jnp.minimum(jnp.sum(blk_start[:, None] >= pend[None, :], axis=-1), N_EXPERTS - 1).astype(jnp.int32)
    return row_tok, pos, blk_e, n_used.reshape(1)


def kernel(x, p, rel_bias, norm_mix_gain, w_in, qn_a_gain, kn_a_gain, q_a_gain, w_q_up, kv_a_gain, w_kv_up, qn_nope_gain, qn_rope_gain, kn_nope_gain, kn_rope_gain, w_out, norm_ffn_gain, w_router_group, b_router_group, w_router_expert, b_router_expert, w_exp_gate, w_exp_up, w_exp_down, w_ple_proj, ple_norm_gain, w_ple_gate, b_ple_gate):
    B, S, D = x.shape
    assert (S, D) == (SEQ, D_MODEL) and p.shape[0] == 1
    row = lambda a: a.reshape(1, -1).astype(F32)
    zeros = lambda *s: jnp.zeros(s, F32)

    w_in_p = jnp.concatenate([w_in[0, :, :C_KR0], zeros(D, 64), w_in[0, :, C_KR0:], zeros(D, 32)],
                             axis=1).astype(BF16)
    gsum_a = _block_diag_ones([(HEAD_DIM_A, True)] * N_HEADS_A, A_WIDTH)
    gq_a = row(jnp.tile(qn_a_gain[0], N_HEADS_A)) * (HEAD_DIM_A ** -0.5 * LOG2E)
    gk_a = row(jnp.tile(kn_a_gain[0], N_HEADS_A))

    wq_p = jnp.pad(w_q_up[0].reshape(Q_LORA, N_HEADS_M, NOPE_DIM + ROPE_DIM),
                   ((0, 0), (0, 0), (0, HEAD_SLOT - NOPE_DIM - ROPE_DIM))).reshape(Q_LORA, -1).astype(BF16)
    wkv = w_kv_up[0].reshape(KV_LORA, N_HEADS_M, NOPE_DIM + V_DIM)
    wk_p = jnp.pad(wkv[..., :NOPE_DIM], ((0, 0), (0, 0), (0, HEAD_SLOT - NOPE_DIM))).reshape(KV_LORA, -1)
    wkv_p = jnp.concatenate([wk_p, wkv[..., NOPE_DIM:].reshape(KV_LORA, -1)], axis=1).astype(BF16)
    gs_q = _block_diag_ones([(NOPE_DIM, True), (ROPE_DIM, True), (32, False)], HEAD_SLOT)
    gs_k = _block_diag_ones([(NOPE_DIM, True), (64, False)], HEAD_SLOT)
    inv_cnt_q = jnp.asarray(np.concatenate([np.full(64, 1 / NOPE_DIM), np.full(32, 1 / ROPE_DIM),
                                            np.ones(32)]).astype(np.float32)).reshape(1, HEAD_SLOT)
    mla_scale = (NOPE_DIM + ROPE_DIM) ** -0.5 * LOG2E
    gq_m = row(jnp.concatenate([qn_nope_gain[0], qn_rope_gain[0], zeros(32)])) * mla_scale
    gk_m = row(jnp.concatenate([kn_nope_gain[0], zeros(64)]))
    gkr_m = row(jnp.concatenate([zeros(64), kn_rope_gain[0], zeros(32)]))

    half = ROPE_DIM // 2
    inv = 1.0 / (ROPE_THETA ** (jnp.arange(half, dtype=F32) * 2.0 / ROPE_DIM))
    ang = jnp.arange(S, dtype=jnp.int32).astype(F32)[:, None] * inv[None, :]
    cosv, sinv = jnp.cos(ang), jnp.sin(ang)
    cos_t = jnp.concatenate([jnp.ones((S, 64), F32), cosv, cosv, jnp.ones((S, 32), F32)], -1)
    sin_a = jnp.concatenate([zeros(S, 80), sinv, zeros(S, 32)], -1)
    sin_b = jnp.concatenate([zeros(S, 64), -sinv, zeros(S, 48)], -1)

    loc1 = [16 * a + r for r in range(16) for a in range(8)]
    loc2 = [4 * a + c for c in range(4) for a in range(32)]
    loc3 = list(range(BAND_BLOCK))
    t1 = _bias_table(rel_bias, loc1, 1, True)
    t2 = _bias_table(rel_bias, loc2, 4, True)
    t3 = jnp.concatenate([jnp.full((N_HEADS_A, BAND_BLOCK, BAND_BLOCK), NEG, F32),
                          _bias_table(rel_bias, loc3, 16, False)], axis=-1)

    n_pad_g, n_pad_e = ROUTE_GROUP_ROWS - N_GROUPS, LANES - ROUTE_EXPERT_ROW0 - N_EXPERTS
    w_r = jnp.concatenate([w_router_group[0].T, zeros(n_pad_g, D), w_router_expert[0].T, zeros(n_pad_e, D)], axis=0)
    w_r_hi = w_r.astype(BF16)
    w_r_lo = (w_r - w_r_hi.astype(F32)).astype(BF16)
    b_r = jnp.concatenate([b_router_group[0], jnp.full((n_pad_g,), NEG, F32), b_router_expert[0],
                           zeros(n_pad_e)]).reshape(LANES, 1)
    tri = jnp.asarray(np.triu(np.ones((TM, TM), np.float32), 1), BF16)

    w_out_b = w_out[0].astype(BF16)
    part_b = B // BATCH_PARTS
    n_tokens = part_b * S
    parts = []
    for b0 in range(0, B, part_b):
        qa, ka, va, cq, ckv, kr = _inproj(x, b0, part_b, row(norm_mix_gain[0]), w_in_p, gsum_a, gq_a, gk_a,
                                          row(q_a_gain[0]), row(kv_a_gain[0]))
        o_a = _dilated(qa, ka, va, t1, t2, t3)
        q_m, k_m, v_m = _mla_prep(cq, ckv, kr, wq_p, wkv_p, gs_q, gs_k, inv_cnt_q, gq_m, gk_m, gkr_m,
                                  cos_t, sin_a, sin_b)
        o_m = _mla_attn(q_m, k_m, v_m)
        h1, xn2, route, cnt = _outproj(o_a, o_m, x, b0, w_out_b, row(norm_ffn_gain[0]),
                                       w_r_hi, w_r_lo, b_r, tri)
        row_tok, pos, blk_e, n_used = _dispatch_plan(route, cnt[:, 0].astype(jnp.int32), n_tokens)
        xs = xn2.reshape(n_tokens, D)[row_tok]
        y = _experts(blk_e, n_used, xs, w_exp_gate[0], w_exp_up[0], w_exp_down[0])
        y_tok = y[pos.T.reshape(-1)].reshape(TOP_K, n_tokens, D)
        parts.append((h1, y_tok, route))
    return _ple(parts, p[0], w_ple_proj[0].astype(BF16), row(ple_norm_gain[0]),
                w_ple_gate[0].astype(BF16), row(b_ple_gate[0]))
```

```python
import functools
import math

import jax
import jax.numpy as jnp
import numpy as np
from jax import lax
from jax.experimental import pallas as pl
from jax.experimental.pallas import tpu as pltpu

F32 = jnp.float32
BF16 = jnp.bfloat16

D_MODEL = 1024
SEQ = 2048
PLE_DIM = 256
EPS = 1e-6
NEG = -1e30
HEAD_DIM_A = 64
A_WIDTH = 512
N_HEADS_A = 8
BAND_BLOCK = 128
REL_BUCKETS = 32
REL_MAX_DISTANCE = 2048
M_WIDTH = 512
V_DIM = 64
N_HEADS_M = 8
Q_LORA = 384
KV_LORA = 256
NOPE_DIM = 64
ROPE_DIM = 32
ROPE_THETA = 10000.0
N_GROUPS = 4
EXPERTS_PER_GROUP = 8
N_EXPERTS = 32
TOP_K = 2
EXPERT_FF = 512

LANES = 128
N_RES = 16
M_SUB = SEQ // N_RES
TM = 512
N_ROW_STEPS = SEQ // TM
M_STEP = TM // N_RES
HEAD_SLOT = LANES
MLA_TQ = 256
MLA_SKEW = 4
MOE_TM = 512
ATTN_SKEW = 5
BATCH_PARTS = 1
SUB_TILES = 2
IN_ROWS, IN_TILES = 512, 2
VMEM_LIMIT = 48 * 1024 * 1024

IN_COLS_PAD = 3 * A_WIDTH + Q_LORA + KV_LORA + LANES
C_Q0, C_K0, C_V0, C_CQ0, C_CKV0, C_KR0 = 0, 512, 1024, 1536, 1920, 2176

NT_DIMS = (((1,), (1,)), ((), ()))
LOG2E = math.log2(math.e)
LN2 = math.log(2.0)


def _cparams(sem):
    return pltpu.CompilerParams(dimension_semantics=sem, vmem_limit_bytes=VMEM_LIMIT)


def _full(a):
    return pl.BlockSpec(a.shape, lambda *_: (0,) * a.ndim)


def _rms(x, n):
    return x * lax.rsqrt(jnp.sum(x * x, axis=-1, keepdims=True) * (1.0 / n) + EPS)


def _residue_rows(r):
    return pl.ds(r, M_STEP, stride=N_RES)


def _inproj_kernel(x_ref, g_ref, w_ref, gsum_ref, gq_ref, gk_ref, gcq_ref, gckv_ref,
                   qa_ref, ka_ref, va_ref, cq_ref, ckv_ref, kr_ref, perm_scr, xn_scr):
    n_chunks = A_WIDTH // LANES
    for s in range(IN_TILES):
        xn_scr[s] = (_rms(x_ref[0, s * IN_ROWS:(s + 1) * IN_ROWS], D_MODEL) * g_ref[...]).astype(BF16)

    def head_norm(t, gain_ref):
        ss = jnp.dot((t * t).astype(BF16), gsum_ref[...], preferred_element_type=F32)
        return t * lax.rsqrt(ss * (1.0 / HEAD_DIM_A) + EPS) * gain_ref[...]

    m_tile = IN_ROWS // N_RES
    for s in range(IN_TILES):
        rows = slice(s * IN_ROWS, (s + 1) * IN_ROWS)
        proj = lambda c0, c1, s=s: jnp.dot(xn_scr[s], w_ref[:, c0:c1], preferred_element_type=F32)

        def put_residue(ref, val, t, s=s):
            for c in range(n_chunks):
                perm_scr[s, t, c] = val[:, c * LANES:(c + 1) * LANES]
            for r in range(N_RES):
                for c in range(n_chunks):
                    ref[0, r, s * m_tile:(s + 1) * m_tile, c * LANES:(c + 1) * LANES] = (
                        perm_scr[s, t, c, pl.ds(r, m_tile, stride=N_RES), :].astype(ref.dtype))

        put_residue(qa_ref, head_norm(proj(C_Q0, C_K0), gq_ref), 0)
        put_residue(ka_ref, head_norm(proj(C_K0, C_V0), gk_ref), 1)
        put_residue(va_ref, proj(C_V0, C_CQ0), 2)
        cq_ref[0, rows] = (_rms(proj(C_CQ0, C_CKV0), Q_LORA) * gcq_ref[...]).astype(BF16)
        ckv_ref[0, rows] = (_rms(proj(C_CKV0, C_KR0), KV_LORA) * gckv_ref[...]).astype(BF16)
        kr_ref[0, rows] = proj(C_KR0, IN_COLS_PAD)


def _inproj(x, b0, B, g, w_in_p, gsum, gq, gk, gcq, gckv):
    step_rows = IN_TILES * IN_ROWS
    res = lambda: (jax.ShapeDtypeStruct((B, N_RES, M_SUB, A_WIDTH), BF16),
                   pl.BlockSpec((1, N_RES, step_rows // N_RES, A_WIDTH), lambda b, j: (b, 0, j, 0)))
    nat = lambda c, dt: (jax.ShapeDtypeStruct((B, SEQ, c), dt),
                         pl.BlockSpec((1, step_rows, c), lambda b, j: (b, j, 0)))
    outs = [res(), res(), res(), nat(Q_LORA, BF16), nat(KV_LORA, BF16), nat(LANES, F32)]
    return pl.pallas_call(
        _inproj_kernel,
        grid=(B, SEQ // step_rows),
        in_specs=[pl.BlockSpec((1, step_rows, D_MODEL), lambda b, j: (b + b0, j, 0)),
                  _full(g), _full(w_in_p), _full(gsum), _full(gq), _full(gk), _full(gcq), _full(gckv)],
        out_specs=[o[1] for o in outs],
        out_shape=[o[0] for o in outs],
        scratch_shapes=[pltpu.VMEM((IN_TILES, 3, A_WIDTH // LANES, IN_ROWS, LANES), F32),
                        pltpu.VMEM((IN_TILES, IN_ROWS, D_MODEL), BF16)],
        compiler_params=_cparams(("parallel", "parallel")),
        name="inproj",
    )(x, g, w_in_p, gsum, gq, gk, gcq, gckv)


def _mla_prep_kernel(cq_ref, ckv_ref, kr_ref, wq_ref, wkv_ref, gsq_ref, gsk_ref, icq_ref,
                     gq_ref, gk_ref, gkr_ref, cos_ref, sa_ref, sb_ref, q_ref, k_ref, v_ref):
    sub = TM // SUB_TILES
    lane = lax.broadcasted_iota(jnp.int32, (sub, LANES), 1)

    for t in range(SUB_TILES):
        rows = slice(t * sub, (t + 1) * sub)
        cos, sa, sb = cos_ref[rows, :], sa_ref[rows, :], sb_ref[rows, :]

        def rope(x, cos=cos, sa=sa, sb=sb):
            return x * cos + pltpu.roll(x, 16, 1) * sa + pltpu.roll(x, LANES - 16, 1) * sb

        q = jnp.dot(cq_ref[0, rows, :], wq_ref[...], preferred_element_type=F32)
        kv = jnp.dot(ckv_ref[0, rows, :], wkv_ref[...], preferred_element_type=F32)
        k_rope = rope(_rms(kr_ref[0, rows, :], ROPE_DIM) * gkr_ref[...])
        for h in range(N_HEADS_M):
            sl = slice(h * HEAD_SLOT, (h + 1) * HEAD_SLOT)
            qh = q[:, sl]
            ss = jnp.dot((qh * qh).astype(BF16), gsq_ref[...], preferred_element_type=F32)
            q_ref[0, rows, sl] = rope(qh * lax.rsqrt(ss * icq_ref[...] + EPS) * gq_ref[...]).astype(BF16)
            kh = kv[:, sl]
            ssk = jnp.dot((kh * kh).astype(BF16), gsk_ref[...], preferred_element_type=F32)
            kn = kh * lax.rsqrt(ssk * (1.0 / NOPE_DIM) + EPS) * gk_ref[...] + k_rope
            k_ref[0, rows, sl] = kn.astype(BF16)
        for hp in range(N_HEADS_M // 2):
            v_pair = kv[:, N_HEADS_M * HEAD_SLOT + hp * LANES:N_HEADS_M * HEAD_SLOT + (hp + 1) * LANES]
            v_ref[0, rows, (2 * hp) * HEAD_SLOT:(2 * hp + 1) * HEAD_SLOT] = jnp.where(lane < V_DIM, v_pair, 1.0).astype(BF16)
            v_ref[0, rows, (2 * hp + 1) * HEAD_SLOT:(2 * hp + 2) * HEAD_SLOT] = jnp.where(lane < V_DIM, 1.0, v_pair).astype(BF16)


def _mla_prep(cq, ckv, kr, wq_p, wkv_p, gsq, gsk, icq, gq, gk, gkr, cos, sa, sb):
    B = cq.shape[0]
    rows = lambda c: pl.BlockSpec((1, TM, c), lambda b, j: (b, j, 0))
    tab = pl.BlockSpec((TM, LANES), lambda b, j: (j, 0))
    wide = N_HEADS_M * HEAD_SLOT
    return pl.pallas_call(
        _mla_prep_kernel,
        grid=(B, N_ROW_STEPS),
        in_specs=[rows(Q_LORA), rows(KV_LORA), rows(LANES), _full(wq_p), _full(wkv_p), _full(gsq),
                  _full(gsk), _full(icq), _full(gq), _full(gk), _full(gkr), tab, tab, tab],
        out_specs=[rows(wide)] * 3,
        out_shape=[jax.ShapeDtypeStruct((B, SEQ, wide), BF16)] * 3,
        compiler_params=_cparams(("parallel", "parallel")),
        name="mla_prep",
    )(cq, ckv, kr, wq_p, wkv_p, gsq, gsk, icq, gq, gk, gkr, cos, sa, sb)


def _mla_attn_kernel(q_ref, k_ref, v_ref, o_ref):
    n_q = SEQ // MLA_TQ
    row = lax.broadcasted_iota(jnp.int32, (MLA_TQ, MLA_TQ), 0)
    col = lax.broadcasted_iota(jnp.int32, (MLA_TQ, MLA_TQ), 1)
    lane = lax.broadcasted_iota(jnp.int32, (MLA_TQ, LANES), 1)
    heads = [slice(hh * HEAD_SLOT, (hh + 1) * HEAD_SLOT) for hh in range(2)]

    def probs(i, hh):
        n_keys = (i + 1) * MLA_TQ
        s = lax.dot_general(q_ref[0, i * MLA_TQ:n_keys, heads[hh]], k_ref[0, 0:n_keys, heads[hh]],
                            NT_DIMS, preferred_element_type=F32)
        diag = jnp.where(col <= row, s[:, n_keys - MLA_TQ:], NEG)
        s = diag if i == 0 else jnp.concatenate([s[:, :n_keys - MLA_TQ], diag], axis=1)
        return jnp.exp2(s - jnp.max(s, axis=-1, keepdims=True)).astype(BF16)

    def values(i, hh, p):
        return jnp.dot(p, v_ref[0, 0:(i + 1) * MLA_TQ, heads[hh]], preferred_element_type=F32)

    units = [(i, hh) for i in range(n_q) for hh in range(2)]
    acc, pending = {}, []
    for u in units:
        pending.append((u, probs(*u)))
        if len(pending) > MLA_SKEW:
            done, p = pending.pop(0)
            acc[done] = values(*done, p)
    for done, p in pending:
        acc[done] = values(*done, p)
    for i in range(n_q):
        num = jnp.where(lane < V_DIM, acc[(i, 0)], acc[(i, 1)])
        den = pltpu.roll(jnp.where(lane < V_DIM, acc[(i, 1)], acc[(i, 0)]), V_DIM, 1)
        o_ref[0, i * MLA_TQ:(i + 1) * MLA_TQ, :] = (num / den).astype(BF16)


def _mla_attn(q, k, v):
    B = q.shape[0]
    pair = lambda c: pl.BlockSpec((1, SEQ, c), lambda b, h: (b, 0, h))
    return pl.pallas_call(
        _mla_attn_kernel,
        grid=(B, N_HEADS_M // 2),
        in_specs=[pair(2 * HEAD_SLOT)] * 3,
        out_specs=pair(2 * V_DIM),
        out_shape=jax.ShapeDtypeStruct((B, SEQ, M_WIDTH), BF16),
        compiler_params=_cparams(("parallel", "parallel")),
        name="mla_attn",
    )(q, k, v)


N_BLK = SEQ // BAND_BLOCK


def _dilated_kernel(q_ref, k_ref, v_ref, t1_ref, t2_ref, t3_ref, ones_ref, o_ref,
                    q1, k1, v1, q2, k2, v2, ob1, ls1, ob2, ls2, ob3, ls3):
    bb = BAND_BLOCK
    lane = lax.broadcasted_iota(jnp.int32, (bb, LANES), 1)
    lane_row = lax.broadcasted_iota(jnp.int32, (1, LANES), 1)
    own = [(lane_row < HEAD_DIM_A).astype(BF16), (lane_row >= HEAD_DIM_A).astype(BF16)]
    rows = lambda lo, hi: slice(lo * bb, hi * bb)

    for src, d1, d2 in ((q_ref, q1, q2), (k_ref, k1, k2), (v_ref, v1, v2)):
        for n2 in range(N_BLK // 2):
            pieces = [src[0, r, 16 * n2:16 * n2 + 16, :].astype(F32) for r in range(N_RES)]
            d1[rows(2 * n2, 2 * n2 + 1), :] = jnp.concatenate([p[0:8] for p in pieces], axis=0).astype(BF16)
            d1[rows(2 * n2 + 1, 2 * n2 + 2), :] = jnp.concatenate([p[8:16] for p in pieces], axis=0).astype(BF16)
        for r4 in range(4):
            for n in range(4):
                d2[rows(r4 * 4 + n, r4 * 4 + n + 1), :] = jnp.concatenate(
                    [src[0, r4 + 4 * c, 32 * n:32 * n + 32, :] for c in range(4)], axis=0)

    blocks = []
    for idx in range(N_BLK):
        lo = idx - 1 if idx > 0 else idx
        blocks.append((q1, k1, v1, rows(lo, idx + 1), t1_ref, ob1, ls1, idx))
    for idx in range(N_BLK):
        lo = idx - 1 if idx % 4 else idx
        blocks.append((q2, k2, v2, rows(lo, idx + 1), t2_ref, ob2, ls2, idx))
    for r in range(N_RES):
        blocks.append((None, None, None, r, t3_ref, ob3, ls3, r))

    def scores_and_probs(blk, hh):
        qd, kd, _, kv_rows, t_ref, _, _, idx = blk
        q = qd[rows(idx, idx + 1), :] if qd is not None else q_ref[0, idx]
        keys = kd[kv_rows, :] if kd is not None else k_ref[0, kv_rows]
        n_keys = keys.shape[0]
        s = lax.dot_general(q * own[hh], keys, NT_DIMS, preferred_element_type=F32)
        s = s + t_ref[hh, :, 2 * bb - n_keys:2 * bb]
        m = jnp.max(s, axis=-1, keepdims=True)
        return m, jnp.exp2(s - m).astype(BF16)

    def finish(blk, m, e):
        _, _, vd, kv_rows, _, o_dst, l_dst, idx = blk
        vals = vd[kv_rows, :] if vd is not None else v_ref[0, kv_rows]
        n_keys = vals.shape[0]
        num = (jnp.dot(e[0], vals * own[0], preferred_element_type=F32)
               + jnp.dot(e[1], vals * own[1], preferred_element_type=F32))
        den = (jnp.dot(e[0], ones_ref[0, 0:n_keys, :], preferred_element_type=F32)
               + jnp.dot(e[1], ones_ref[1, 0:n_keys, :], preferred_element_type=F32))
        o_dst[idx] = num / den
        l_dst[idx] = jnp.where(lane < HEAD_DIM_A, m[0], m[1]) * LN2 + jnp.log(den)

    stage = {}
    for t in range(len(blocks) + ATTN_SKEW):
        if t < len(blocks):
            stage[t] = [scores_and_probs(blocks[t], hh) for hh in range(2)]
        d = t - ATTN_SKEW
        if d >= 0:
            finish(blocks[d], [stage[d][hh][0] for hh in range(2)], [stage[d][hh][1] for hh in range(2)])
            del stage[d]

    for r in range(N_RES):
        r4, c = r % 4, r // 4
        gather1 = lambda ref: jnp.concatenate([ref[n, 8 * r:8 * r + 8, :] for n in range(N_BLK)], axis=0)
        gather2 = lambda ref: jnp.concatenate(
            [ref[r4 * 4 + n, 32 * c:32 * c + 32, :] for n in range(4)], axis=0)
        o_b = [gather1(ob1), gather2(ob2), ob3[r]]
        l_b = [gather1(ls1), gather2(ls2), ls3[r]]
        top = jnp.maximum(jnp.maximum(l_b[0], l_b[1]), l_b[2])
        w_b = [jnp.exp(l - top) for l in l_b]
        num = w_b[0] * o_b[0] + w_b[1] * o_b[1] + w_b[2] * o_b[2]
        o_ref[0, r] = (num / (w_b[0] + w_b[1] + w_b[2])).astype(BF16)


def _dilated(qa, ka, va, t1, t2, t3):
    B = qa.shape[0]
    head_lanes = np.arange(LANES)[None, None, :] // HEAD_DIM_A == np.arange(2)[:, None, None]
    ones = jnp.asarray(np.broadcast_to(head_lanes, (2, 2 * BAND_BLOCK, LANES)).astype(np.float32), BF16)
    blk = pl.BlockSpec((1, N_RES, M_SUB, LANES), lambda b, h: (b, 0, 0, h))
    tab = lambda t: pl.BlockSpec((2,) + t.shape[1:], lambda b, h: (h, 0, 0))
    blocked_bf16 = pltpu.VMEM((N_BLK * BAND_BLOCK, LANES), BF16)
    blocked_f32 = pltpu.VMEM((N_BLK, BAND_BLOCK, LANES), F32)
    return pl.pallas_call(
        _dilated_kernel,
        grid=(B, N_HEADS_A // 2),
        in_specs=[blk, blk, blk, tab(t1), tab(t2), tab(t3), _full(ones)],
        out_specs=blk,
        out_shape=jax.ShapeDtypeStruct((B, N_RES, M_SUB, A_WIDTH), BF16),
        scratch_shapes=[blocked_bf16] * 6 + [blocked_f32] * 6,
        compiler_params=_cparams(("parallel", "parallel")),
        name="dilated_attn",
    )(qa, ka, va, t1, t2, t3, ones)


R_W1, R_W2, R_E1, R_E2, R_RANK1, R_RANK2 = range(6)
ROUTE_ROWS = 8
ROUTE_GROUP_ROWS = 8
ROUTE_EXPERT_ROW0 = 8


def _outproj_kernel(oa_ref, om_ref, x_ref, wo_ref, g_ref, wrh_ref, wrl_ref, br_ref, tri_ref,
                    h_ref, xn_ref, route_ref, cnt_ref, perm_scr, carry_scr):
    @pl.when((pl.program_id(0) == 0) & (pl.program_id(1) == 0))
    def _():
        carry_scr[...] = jnp.zeros(carry_scr.shape, F32)

    n_chunks = A_WIDTH // LANES
    for r in range(N_RES):
        for c in range(n_chunks):
            perm_scr[c, _residue_rows(r), :] = oa_ref[0, r, :, c * LANES:(c + 1) * LANES].astype(F32)
    oa = jnp.concatenate([perm_scr[c] for c in range(n_chunks)], axis=1).astype(BF16)
    h = (x_ref[0] + jnp.dot(oa, wo_ref[0:A_WIDTH, :], preferred_element_type=F32)
         + jnp.dot(om_ref[0], wo_ref[A_WIDTH:, :], preferred_element_type=F32))
    h_ref[0] = h
    xn = _rms(h, D_MODEL) * g_ref[...]
    xn_ref[0] = xn.astype(BF16)
    hi = xn.astype(BF16)
    lo = (xn - hi.astype(F32)).astype(BF16)
    lg = (lax.dot_general(wrh_ref[...], hi, NT_DIMS, preferred_element_type=F32)
          + lax.dot_general(wrh_ref[...], lo, NT_DIMS, preferred_element_type=F32)
          + lax.dot_general(wrl_ref[...], hi, NT_DIMS, preferred_element_type=F32)) + br_ref[...]

    sub = lax.broadcasted_iota(jnp.int32, (EXPERTS_PER_GROUP, TM), 0).astype(F32)
    cmax = lambda t: jnp.max(t, axis=0, keepdims=True)
    cmin = lambda t: jnp.min(t, axis=0, keepdims=True)
    csum = lambda t: jnp.sum(t, axis=0, keepdims=True)
    none = float(EXPERTS_PER_GROUP)

    gl = lg[0:ROUTE_GROUP_ROWS]
    ge = jnp.exp(gl - cmax(gl))
    gsum = csum(ge)
    g_gate = 1.0 / gsum
    g_idx = cmin(jnp.where(ge / gsum == g_gate, sub, none))
    el = lg[ROUTE_EXPERT_ROW0:ROUTE_EXPERT_ROW0 + EXPERTS_PER_GROUP]
    for g in range(1, N_GROUPS):
        r0 = ROUTE_EXPERT_ROW0 + g * EXPERTS_PER_GROUP
        el = jnp.where(g_idx == float(g), lg[r0:r0 + EXPERTS_PER_GROUP], el)
    ee = jnp.exp(el - cmax(el))
    esum = csum(ee)
    eprob = ee / esum
    p1 = 1.0 / esum
    i1 = cmin(jnp.where(eprob == p1, sub, none))
    rest = jnp.where(sub == i1, -1.0, eprob)
    p2 = cmax(rest)
    i2 = cmin(jnp.where(rest == p2, sub, none))
    den = p1 + p2
    e1 = g_idx * EXPERTS_PER_GROUP + i1
    e2 = g_idx * EXPERTS_PER_GROUP + i2
    erow = lax.broadcasted_iota(jnp.int32, (N_EXPERTS, TM), 0).astype(F32)
    onehot = ((erow == e1) | (erow == e2)).astype(F32)
    before = jnp.dot(onehot.astype(BF16), tri_ref[...], preferred_element_type=F32) + carry_scr[:, 0:1]
    rank1 = csum(jnp.where(erow == e1, before, 0.0))
    rank2 = csum(jnp.where(erow == e2, before, 0.0))
    carry_scr[...] = carry_scr[...] + jnp.sum(onehot, axis=1, keepdims=True)
    cnt_ref[...] = carry_scr[...]

    record = jnp.zeros((ROUTE_ROWS, TM), F32)
    for i, val in ((R_W1, g_gate * (p1 / den)), (R_W2, g_gate * (p2 / den)), (R_E1, e1), (R_E2, e2),
                   (R_RANK1, rank1), (R_RANK2, rank2)):
        record = jnp.where(sub == float(i), val, record)
    route_ref[0] = record


def _outproj(oa, om, x, b0, wo, g, wrh, wrl, br, tri):
    B = oa.shape[0]
    rows = lambda c: pl.BlockSpec((1, TM, c), lambda b, j: (b, j, 0))
    shp = lambda c, dt: jax.ShapeDtypeStruct((B, SEQ, c), dt)
    return pl.pallas_call(
        _outproj_kernel,
        grid=(B, N_ROW_STEPS),
        in_specs=[pl.BlockSpec((1, N_RES, M_STEP, A_WIDTH), lambda b, j: (b, 0, j, 0)),
                  rows(M_WIDTH), pl.BlockSpec((1, TM, D_MODEL), lambda b, j: (b + b0, j, 0)),
                  _full(wo), _full(g), _full(wrh), _full(wrl), _full(br), _full(tri)],
        out_specs=[rows(D_MODEL), rows(D_MODEL),
                   pl.BlockSpec((1, ROUTE_ROWS, TM), lambda b, j: (b * N_ROW_STEPS + j, 0, 0)),
                   pl.BlockSpec((N_EXPERTS, LANES), lambda b, j: (0, 0))],
        out_shape=[shp(D_MODEL, F32), shp(D_MODEL, BF16),
                   jax.ShapeDtypeStruct((B * N_ROW_STEPS, ROUTE_ROWS, TM), F32),
                   jax.ShapeDtypeStruct((N_EXPERTS, LANES), F32)],
        scratch_shapes=[pltpu.VMEM((A_WIDTH // LANES, TM, LANES), F32), pltpu.VMEM((N_EXPERTS, LANES), F32)],
        compiler_params=_cparams(("arbitrary", "arbitrary")),
        name="outproj_router",
    )(oa, om, x, wo, g, wrh, wrl, br, tri)


def _expert_kernel(blk_e_ref, n_used_ref, x_ref, wg_ref, wu_ref, wd_ref, y_ref, wg_s, wu_s, wd_s):
    i = pl.program_id(0)

    @pl.when(i < n_used_ref[0])
    def _():
        @pl.when((i == 0) | (blk_e_ref[i] != blk_e_ref[jnp.maximum(i - 1, 0)]))
        def _():
            wg_s[...] = wg_ref[0].astype(BF16)
            wu_s[...] = wu_ref[0].astype(BF16)
            wd_s[...] = wd_ref[0].astype(BF16)

        half = MOE_TM // SUB_TILES
        gate_up = []
        for s in range(SUB_TILES):
            x = x_ref[s * half:(s + 1) * half, :]
            gate_up.append((jnp.dot(x, wg_s[...], preferred_element_type=F32),
                            jnp.dot(x, wu_s[...], preferred_element_type=F32)))
        for s, (gate, up) in enumerate(gate_up):
            hdn = (gate * jax.nn.sigmoid(gate) * up).astype(BF16)
            y_ref[s * half:(s + 1) * half, :] = jnp.dot(hdn, wd_s[...], preferred_element_type=F32).astype(y_ref.dtype)

    @pl.when(i >= n_used_ref[0])
    def _():
        y_ref[...] = jnp.zeros(y_ref.shape, y_ref.dtype)


def _experts(blk_e, n_used, xs, wg, wu, wd):
    n_blocks = xs.shape[0] // MOE_TM
    row_in = lambda i, be, nu: (jnp.minimum(i, nu[0] - 1), 0)
    wsel = lambda i, be, nu: (be[i], 0, 0)
    return pl.pallas_call(
        _expert_kernel,
        grid_spec=pltpu.PrefetchScalarGridSpec(
            num_scalar_prefetch=2,
            grid=(n_blocks,),
            in_specs=[pl.BlockSpec((MOE_TM, D_MODEL), row_in),
                      pl.BlockSpec((1, D_MODEL, EXPERT_FF), wsel),
                      pl.BlockSpec((1, D_MODEL, EXPERT_FF), wsel),
                      pl.BlockSpec((1, EXPERT_FF, D_MODEL), wsel)],
            out_specs=pl.BlockSpec((MOE_TM, D_MODEL), lambda i, be, nu: (i, 0)),
            scratch_shapes=[pltpu.VMEM((D_MODEL, EXPERT_FF), BF16), pltpu.VMEM((D_MODEL, EXPERT_FF), BF16),
                            pltpu.VMEM((EXPERT_FF, D_MODEL), BF16)]),
        out_shape=jax.ShapeDtypeStruct(xs.shape, BF16),
        compiler_params=_cparams(("arbitrary",)),
        name="expert_ffn",
    )(blk_e, n_used, xs, wg, wu, wd)


def _ple_kernel(n_parts, part_b, *refs):
    parts = [refs[4 * i:4 * i + 4] for i in range(n_parts)]
    p_ref, wp_ref, gp_ref, wg_ref, bg_ref, o_ref = refs[4 * n_parts:]

    def combine_and_gate(h_ref, y1_ref, y2_ref, route_ref):
        rec = jnp.concatenate([route_ref[0], jnp.zeros((LANES - ROUTE_ROWS, TM), F32)], axis=0)
        route = jnp.concatenate([rec[:, c * LANES:(c + 1) * LANES].T for c in range(TM // LANES)], axis=0)
        h = (h_ref[0] + route[:, R_W1:R_W1 + 1] * y1_ref[0].astype(F32)
             + route[:, R_W2:R_W2 + 1] * y2_ref[0].astype(F32))
        e = _rms(jnp.dot(p_ref[0].astype(BF16), wp_ref[...], preferred_element_type=F32), D_MODEL) * gp_ref[...]
        g = jax.nn.sigmoid(jnp.dot(h.astype(BF16), wg_ref[...], preferred_element_type=F32) + bg_ref[...])
        o_ref[0] = h + g * e

    for i, part in enumerate(parts):
        pl.when(pl.program_id(0) // part_b == i)(functools.partial(combine_and_gate, *part))


def _ple(parts, p, wp, gp, wg, bg):
    n_parts, part_b = len(parts), parts[0][0].shape[0]
    rows = lambda c: pl.BlockSpec((1, TM, c), lambda b, j: (b, j, 0))
    in_specs, args = [], []
    for i, (h, y_tok, route) in enumerate(parts):
        local = lambda b, i=i: jnp.clip(b - i * part_b, 0, part_b - 1)
        step = lambda b, j, local=local: local(b) * N_ROW_STEPS + j
        in_specs += [pl.BlockSpec((1, TM, D_MODEL), lambda b, j, local=local: (local(b), j, 0)),
                     pl.BlockSpec((1, TM, D_MODEL), lambda b, j, step=step: (0, step(b, j), 0)),
                     pl.BlockSpec((1, TM, D_MODEL), lambda b, j, step=step: (1, step(b, j), 0)),
                     pl.BlockSpec((1, ROUTE_ROWS, TM), lambda b, j, step=step: (step(b, j), 0, 0))]
        args += [h, y_tok, y_tok, route]
    return pl.pallas_call(
        functools.partial(_ple_kernel, n_parts, part_b),
        grid=(n_parts * part_b, N_ROW_STEPS),
        in_specs=in_specs + [rows(PLE_DIM), _full(wp), _full(gp), _full(wg), _full(bg)],
        out_specs=rows(D_MODEL),
        out_shape=jax.ShapeDtypeStruct((n_parts * part_b, SEQ, D_MODEL), F32),
        compiler_params=_cparams(("parallel", "parallel")),
        name="ple_gate",
    )(*args, p, wp, gp, wg, bg)


def _t5_bucket(dist):
    max_exact = REL_BUCKETS // 2
    n = jnp.maximum(dist, 0)
    nf = jnp.maximum(n, 1).astype(F32)
    large = max_exact + (jnp.log(nf / max_exact) / math.log(REL_MAX_DISTANCE / max_exact)
                         * (REL_BUCKETS - max_exact)).astype(jnp.int32)
    large = jnp.minimum(large, REL_BUCKETS - 1)
    return jnp.where(n < max_exact, n, large)


def _bias_table(rel_bias, local_index, dilation, with_prev):
    loc = np.asarray(local_index)
    delta = loc[:, None] - loc[None, :]
    if with_prev:
        delta = np.concatenate([delta + BAND_BLOCK, delta], axis=1)
    ok = (delta >= 0) & (delta <= BAND_BLOCK)
    bucket = _t5_bucket(jnp.asarray(delta * dilation, jnp.int32))
    picked = jnp.where(bucket[None, :, :, None] == jnp.arange(REL_BUCKETS, dtype=jnp.int32),
                       rel_bias.astype(F32).T[:, None, None, :], 0.0)
    return jnp.where(jnp.asarray(ok)[None], jnp.sum(picked, axis=-1) * LOG2E, NEG)


def _block_diag_ones(sizes, total):
    g = np.zeros((total, total), np.float32)
    o = 0
    for s, on in sizes:
        if on:
            g[o:o + s, o:o + s] = 1.0
        o += s
    return jnp.asarray(g, BF16)


def _dispatch_plan(route, counts, n_tokens):
    field = lambda i: route[:, i, :].reshape(n_tokens).astype(jnp.int32)
    e = jnp.stack([field(R_E1), field(R_E2)], axis=-1)
    rank = jnp.stack([field(R_RANK1), field(R_RANK2)], axis=-1)
    pcounts = (counts + MOE_TM - 1) // MOE_TM * MOE_TM
    pend = jnp.cumsum(pcounts)
    pstart = pend - pcounts
    ids = jnp.arange(N_EXPERTS, dtype=jnp.int32)
    pos = rank + jnp.sum(jnp.where(e[..., None] == ids, pstart, 0), axis=-1)
    n_assign = n_tokens * TOP_K
    shift = int(math.ceil(math.log2(n_assign)))
    keys = (e.reshape(-1) << shift) | jnp.arange(n_assign, dtype=jnp.int32)
    tok_sorted = (jnp.sort(keys) & ((1 << shift) - 1)) // TOP_K
    n_rows = n_assign + N_EXPERTS * MOE_TM
    rows = jnp.arange(n_rows, dtype=jnp.int32)
    row_e = jnp.sum(rows[:, None] >= pend[None, :], axis=-1)
    row_e = jnp.minimum(row_e, N_EXPERTS - 1)
    pick = lambda tbl: jnp.sum(jnp.where(row_e[:, None] == ids, tbl, 0), axis=-1)
    within = rows - pick(pstart)
    src = jnp.clip(pick(jnp.cumsum(counts) - counts) + within, 0, n_assign - 1)
    row_tok = jnp.where(within < pick(counts), tok_sorted[src], rows % n_tokens)
    n_blocks = n_rows // MOE_TM
    n_used = (pend[-1] // MOE_TM).astype(jnp.int32)
    blk_start = jnp.minimum(jnp.arange(n_blocks, dtype=jnp.int32), n_used - 1) * MOE_TM
    blk_e = jnp.minimum(jnp.sum(blk_start[:, None] >= pend[None, :], axis=-1), N_EXPERTS - 1).astype(jnp.int32)
    return row_tok, pos, blk_e, n_used.reshape(1)


def kernel(x, p, rel_bias, norm_mix_gain, w_in, qn_a_gain, kn_a_gain, q_a_gain, w_q_up, kv_a_gain, w_kv_up, qn_nope_gain, qn_rope_gain, kn_nope_gain, kn_rope_gain, w_out, norm_ffn_gain, w_router_group, b_router_group, w_router_expert, b_router_expert, w_exp_gate, w_exp_up, w_exp_down, w_ple_proj, ple_norm_gain, w_ple_gate, b_ple_gate):
    B, S, D = x.shape
    assert (S, D) == (SEQ, D_MODEL) and p.shape[0] == 1
    row = lambda a: a.reshape(1, -1).astype(F32)
    zeros = lambda *s: jnp.zeros(s, F32)

    w_in_p = jnp.concatenate([w_in[0, :, :C_KR0], zeros(D, 64), w_in[0, :, C_KR0:], zeros(D, 32)],
                             axis=1).astype(BF16)
    gsum_a = _block_diag_ones([(HEAD_DIM_A, True)] * N_HEADS_A, A_WIDTH)
    gq_a = row(jnp.tile(qn_a_gain[0], N_HEADS_A)) * (HEAD_DIM_A ** -0.5 * LOG2E)
    gk_a = row(jnp.tile(kn_a_gain[0], N_HEADS_A))

    wq_p = jnp.pad(w_q_up[0].reshape(Q_LORA, N_HEADS_M, NOPE_DIM + ROPE_DIM),
                   ((0, 0), (0, 0), (0, HEAD_SLOT - NOPE_DIM - ROPE_DIM))).reshape(Q_LORA, -1).astype(BF16)
    wkv = w_kv_up[0].reshape(KV_LORA, N_HEADS_M, NOPE_DIM + V_DIM)
    wk_p = jnp.pad(wkv[..., :NOPE_DIM], ((0, 0), (0, 0), (0, HEAD_SLOT - NOPE_DIM))).reshape(KV_LORA, -1)
    wkv_p = jnp.concatenate([wk_p, wkv[..., NOPE_DIM:].reshape(KV_LORA, -1)], axis=1).astype(BF16)
    gs_q = _block_diag_ones([(NOPE_DIM, True), (ROPE_DIM, True), (32, False)], HEAD_SLOT)
    gs_k = _block_diag_ones([(NOPE_DIM, True), (64, False)], HEAD_SLOT)
    inv_cnt_q = jnp.asarray(np.concatenate([np.full(64, 1 / NOPE_DIM), np.full(32, 1 / ROPE_DIM),
                                            np.ones(32)]).astype(np.float32)).reshape(1, HEAD_SLOT)
    mla_scale = (NOPE_DIM + ROPE_DIM) ** -0.5 * LOG2E
    gq_m = row(jnp.concatenate([qn_nope_gain[0], qn_rope_gain[0], zeros(32)])) * mla_scale
    gk_m = row(jnp.concatenate([kn_nope_gain[0], zeros(64)]))
    gkr_m = row(jnp.concatenate([zeros(64), kn_rope_gain[0], zeros(32)]))

    half = ROPE_DIM // 2
    inv = 1.0 / (ROPE_THETA ** (jnp.arange(half, dtype=F32) * 2.0 / ROPE_DIM))
    ang = jnp.arange(S, dtype=jnp.int32).astype(F32)[:, None] * inv[None, :]
    cosv, sinv = jnp.cos(ang), jnp.sin(ang)
    cos_t = jnp.concatenate([jnp.ones((S, 64), F32), cosv, cosv, jnp.ones((S, 32), F32)], -1)
    sin_a = jnp.concatenate([zeros(S, 80), sinv, zeros(S, 32)], -1)
    sin_b = jnp.concatenate([zeros(S, 64), -sinv, zeros(S, 48)], -1)

    loc1 = [16 * a + r for r in range(16) for a in range(8)]
    loc2 = [4 * a + c for c in range(4) for a in range(32)]
    loc3 = list(range(BAND_BLOCK))
    t1 = _bias_table(rel_bias, loc1, 1, True)
    t2 = _bias_table(rel_bias, loc2, 4, True)
    t3 = jnp.concatenate([jnp.full((N_HEADS_A, BAND_BLOCK, BAND_BLOCK), NEG, F32),
                          _bias_table(rel_bias, loc3, 16, False)], axis=-1)

    n_pad_g, n_pad_e = ROUTE_GROUP_ROWS - N_GROUPS, LANES - ROUTE_EXPERT_ROW0 - N_EXPERTS
    w_r = jnp.concatenate([w_router_group[0].T, zeros(n_pad_g, D), w_router_expert[0].T, zeros(n_pad_e, D)], axis=0)
    w_r_hi = w_r.astype(BF16)
    w_r_lo = (w_r - w_r_hi.astype(F32)).astype(BF16)
    b_r = jnp.concatenate([b_router_group[0], jnp.full((n_pad_g,), NEG, F32), b_router_expert[0],
                           zeros(n_pad_e)]).reshape(LANES, 1)
    tri = jnp.asarray(np.triu(np.ones((TM, TM), np.float32), 1), BF16)

    w_out_b = w_out[0].astype(BF16)
    part_b = B // BATCH_PARTS
    n_tokens = part_b * S
    parts = []
    for b0 in range(0, B, part_b):
        qa, ka, va, cq, ckv, kr = _inproj(x, b0, part_b, row(norm_mix_gain[0]), w_in_p, gsum_a, gq_a, gk_a,
                                          row(q_a_gain[0]), row(kv_a_gain[0]))
        o_a = _dilated(qa, ka, va, t1, t2, t3)
        q_m, k_m, v_m = _mla_prep(cq, ckv, kr, wq_p, wkv_p, gs_q, gs_k, inv_cnt_q, gq_m, gk_m, gkr_m,
                                  cos_t, sin_a, sin_b)
        o_m = _mla_attn(q_m, k_m, v_m)
        h1, xn2, route, cnt = _outproj(o_a, o_m, x, b0, w_out_b, row(norm_ffn_gain[0]),
                                       w_r_hi, w_r_lo, b_r, tri)
        row_tok, pos, blk_e, n_used = _dispatch_plan(route, cnt[:, 0].astype(jnp.int32), n_tokens)
        xs = xn2.reshape(n_tokens, D)[row_tok]
        y = _experts(blk_e, n_used, xs, w_exp_gate[0], w_exp_up[0], w_exp_down[0])
        y_tok = y[pos.T.reshape(-1)].reshape(TOP_K, n_tokens, D)
        parts.append((h1, y_tok, route))
    return _ple(parts, p[0], w_ple_proj[0].astype(BF16), row(ple_norm_gain[0]),
                w_ple_gate[0].astype(BF16), row(b_ple_gate[0]))
```

```python
import functools
import math

import jax
import jax.numpy as jnp
import numpy as np
from jax import lax
from jax.experimental import pallas as pl
from jax.experimental.pallas import tpu as pltpu

F32 = jnp.float32
BF16 = jnp.bfloat16

D_MODEL = 1024
SEQ = 2048
PLE_DIM = 256
EPS = 1e-6
NEG = -1e30
HEAD_DIM_A = 64
A_WIDTH = 512
N_HEADS_A = 8
BAND_BLOCK = 128
REL_BUCKETS = 32
REL_MAX_DISTANCE = 2048
M_WIDTH = 512
V_DIM = 64
N_HEADS_M = 8
Q_LORA = 384
KV_LORA = 256
NOPE_DIM = 64
ROPE_DIM = 32
ROPE_THETA = 10000.0
N_GROUPS = 4
EXPERTS_PER_GROUP = 8
N_EXPERTS = 32
TOP_K = 2
EXPERT_FF = 512

LANES = 128
N_RES = 16
M_SUB = SEQ // N_RES
TM = 1024
N_ROW_STEPS = SEQ // TM
M_STEP = TM // N_RES
HEAD_SLOT = LANES
MLA_TQ = 256
MLA_SKEW = 4
MOE_TM = 512
ATTN_SKEW = 5
BATCH_PARTS = 1
SUB_TILES = 2
IN_ROWS, IN_TILES = 512, 2
VMEM_LIMIT = 48 * 1024 * 1024

IN_COLS_PAD = 3 * A_WIDTH + Q_LORA + KV_LORA + LANES
C_Q0, C_K0, C_V0, C_CQ0, C_CKV0, C_KR0 = 0, 512, 1024, 1536, 1920, 2176

NT_DIMS = (((1,), (1,)), ((), ()))
LOG2E = math.log2(math.e)
LN2 = math.log(2.0)


def _cparams(sem):
    return pltpu.CompilerParams(dimension_semantics=sem, vmem_limit_bytes=VMEM_LIMIT)


def _full(a):
    return pl.BlockSpec(a.shape, lambda *_: (0,) * a.ndim)


def _rms(x, n):
    return x * lax.rsqrt(jnp.sum(x * x, axis=-1, keepdims=True) * (1.0 / n) + EPS)


def _residue_rows(r):
    return pl.ds(r, M_STEP, stride=N_RES)


def _inproj_kernel(x_ref, g_ref, w_ref, gsum_ref, gq_ref, gk_ref, gcq_ref, gckv_ref,
                   qa_ref, ka_ref, va_ref, cq_ref, ckv_ref, kr_ref, perm_scr, xn_scr):
    n_chunks = A_WIDTH // LANES
    for s in range(IN_TILES):
        xn_scr[s] = (_rms(x_ref[0, s * IN_ROWS:(s + 1) * IN_ROWS], D_MODEL) * g_ref[...]).astype(BF16)

    def head_norm(t, gain_ref):
        ss = jnp.dot((t * t).astype(BF16), gsum_ref[...], preferred_element_type=F32)
        return t * lax.rsqrt(ss * (1.0 / HEAD_DIM_A) + EPS) * gain_ref[...]

    m_tile = IN_ROWS // N_RES
    for s in range(IN_TILES):
        rows = slice(s * IN_ROWS, (s + 1) * IN_ROWS)
        proj = lambda c0, c1, s=s: jnp.dot(xn_scr[s], w_ref[:, c0:c1], preferred_element_type=F32)

        def put_residue(ref, val, t, s=s):
            for c in range(n_chunks):
                perm_scr[s, t, c] = val[:, c * LANES:(c + 1) * LANES]
            for r in range(N_RES):
                for c in range(n_chunks):
                    ref[0, r, s * m_tile:(s + 1) * m_tile, c * LANES:(c + 1) * LANES] = (
                        perm_scr[s, t, c, pl.ds(r, m_tile, stride=N_RES), :].astype(ref.dtype))

        put_residue(qa_ref, head_norm(proj(C_Q0, C_K0), gq_ref), 0)
        put_residue(ka_ref, head_norm(proj(C_K0, C_V0), gk_ref), 1)
        put_residue(va_ref, proj(C_V0, C_CQ0), 2)
        cq_ref[0, rows] = (_rms(proj(C_CQ0, C_CKV0), Q_LORA) * gcq_ref[...]).astype(BF16)
        ckv_ref[0, rows] = (_rms(proj(C_CKV0, C_KR0), KV_LORA) * gckv_ref[...]).astype(BF16)
        kr_ref[0, rows] = proj(C_KR0, IN_COLS_PAD)


def _inproj(x, b0, B, g, w_in_p, gsum, gq, gk, gcq, gckv):
    step_rows = IN_TILES * IN_ROWS
    res = lambda: (jax.ShapeDtypeStruct((B, N_RES, M_SUB, A_WIDTH), BF16),
                   pl.BlockSpec((1, N_RES, step_rows // N_RES, A_WIDTH), lambda b, j: (b, 0, j, 0)))
    nat = lambda c, dt: (jax.ShapeDtypeStruct((B, SEQ, c), dt),
                         pl.BlockSpec((1, step_rows, c), lambda b, j: (b, j, 0)))
    outs = [res(), res(), res(), nat(Q_LORA, BF16), nat(KV_LORA, BF16), nat(LANES, F32)]
    return pl.pallas_call(
        _inproj_kernel,
        grid=(B, SEQ // step_rows),
        in_specs=[pl.BlockSpec((1, step_rows, D_MODEL), lambda b, j: (b + b0, j, 0)),
                  _full(g), _full(w_in_p), _full(gsum), _full(gq), _full(gk), _full(gcq), _full(gckv)],
        out_specs=[o[1] for o in outs],
        out_shape=[o[0] for o in outs],
        scratch_shapes=[pltpu.VMEM((IN_TILES, 3, A_WIDTH // LANES, IN_ROWS, LANES), F32),
                        pltpu.VMEM((IN_TILES, IN_ROWS, D_MODEL), BF16)],
        compiler_params=_cparams(("parallel", "parallel")),
        name="inproj",
    )(x, g, w_in_p, gsum, gq, gk, gcq, gckv)


def _mla_prep_kernel(cq_ref, ckv_ref, kr_ref, wq_ref, wkv_ref, gsq_ref, gsk_ref, icq_ref,
                     gq_ref, gk_ref, gkr_ref, cos_ref, sa_ref, sb_ref, q_ref, k_ref, v_ref):
    sub = TM // SUB_TILES
    lane = lax.broadcasted_iota(jnp.int32, (sub, LANES), 1)

    for t in range(SUB_TILES):
        rows = slice(t * sub, (t + 1) * sub)
        cos, sa, sb = cos_ref[rows, :], sa_ref[rows, :], sb_ref[rows, :]

        def rope(x, cos=cos, sa=sa, sb=sb):
            return x * cos + pltpu.roll(x, 16, 1) * sa + pltpu.roll(x, LANES - 16, 1) * sb

        q = jnp.dot(cq_ref[0, rows, :], wq_ref[...], preferred_element_type=F32)
        kv = jnp.dot(ckv_ref[0, rows, :], wkv_ref[...], preferred_element_type=F32)
        k_rope = rope(_rms(kr_ref[0, rows, :], ROPE_DIM) * gkr_ref[...])
        for h in range(N_HEADS_M):
            sl = slice(h * HEAD_SLOT, (h + 1) * HEAD_SLOT)
            qh = q[:, sl]
            ss = jnp.dot((qh * qh).astype(BF16), gsq_ref[...], preferred_element_type=F32)
            q_ref[0, rows, sl] = rope(qh * lax.rsqrt(ss * icq_ref[...] + EPS) * gq_ref[...]).astype(BF16)
            kh = kv[:, sl]
            ssk = jnp.dot((kh * kh).astype(BF16), gsk_ref[...], preferred_element_type=F32)
            kn = kh * lax.rsqrt(ssk * (1.0 / NOPE_DIM) + EPS) * gk_ref[...] + k_rope
            k_ref[0, rows, sl] = kn.astype(BF16)
        for hp in range(N_HEADS_M // 2):
            v_pair = kv[:, N_HEADS_M * HEAD_SLOT + hp * LANES:N_HEADS_M * HEAD_SLOT + (hp + 1) * LANES]
            v_ref[0, rows, (2 * hp) * HEAD_SLOT:(2 * hp + 1) * HEAD_SLOT] = jnp.where(lane < V_DIM, v_pair, 1.0).astype(BF16)
            v_ref[0, rows, (2 * hp + 1) * HEAD_SLOT:(2 * hp + 2) * HEAD_SLOT] = jnp.where(lane < V_DIM, 1.0, v_pair).astype(BF16)


def _mla_prep(cq, ckv, kr, wq_p, wkv_p, gsq, gsk, icq, gq, gk, gkr, cos, sa, sb):
    B = cq.shape[0]
    rows = lambda c: pl.BlockSpec((1, TM, c), lambda b, j: (b, j, 0))
    tab = pl.BlockSpec((TM, LANES), lambda b, j: (j, 0))
    wide = N_HEADS_M * HEAD_SLOT
    return pl.pallas_call(
        _mla_prep_kernel,
        grid=(B, N_ROW_STEPS),
        in_specs=[rows(Q_LORA), rows(KV_LORA), rows(LANES), _full(wq_p), _full(wkv_p), _full(gsq),
                  _full(gsk), _full(icq), _full(gq), _full(gk), _full(gkr), tab, tab, tab],
        out_specs=[rows(wide)] * 3,
        out_shape=[jax.ShapeDtypeStruct((B, SEQ, wide), BF16)] * 3,
        compiler_params=_cparams(("parallel", "parallel")),
        name="mla_prep",
    )(cq, ckv, kr, wq_p, wkv_p, gsq, gsk, icq, gq, gk, gkr, cos, sa, sb)


def _mla_attn_kernel(q_ref, k_ref, v_ref, o_ref):
    n_q = SEQ // MLA_TQ
    row = lax.broadcasted_iota(jnp.int32, (MLA_TQ, MLA_TQ), 0)
    col = lax.broadcasted_iota(jnp.int32, (MLA_TQ, MLA_TQ), 1)
    lane = lax.broadcasted_iota(jnp.int32, (MLA_TQ, LANES), 1)
    heads = [slice(hh * HEAD_SLOT, (hh + 1) * HEAD_SLOT) for hh in range(2)]

    def probs(i, hh):
        n_keys = (i + 1) * MLA_TQ
        s = lax.dot_general(q_ref[0, i * MLA_TQ:n_keys, heads[hh]], k_ref[0, 0:n_keys, heads[hh]],
                            NT_DIMS, preferred_element_type=F32)
        diag = jnp.where(col <= row, s[:, n_keys - MLA_TQ:], NEG)
        s = diag if i == 0 else jnp.concatenate([s[:, :n_keys - MLA_TQ], diag], axis=1)
        return jnp.exp2(s - jnp.max(s, axis=-1, keepdims=True)).astype(BF16)

    def values(i, hh, p):
        return jnp.dot(p, v_ref[0, 0:(i + 1) * MLA_TQ, heads[hh]], preferred_element_type=F32)

    units = [(i, hh) for i in range(n_q) for hh in range(2)]
    acc, pending = {}, []
    for u in units:
        pending.append((u, probs(*u)))
        if len(pending) > MLA_SKEW:
            done, p = pending.pop(0)
            acc[done] = values(*done, p)
    for done, p in pending:
        acc[done] = values(*done, p)
    for i in range(n_q):
        num = jnp.where(lane < V_DIM, acc[(i, 0)], acc[(i, 1)])
        den = pltpu.roll(jnp.where(lane < V_DIM, acc[(i, 1)], acc[(i, 0)]), V_DIM, 1)
        o_ref[0, i * MLA_TQ:(i + 1) * MLA_TQ, :] = (num / den).astype(BF16)


def _mla_attn(q, k, v):
    B = q.shape[0]
    pair = lambda c: pl.BlockSpec((1, SEQ, c), lambda b, h: (b, 0, h))
    return pl.pallas_call(
        _mla_attn_kernel,
        grid=(B, N_HEADS_M // 2),
        in_specs=[pair(2 * HEAD_SLOT)] * 3,
        out_specs=pair(2 * V_DIM),
        out_shape=jax.ShapeDtypeStruct((B, SEQ, M_WIDTH), BF16),
        compiler_params=_cparams(("parallel", "parallel")),
        name="mla_attn",
    )(q, k, v)


N_BLK = SEQ // BAND_BLOCK


def _dilated_kernel(q_ref, k_ref, v_ref, t1_ref, t2_ref, t3_ref, ones_ref, o_ref,
                    q1, k1, v1, q2, k2, v2, ob1, ls1, ob2, ls2, ob3, ls3):
    bb = BAND_BLOCK
    lane = lax.broadcasted_iota(jnp.int32, (bb, LANES), 1)
    lane_row = lax.broadcasted_iota(jnp.int32, (1, LANES), 1)
    own = [(lane_row < HEAD_DIM_A).astype(BF16), (lane_row >= HEAD_DIM_A).astype(BF16)]
    rows = lambda lo, hi: slice(lo * bb, hi * bb)

    for src, d1, d2 in ((q_ref, q1, q2), (k_ref, k1, k2), (v_ref, v1, v2)):
        for n2 in range(N_BLK // 2):
            pieces = [src[0, r, 16 * n2:16 * n2 + 16, :].astype(F32) for r in range(N_RES)]
            d1[rows(2 * n2, 2 * n2 + 1), :] = jnp.concatenate([p[0:8] for p in pieces], axis=0).astype(BF16)
            d1[rows(2 * n2 + 1, 2 * n2 + 2), :] = jnp.concatenate([p[8:16] for p in pieces], axis=0).astype(BF16)
        for r4 in range(4):
            for n in range(4):
                d2[rows(r4 * 4 + n, r4 * 4 + n + 1), :] = jnp.concatenate(
                    [src[0, r4 + 4 * c, 32 * n:32 * n + 32, :] for c in range(4)], axis=0)

    blocks = []
    for idx in range(N_BLK):
        lo = idx - 1 if idx > 0 else idx
        blocks.append((q1, k1, v1, rows(lo, idx + 1), t1_ref, ob1, ls1, idx))
    for idx in range(N_BLK):
        lo = idx - 1 if idx % 4 else idx
        blocks.append((q2, k2, v2, rows(lo, idx + 1), t2_ref, ob2, ls2, idx))
    for r in range(N_RES):
        blocks.append((None, None, None, r, t3_ref, ob3, ls3, r))

    def scores_and_probs(blk, hh):
        qd, kd, _, kv_rows, t_ref, _, _, idx = blk
        q = qd[rows(idx, idx + 1), :] if qd is not None else q_ref[0, idx]
        keys = kd[kv_rows, :] if kd is not None else k_ref[0, kv_rows]
        n_keys = keys.shape[0]
        s = lax.dot_general(q * own[hh], keys, NT_DIMS, preferred_element_type=F32)
        s = s + t_ref[hh, :, 2 * bb - n_keys:2 * bb]
        m = jnp.max(s, axis=-1, keepdims=True)
        return m, jnp.exp2(s - m).astype(BF16)

    def finish(blk, m, e):
        _, _, vd, kv_rows, _, o_dst, l_dst, idx = blk
        vals = vd[kv_rows, :] if vd is not None else v_ref[0, kv_rows]
        n_keys = vals.shape[0]
        num = (jnp.dot(e[0], vals * own[0], preferred_element_type=F32)
               + jnp.dot(e[1], vals * own[1], preferred_element_type=F32))
        den = (jnp.dot(e[0], ones_ref[0, 0:n_keys, :], preferred_element_type=F32)
               + jnp.dot(e[1], ones_ref[1, 0:n_keys, :], preferred_element_type=F32))
        o_dst[idx] = num / den
        l_dst[idx] = jnp.where(lane < HEAD_DIM_A, m[0], m[1]) * LN2 + jnp.log(den)

    stage = {}
    for t in range(len(blocks) + ATTN_SKEW):
        if t < len(blocks):
            stage[t] = [scores_and_probs(blocks[t], hh) for hh in range(2)]
        d = t - ATTN_SKEW
        if d >= 0:
            finish(blocks[d], [stage[d][hh][0] for hh in range(2)], [stage[d][hh][1] for hh in range(2)])
            del stage[d]

    for r in range(N_RES):
        r4, c = r % 4, r // 4
        gather1 = lambda ref: jnp.concatenate([ref[n, 8 * r:8 * r + 8, :] for n in range(N_BLK)], axis=0)
        gather2 = lambda ref: jnp.concatenate(
            [ref[r4 * 4 + n, 32 * c:32 * c + 32, :] for n in range(4)], axis=0)
        o_b = [gather1(ob1), gather2(ob2), ob3[r]]
        l_b = [gather1(ls1), gather2(ls2), ls3[r]]
        top = jnp.maximum(jnp.maximum(l_b[0], l_b[1]), l_b[2])
        w_b = [jnp.exp(l - top) for l in l_b]
        num = w_b[0] * o_b[0] + w_b[1] * o_b[1] + w_b[2] * o_b[2]
        o_ref[0, r] = (num / (w_b[0] + w_b[1] + w_b[2])).astype(BF16)


def _dilated(qa, ka, va, t1, t2, t3):
    B = qa.shape[0]
    head_lanes = np.arange(LANES)[None, None, :] // HEAD_DIM_A == np.arange(2)[:, None, None]
    ones = jnp.asarray(np.broadcast_to(head_lanes, (2, 2 * BAND_BLOCK, LANES)).astype(np.float32), BF16)
    blk = pl.BlockSpec((1, N_RES, M_SUB, LANES), lambda b, h: (b, 0, 0, h))
    tab = lambda t: pl.BlockSpec((2,) + t.shape[1:], lambda b, h: (h, 0, 0))
    blocked_bf16 = pltpu.VMEM((N_BLK * BAND_BLOCK, LANES), BF16)
    blocked_f32 = pltpu.VMEM((N_BLK, BAND_BLOCK, LANES), F32)
    return pl.pallas_call(
        _dilated_kernel,
        grid=(B, N_HEADS_A // 2),
        in_specs=[blk, blk, blk, tab(t1), tab(t2), tab(t3), _full(ones)],
        out_specs=blk,
        out_shape=jax.ShapeDtypeStruct((B, N_RES, M_SUB, A_WIDTH), BF16),
        scratch_shapes=[blocked_bf16] * 6 + [blocked_f32] * 6,
        compiler_params=_cparams(("parallel", "parallel")),
        name="dilated_attn",
    )(qa, ka, va, t1, t2, t3, ones)


R_W1, R_W2, R_E1, R_E2, R_RANK1, R_RANK2 = range(6)
ROUTE_ROWS = 8
ROUTE_GROUP_ROWS = 8
ROUTE_EXPERT_ROW0 = 8


def _outproj_kernel(oa_ref, om_ref, x_ref, wo_ref, g_ref, wrh_ref, wrl_ref, br_ref, tri_ref,
                    h_ref, xn_ref, route_ref, cnt_ref, perm_scr, carry_scr):
    @pl.when((pl.program_id(0) == 0) & (pl.program_id(1) == 0))
    def _():
        carry_scr[...] = jnp.zeros(carry_scr.shape, F32)

    n_chunks = A_WIDTH // LANES
    for r in range(N_RES):
        for c in range(n_chunks):
            perm_scr[c, _residue_rows(r), :] = oa_ref[0, r, :, c * LANES:(c + 1) * LANES].astype(F32)
    oa = jnp.concatenate([perm_scr[c] for c in range(n_chunks)], axis=1).astype(BF16)
    h = (x_ref[0] + jnp.dot(oa, wo_ref[0:A_WIDTH, :], preferred_element_type=F32)
         + jnp.dot(om_ref[0], wo_ref[A_WIDTH:, :], preferred_element_type=F32))
    h_ref[0] = h
    xn = _rms(h, D_MODEL) * g_ref[...]
    xn_ref[0] = xn.astype(BF16)
    hi = xn.astype(BF16)
    lo = (xn - hi.astype(F32)).astype(BF16)
    lg = (lax.dot_general(wrh_ref[...], hi, NT_DIMS, preferred_element_type=F32)
          + lax.dot_general(wrh_ref[...], lo, NT_DIMS, preferred_element_type=F32)
          + lax.dot_general(wrl_ref[...], hi, NT_DIMS, preferred_element_type=F32)) + br_ref[...]

    sub = lax.broadcasted_iota(jnp.int32, (EXPERTS_PER_GROUP, TM), 0).astype(F32)
    cmax = lambda t: jnp.max(t, axis=0, keepdims=True)
    cmin = lambda t: jnp.min(t, axis=0, keepdims=True)
    csum = lambda t: jnp.sum(t, axis=0, keepdims=True)
    none = float(EXPERTS_PER_GROUP)

    gl = lg[0:ROUTE_GROUP_ROWS]
    ge = jnp.exp(gl - cmax(gl))
    gsum = csum(ge)
    g_gate = 1.0 / gsum
    g_idx = cmin(jnp.where(ge / gsum == g_gate, sub, none))
    el = lg[ROUTE_EXPERT_ROW0:ROUTE_EXPERT_ROW0 + EXPERTS_PER_GROUP]
    for g in range(1, N_GROUPS):
        r0 = ROUTE_EXPERT_ROW0 + g * EXPERTS_PER_GROUP
        el = jnp.where(g_idx == float(g), lg[r0:r0 + EXPERTS_PER_GROUP], el)
    ee = jnp.exp(el - cmax(el))
    esum = csum(ee)
    eprob = ee / esum
    p1 = 1.0 / esum
    i1 = cmin(jnp.where(eprob == p1, sub, none))
    rest = jnp.where(sub == i1, -1.0, eprob)
    p2 = cmax(rest)
    i2 = cmin(jnp.where(rest == p2, sub, none))
    den = p1 + p2
    e1 = g_idx * EXPERTS_PER_GROUP + i1
    e2 = g_idx * EXPERTS_PER_GROUP + i2
    erow = lax.broadcasted_iota(jnp.int32, (N_EXPERTS, TM), 0).astype(F32)
    onehot = ((erow == e1) | (erow == e2)).astype(F32)
    before = jnp.dot(onehot.astype(BF16), tri_ref[...], preferred_element_type=F32) + carry_scr[:, 0:1]
    rank1 = csum(jnp.where(erow == e1, before, 0.0))
    rank2 = csum(jnp.where(erow == e2, before, 0.0))
    carry_scr[...] = carry_scr[...] + jnp.sum(onehot, axis=1, keepdims=True)
    cnt_ref[...] = carry_scr[...]

    record = jnp.zeros((ROUTE_ROWS, TM), F32)
    for i, val in ((R_W1, g_gate * (p1 / den)), (R_W2, g_gate * (p2 / den)), (R_E1, e1), (R_E2, e2),
                   (R_RANK1, rank1), (R_RANK2, rank2)):
        record = jnp.where(sub == float(i), val, record)
    route_ref[0] = record


def _outproj(oa, om, x, b0, wo, g, wrh, wrl, br, tri):
    B = oa.shape[0]
    rows = lambda c: pl.BlockSpec((1, TM, c), lambda b, j: (b, j, 0))
    shp = lambda c, dt: jax.ShapeDtypeStruct((B, SEQ, c), dt)
    return pl.pallas_call(
        _outproj_kernel,
        grid=(B, N_ROW_STEPS),
        in_specs=[pl.BlockSpec((1, N_RES, M_STEP, A_WIDTH), lambda b, j: (b, 0, j, 0)),
                  rows(M_WIDTH), pl.BlockSpec((1, TM, D_MODEL), lambda b, j: (b + b0, j, 0)),
                  _full(wo), _full(g), _full(wrh), _full(wrl), _full(br), _full(tri)],
        out_specs=[rows(D_MODEL), rows(D_MODEL),
                   pl.BlockSpec((1, ROUTE_ROWS, TM), lambda b, j: (b * N_ROW_STEPS + j, 0, 0)),
                   pl.BlockSpec((N_EXPERTS, LANES), lambda b, j: (0, 0))],
        out_shape=[shp(D_MODEL, F32), shp(D_MODEL, BF16),
                   jax.ShapeDtypeStruct((B * N_ROW_STEPS, ROUTE_ROWS, TM), F32),
                   jax.ShapeDtypeStruct((N_EXPERTS, LANES), F32)],
        scratch_shapes=[pltpu.VMEM((A_WIDTH // LANES, TM, LANES), F32), pltpu.VMEM((N_EXPERTS, LANES), F32)],
        compiler_params=_cparams(("arbitrary", "arbitrary")),
        name="outproj_router",
    )(oa, om, x, wo, g, wrh, wrl, br, tri)


def _expert_kernel(blk_e_ref, n_used_ref, x_ref, wg_ref, wu_ref, wd_ref, y_ref, wg_s, wu_s, wd_s):
    i = pl.program_id(0)

    @pl.when(i < n_used_ref[0])
    def _():
        @pl.when((i == 0) | (blk_e_ref[i] != blk_e_ref[jnp.maximum(i - 1, 0)]))
        def _():
            wg_s[...] = wg_ref[0].astype(BF16)
            wu_s[...] = wu_ref[0].astype(BF16)
            wd_s[...] = wd_ref[0].astype(BF16)

        half = MOE_TM // SUB_TILES
        gate_up = []
        for s in range(SUB_TILES):
            x = x_ref[s * half:(s + 1) * half, :]
            gate_up.append((jnp.dot(x, wg_s[...], preferred_element_type=F32),
                            jnp.dot(x, wu_s[...], preferred_element_type=F32)))
        for s, (gate, up) in enumerate(gate_up):
            hdn = (gate * jax.nn.sigmoid(gate) * up).astype(BF16)
            y_ref[s * half:(s + 1) * half, :] = jnp.dot(hdn, wd_s[...], preferred_element_type=F32).astype(y_ref.dtype)

    @pl.when(i >= n_used_ref[0])
    def _():
        y_ref[...] = jnp.zeros(y_ref.shape, y_ref.dtype)


def _experts(blk_e, n_used, xs, wg, wu, wd):
    n_blocks = xs.shape[0] // MOE_TM
    row_in = lambda i, be, nu: (jnp.minimum(i, nu[0] - 1), 0)
    wsel = lambda i, be, nu: (be[i], 0, 0)
    return pl.pallas_call(
        _expert_kernel,
        grid_spec=pltpu.PrefetchScalarGridSpec(
            num_scalar_prefetch=2,
            grid=(n_blocks,),
            in_specs=[pl.BlockSpec((MOE_TM, D_MODEL), row_in),
                      pl.BlockSpec((1, D_MODEL, EXPERT_FF), wsel),
                      pl.BlockSpec((1, D_MODEL, EXPERT_FF), wsel),
                      pl.BlockSpec((1, EXPERT_FF, D_MODEL), wsel)],
            out_specs=pl.BlockSpec((MOE_TM, D_MODEL), lambda i, be, nu: (i, 0)),
            scratch_shapes=[pltpu.VMEM((D_MODEL, EXPERT_FF), BF16), pltpu.VMEM((D_MODEL, EXPERT_FF), BF16),
                            pltpu.VMEM((EXPERT_FF, D_MODEL), BF16)]),
        out_shape=jax.ShapeDtypeStruct(xs.shape, BF16),
        compiler_params=_cparams(("arbitrary",)),
        name="expert_ffn",
    )(blk_e, n_used, xs, wg, wu, wd)


def _ple_kernel(n_parts, part_b, *refs):
    parts = [refs[4 * i:4 * i + 4] for i in range(n_parts)]
    p_ref, wp_ref, gp_ref, wg_ref, bg_ref, o_ref = refs[4 * n_parts:]

    def combine_and_gate(h_ref, y1_ref, y2_ref, route_ref):
        rec = jnp.concatenate([route_ref[0], jnp.zeros((LANES - ROUTE_ROWS, TM), F32)], axis=0)
        route = jnp.concatenate([rec[:, c * LANES:(c + 1) * LANES].T for c in range(TM // LANES)], axis=0)
        h = (h_ref[0] + route[:, R_W1:R_W1 + 1] * y1_ref[0].astype(F32)
             + route[:, R_W2:R_W2 + 1] * y2_ref[0].astype(F32))
        e = _rms(jnp.dot(p_ref[0].astype(BF16), wp_ref[...], preferred_element_type=F32), D_MODEL) * gp_ref[...]
        g = jax.nn.sigmoid(jnp.dot(h.astype(BF16), wg_ref[...], preferred_element_type=F32) + bg_ref[...])
        o_ref[0] = h + g * e

    for i, part in enumerate(parts):
        pl.when(pl.program_id(0) // part_b == i)(functools.partial(combine_and_gate, *part))


def _ple(parts, p, wp, gp, wg, bg):
    n_parts, part_b = len(parts), parts[0][0].shape[0]
    rows = lambda c: pl.BlockSpec((1, TM, c), lambda b, j: (b, j, 0))
    in_specs, args = [], []
    for i, (h, y_tok, route) in enumerate(parts):
        local = lambda b, i=i: jnp.clip(b - i * part_b, 0, part_b - 1)
        step = lambda b, j, local=local: local(b) * N_ROW_STEPS + j
        in_specs += [pl.BlockSpec((1, TM, D_MODEL), lambda b, j, local=local: (local(b), j, 0)),
                     pl.BlockSpec((1, TM, D_MODEL), lambda b, j, step=step: (0, step(b, j), 0)),
                     pl.BlockSpec((1, TM, D_MODEL), lambda b, j, step=step: (1, step(b, j), 0)),
                     pl.BlockSpec((1, ROUTE_ROWS, TM), lambda b, j, step=step: (step(b, j), 0, 0))]
        args += [h, y_tok, y_tok, route]
    return pl.pallas_call(
        functools.partial(_ple_kernel, n_parts, part_b),
        grid=(n_parts * part_b, N_ROW_STEPS),
        in_specs=in_specs + [rows(PLE_DIM), _full(wp), _full(gp), _full(wg), _full(bg)],
        out_specs=rows(D_MODEL),
        out_shape=jax.ShapeDtypeStruct((n_parts * part_b, SEQ, D_MODEL), F32),
        compiler_params=_cparams(("parallel", "parallel")),
        name="ple_gate",
    )(*args, p, wp, gp, wg, bg)


def _t5_bucket(dist):
    max_exact = REL_BUCKETS // 2
    n = jnp.maximum(dist, 0)
    nf = jnp.maximum(n, 1).astype(F32)
    large = max_exact + (jnp.log(nf / max_exact) / math.log(REL_MAX_DISTANCE / max_exact)
                         * (REL_BUCKETS - max_exact)).astype(jnp.int32)
    large = jnp.minimum(large, REL_BUCKETS - 1)
    return jnp.where(n < max_exact, n, large)


def _bias_table(rel_bias, local_index, dilation, with_prev):
    loc = np.asarray(local_index)
    delta = loc[:, None] - loc[None, :]
    if with_prev:
        delta = np.concatenate([delta + BAND_BLOCK, delta], axis=1)
    ok = (delta >= 0) & (delta <= BAND_BLOCK)
    bucket = _t5_bucket(jnp.asarray(delta * dilation, jnp.int32))
    picked = jnp.where(bucket[None, :, :, None] == jnp.arange(REL_BUCKETS, dtype=jnp.int32),
                       rel_bias.astype(F32).T[:, None, None, :], 0.0)
    return jnp.where(jnp.asarray(ok)[None], jnp.sum(picked, axis=-1) * LOG2E, NEG)


def _block_diag_ones(sizes, total):
    g = np.zeros((total, total), np.float32)
    o = 0
    for s, on in sizes:
        if on:
            g[o:o + s, o:o + s] = 1.0
        o += s
    return jnp.asarray(g, BF16)


def _dispatch_plan(route, counts, n_tokens):
    field = lambda i: route[:, i, :].reshape(n_tokens).astype(jnp.int32)
    e = jnp.stack([field(R_E1), field(R_E2)], axis=-1)
    rank = jnp.stack([field(R_RANK1), field(R_RANK2)], axis=-1)
    pcounts = (counts + MOE_TM - 1) // MOE_TM * MOE_TM
    pend = jnp.cumsum(pcounts)
    pstart = pend - pcounts
    ids = jnp.arange(N_EXPERTS, dtype=jnp.int32)
    pos = rank + jnp.sum(jnp.where(e[..., None] == ids, pstart, 0), axis=-1)
    n_assign = n_tokens * TOP_K
    shift = int(math.ceil(math.log2(n_assign)))
    keys = (e.reshape(-1) << shift) | jnp.arange(n_assign, dtype=jnp.int32)
    tok_sorted = (jnp.sort(keys) & ((1 << shift) - 1)) // TOP_K
    n_rows = n_assign + N_EXPERTS * MOE_TM
    rows = jnp.arange(n_rows, dtype=jnp.int32)
    row_e = jnp.sum(rows[:, None] >= pend[None, :], axis=-1)
    row_e = jnp.minimum(row_e, N_EXPERTS - 1)
    pick = lambda tbl: jnp.sum(jnp.where(row_e[:, None] == ids, tbl, 0), axis=-1)
    within = rows - pick(pstart)
    src = jnp.clip(pick(jnp.cumsum(counts) - counts) + within, 0, n_assign - 1)
    row_tok = jnp.where(within < pick(counts), tok_sorted[src], rows % n_tokens)
    n_blocks = n_rows // MOE_TM
    n_used = (pend[-1] // MOE_TM).astype(jnp.int32)
    blk_start = jnp.minimum(jnp.arange(n_blocks, dtype=jnp.int32), n_used - 1) * MOE_TM
    blk_e = jnp.minimum(jnp.sum(blk_start[:, None] >= pend[None, :], axis=-1), N_EXPERTS - 1).astype(jnp.int32)
    return row_tok, pos, blk_e, n_used.reshape(1)


def kernel(x, p, rel_bias, norm_mix_gain, w_in, qn_a_gain, kn_a_gain, q_a_gain, w_q_up, kv_a_gain, w_kv_up, qn_nope_gain, qn_rope_gain, kn_nope_gain, kn_rope_gain, w_out, norm_ffn_gain, w_router_group, b_router_group, w_router_expert, b_router_expert, w_exp_gate, w_exp_up, w_exp_down, w_ple_proj, ple_norm_gain, w_ple_gate, b_ple_gate):
    B, S, D = x.shape
    assert (S, D) == (SEQ, D_MODEL) and p.shape[0] == 1
    row = lambda a: a.reshape(1, -1).astype(F32)
    zeros = lambda *s: jnp.zeros(s, F32)

    w_in_p = jnp.concatenate([w_in[0, :, :C_KR0], zeros(D, 64), w_in[0, :, C_KR0:], zeros(D, 32)],
                             axis=1).astype(BF16)
    gsum_a = _block_diag_ones([(HEAD_DIM_A, True)] * N_HEADS_A, A_WIDTH)
    gq_a = row(jnp.tile(qn_a_gain[0], N_HEADS_A)) * (HEAD_DIM_A ** -0.5 * LOG2E)
    gk_a = row(jnp.tile(kn_a_gain[0], N_HEADS_A))

    wq_p = jnp.pad(w_q_up[0].reshape(Q_LORA, N_HEADS_M, NOPE_DIM + ROPE_DIM),
                   ((0, 0), (0, 0), (0, HEAD_SLOT - NOPE_DIM - ROPE_DIM))).reshape(Q_LORA, -1).astype(BF16)
    wkv = w_kv_up[0].reshape(KV_LORA, N_HEADS_M, NOPE_DIM + V_DIM)
    wk_p = jnp.pad(wkv[..., :NOPE_DIM], ((0, 0), (0, 0), (0, HEAD_SLOT - NOPE_DIM))).reshape(KV_LORA, -1)
    wkv_p = jnp.concatenate([wk_p, wkv[..., NOPE_DIM:].reshape(KV_LORA, -1)], axis=1).astype(BF16)
    gs_q = _block_diag_ones([(NOPE_DIM, True), (ROPE_DIM, True), (32, False)], HEAD_SLOT)
    gs_k = _block_diag_ones([(NOPE_DIM, True), (64, False)], HEAD_SLOT)
    inv_cnt_q = jnp.asarray(np.concatenate([np.full(64, 1 / NOPE_DIM), np.full(32, 1 / ROPE_DIM),
                                            np.ones(32)]).astype(np.float32)).reshape(1, HEAD_SLOT)
    mla_scale = (NOPE_DIM + ROPE_DIM) ** -0.5 * LOG2E
    gq_m = row(jnp.concatenate([qn_nope_gain[0], qn_rope_gain[0], zeros(32)])) * mla_scale
    gk_m = row(jnp.concatenate([kn_nope_gain[0], zeros(64)]))
    gkr_m = row(jnp.concatenate([zeros(64), kn_rope_gain[0], zeros(32)]))

    half = ROPE_DIM // 2
    inv = 1.0 / (ROPE_THETA ** (jnp.arange(half, dtype=F32) * 2.0 / ROPE_DIM))
    ang = jnp.arange(S, dtype=jnp.int32).astype(F32)[:, None] * inv[None, :]
    cosv, sinv = jnp.cos(ang), jnp.sin(ang)
    cos_t = jnp.concatenate([jnp.ones((S, 64), F32), cosv, cosv, jnp.ones((S, 32), F32)], -1)
    sin_a = jnp.concatenate([zeros(S, 80), sinv, zeros(S, 32)], -1)
    sin_b = jnp.concatenate([zeros(S, 64), -sinv, zeros(S, 48)], -1)

    loc1 = [16 * a + r for r in range(16) for a in range(8)]
    loc2 = [4 * a + c for c in range(4) for a in range(32)]
    loc3 = list(range(BAND_BLOCK))
    t1 = _bias_table(rel_bias, loc1, 1, True)
    t2 = _bias_table(rel_bias, loc2, 4, True)
    t3 = jnp.concatenate([jnp.full((N_HEADS_A, BAND_BLOCK, BAND_BLOCK), NEG, F32),
                          _bias_table(rel_bias, loc3, 16, False)], axis=-1)

    n_pad_g, n_pad_e = ROUTE_GROUP_ROWS - N_GROUPS, LANES - ROUTE_EXPERT_ROW0 - N_EXPERTS
    w_r = jnp.concatenate([w_router_group[0].T, zeros(n_pad_g, D), w_router_expert[0].T, zeros(n_pad_e, D)], axis=0)
    w_r_hi = w_r.astype(BF16)
    w_r_lo = (w_r - w_r_hi.astype(F32)).astype(BF16)
    b_r = jnp.concatenate([b_router_group[0], jnp.full((n_pad_g,), NEG, F32), b_router_expert[0],
                           zeros(n_pad_e)]).reshape(LANES, 1)
    tri = jnp.asarray(np.triu(np.ones((TM, TM), np.float32), 1), BF16)

    w_out_b = w_out[0].astype(BF16)
    part_b = B // BATCH_PARTS
    n_tokens = part_b * S
    parts = []
    for b0 in range(0, B, part_b):
        qa, ka, va, cq, ckv, kr = _inproj(x, b0, part_b, row(norm_mix_gain[0]), w_in_p, gsum_a, gq_a, gk_a,
                                          row(q_a_gain[0]), row(kv_a_gain[0]))
        o_a = _dilated(qa, ka, va, t1, t2, t3)
        q_m, k_m, v_m = _mla_prep(cq, ckv, kr, wq_p, wkv_p, gs_q, gs_k, inv_cnt_q, gq_m, gk_m, gkr_m,
                                  cos_t, sin_a, sin_b)
        o_m = _mla_attn(q_m, k_m, v_m)
        h1, xn2, route, cnt = _outproj(o_a, o_m, x, b0, w_out_b, row(norm_ffn_gain[0]),
                                       w_r_hi, w_r_lo, b_r, tri)
        row_tok, pos, blk_e, n_used = _dispatch_plan(route, cnt[:, 0].astype(jnp.int32), n_tokens)
        xs = xn2.reshape(n_tokens, D)[row_tok]
        y = _experts(blk_e, n_used, xs, w_exp_gate[0], w_exp_up[0], w_exp_down[0])
        y_tok = y[pos.T.reshape(-1)].reshape(TOP_K, n_tokens, D)
        parts.append((h1, y_tok, route))
    return _ple(parts, p[0], w_ple_proj[0].astype(BF16), row(ple_norm_gain[0]),
                w_ple_gate[0].astype(BF16), row(b_ple_gate[0]))
```

```python
import functools
import math

import jax
import jax.numpy as jnp
import numpy as np
from jax import lax
from jax.experimental import pallas as pl
from jax.experimental.pallas import tpu as pltpu

F32 = jnp.float32
BF16 = jnp.bfloat16

D_MODEL = 1024
SEQ = 2048
PLE_DIM = 256
EPS = 1e-6
NEG = -1e30
HEAD_DIM_A = 64
A_WIDTH = 512
N_HEADS_A = 8
BAND_BLOCK = 128
REL_BUCKETS = 32
REL_MAX_DISTANCE = 2048
M_WIDTH = 512
V_DIM = 64
N_HEADS_M = 8
Q_LORA = 384
KV_LORA = 256
NOPE_DIM = 64
ROPE_DIM = 32
ROPE_THETA = 10000.0
N_GROUPS = 4
EXPERTS_PER_GROUP = 8
N_EXPERTS = 32
TOP_K = 2
EXPERT_FF = 512

LANES = 128
N_RES = 16
M_SUB = SEQ // N_RES
TM = 1024
N_ROW_STEPS = SEQ // TM
M_STEP = TM // N_RES
HEAD_SLOT = LANES
MLA_TQ = 128
MLA_SKEW = 12
MOE_TM = 512
ATTN_SKEW = 5
BATCH_PARTS = 1
SUB_TILES = 2
IN_ROWS, IN_TILES = 512, 2
VMEM_LIMIT = 48 * 1024 * 1024

IN_COLS_PAD = 3 * A_WIDTH + Q_LORA + KV_LORA + LANES
C_Q0, C_K0, C_V0, C_CQ0, C_CKV0, C_KR0 = 0, 512, 1024, 1536, 1920, 2176

NT_DIMS = (((1,), (1,)), ((), ()))
LOG2E = math.log2(math.e)
LN2 = math.log(2.0)


def _cparams(sem):
    return pltpu.CompilerParams(dimension_semantics=sem, vmem_limit_bytes=VMEM_LIMIT)


def _full(a):
    return pl.BlockSpec(a.shape, lambda *_: (0,) * a.ndim)


def _rms(x, n):
    return x * lax.rsqrt(jnp.sum(x * x, axis=-1, keepdims=True) * (1.0 / n) + EPS)


def _residue_rows(r):
    return pl.ds(r, M_STEP, stride=N_RES)


def _inproj_kernel(x_ref, g_ref, w_ref, gsum_ref, gq_ref, gk_ref, gcq_ref, gckv_ref,
                   qa_ref, ka_ref, va_ref, cq_ref, ckv_ref, kr_ref, perm_scr, xn_scr):
    n_chunks = A_WIDTH // LANES
    for s in range(IN_TILES):
        xn_scr[s] = (_rms(x_ref[0, s * IN_ROWS:(s + 1) * IN_ROWS], D_MODEL) * g_ref[...]).astype(BF16)

    def head_norm(t, gain_ref):
        ss = jnp.dot((t * t).astype(BF16), gsum_ref[...], preferred_element_type=F32)
        return t * lax.rsqrt(ss * (1.0 / HEAD_DIM_A) + EPS) * gain_ref[...]

    m_tile = IN_ROWS // N_RES
    for s in range(IN_TILES):
        rows = slice(s * IN_ROWS, (s + 1) * IN_ROWS)
        proj = lambda c0, c1, s=s: jnp.dot(xn_scr[s], w_ref[:, c0:c1], preferred_element_type=F32)

        def put_residue(ref, val, t, s=s):
            for c in range(n_chunks):
                perm_scr[s, t, c] = val[:, c * LANES:(c + 1) * LANES]
            for r in range(N_RES):
                for c in range(n_chunks):
                    ref[0, r, s * m_tile:(s + 1) * m_tile, c * LANES:(c + 1) * LANES] = (
                        perm_scr[s, t, c, pl.ds(r, m_tile, stride=N_RES), :].astype(ref.dtype))

        put_residue(qa_ref, head_norm(proj(C_Q0, C_K0), gq_ref), 0)
        put_residue(ka_ref, head_norm(proj(C_K0, C_V0), gk_ref), 1)
        put_residue(va_ref, proj(C_V0, C_CQ0), 2)
        cq_ref[0, rows] = (_rms(proj(C_CQ0, C_CKV0), Q_LORA) * gcq_ref[...]).astype(BF16)
        ckv_ref[0, rows] = (_rms(proj(C_CKV0, C_KR0), KV_LORA) * gckv_ref[...]).astype(BF16)
        kr_ref[0, rows] = proj(C_KR0, IN_COLS_PAD)


def _inproj(x, b0, B, g, w_in_p, gsum, gq, gk, gcq, gckv):
    step_rows = IN_TILES * IN_ROWS
    res = lambda: (jax.ShapeDtypeStruct((B, N_RES, M_SUB, A_WIDTH), BF16),
                   pl.BlockSpec((1, N_RES, step_rows // N_RES, A_WIDTH), lambda b, j: (b, 0, j, 0)))
    nat = lambda c, dt: (jax.ShapeDtypeStruct((B, SEQ, c), dt),
                         pl.BlockSpec((1, step_rows, c), lambda b, j: (b, j, 0)))
    outs = [res(), res(), res(), nat(Q_LORA, BF16), nat(KV_LORA, BF16), nat(LANES, F32)]
    return pl.pallas_call(
        _inproj_kernel,
        grid=(B, SEQ // step_rows),
        in_specs=[pl.BlockSpec((1, step_rows, D_MODEL), lambda b, j: (b + b0, j, 0)),
                  _full(g), _full(w_in_p), _full(gsum), _full(gq), _full(gk), _full(gcq), _full(gckv)],
        out_specs=[o[1] for o in outs],
        out_shape=[o[0] for o in outs],
        scratch_shapes=[pltpu.VMEM((IN_TILES, 3, A_WIDTH // LANES, IN_ROWS, LANES), F32),
                        pltpu.VMEM((IN_TILES, IN_ROWS, D_MODEL), BF16)],
        compiler_params=_cparams(("parallel", "parallel")),
        name="inproj",
    )(x, g, w_in_p, gsum, gq, gk, gcq, gckv)


def _mla_prep_kernel(cq_ref, ckv_ref, kr_ref, wq_ref, wkv_ref, gsq_ref, gsk_ref, icq_ref,
                     gq_ref, gk_ref, gkr_ref, cos_ref, sa_ref, sb_ref, q_ref, k_ref, v_ref):
    sub = TM // SUB_TILES
    lane = lax.broadcasted_iota(jnp.int32, (sub, LANES), 1)

    for t in range(SUB_TILES):
        rows = slice(t * sub, (t + 1) * sub)
        cos, sa, sb = cos_ref[rows, :], sa_ref[rows, :], sb_ref[rows, :]

        def rope(x, cos=cos, sa=sa, sb=sb):
            return x * cos + pltpu.roll(x, 16, 1) * sa + pltpu.roll(x, LANES - 16, 1) * sb

        q = jnp.dot(cq_ref[0, rows, :], wq_ref[...], preferred_element_type=F32)
        kv = jnp.dot(ckv_ref[0, rows, :], wkv_ref[...], preferred_element_type=F32)
        k_rope = rope(_rms(kr_ref[0, rows, :], ROPE_DIM) * gkr_ref[...])
        for h in range(N_HEADS_M):
            sl = slice(h * HEAD_SLOT, (h + 1) * HEAD_SLOT)
            qh = q[:, sl]
            ss = jnp.dot((qh * qh).astype(BF16), gsq_ref[...], preferred_element_type=F32)
            q_ref[0, rows, sl] = rope(qh * lax.rsqrt(ss * icq_ref[...] + EPS) * gq_ref[...]).astype(BF16)
            kh = kv[:, sl]
            ssk = jnp.dot((kh * kh).astype(BF16), gsk_ref[...], preferred_element_type=F32)
            kn = kh * lax.rsqrt(ssk * (1.0 / NOPE_DIM) + EPS) * gk_ref[...] + k_rope
            k_ref[0, rows, sl] = kn.astype(BF16)
        for hp in range(N_HEADS_M // 2):
            v_pair = kv[:, N_HEADS_M * HEAD_SLOT + hp * LANES:N_HEADS_M * HEAD_SLOT + (hp + 1) * LANES]
            v_ref[0, rows, (2 * hp) * HEAD_SLOT:(2 * hp + 1) * HEAD_SLOT] = jnp.where(lane < V_DIM, v_pair, 1.0).astype(BF16)
            v_ref[0, rows, (2 * hp + 1) * HEAD_SLOT:(2 * hp + 2) * HEAD_SLOT] = jnp.where(lane < V_DIM, 1.0, v_pair).astype(BF16)


def _mla_prep(cq, ckv, kr, wq_p, wkv_p, gsq, gsk, icq, gq, gk, gkr, cos, sa, sb):
    B = cq.shape[0]
    rows = lambda c: pl.BlockSpec((1, TM, c), lambda b, j: (b, j, 0))
    tab = pl.BlockSpec((TM, LANES), lambda b, j: (j, 0))
    wide = N_HEADS_M * HEAD_SLOT
    return pl.pallas_call(
        _mla_prep_kernel,
        grid=(B, N_ROW_STEPS),
        in_specs=[rows(Q_LORA), rows(KV_LORA), rows(LANES), _full(wq_p), _full(wkv_p), _full(gsq),
                  _full(gsk), _full(icq), _full(gq), _full(gk), _full(gkr), tab, tab, tab],
        out_specs=[rows(wide)] * 3,
        out_shape=[jax.ShapeDtypeStruct((B, SEQ, wide), BF16)] * 3,
        compiler_params=_cparams(("parallel", "parallel")),
        name="mla_prep",
    )(cq, ckv, kr, wq_p, wkv_p, gsq, gsk, icq, gq, gk, gkr, cos, sa, sb)


def _mla_attn_kernel(q_ref, k_ref, v_ref, o_ref):
    n_q = SEQ // MLA_TQ
    row = lax.broadcasted_iota(jnp.int32, (MLA_TQ, MLA_TQ), 0)
    col = lax.broadcasted_iota(jnp.int32, (MLA_TQ, MLA_TQ), 1)
    lane = lax.broadcasted_iota(jnp.int32, (MLA_TQ, LANES), 1)
    heads = [slice(hh * HEAD_SLOT, (hh + 1) * HEAD_SLOT) for hh in range(2)]

    def probs(i, hh):
        n_keys = (i + 1) * MLA_TQ
        s = lax.dot_general(q_ref[0, i * MLA_TQ:n_keys, heads[hh]], k_ref[0, 0:n_keys, heads[hh]],
                            NT_DIMS, preferred_element_type=F32)
        diag = jnp.where(col <= row, s[:, n_keys - MLA_TQ:], NEG)
        s = diag if i == 0 else jnp.concatenate([s[:, :n_keys - MLA_TQ], diag], axis=1)
        return jnp.exp2(s - jnp.max(s, axis=-1, keepdims=True)).astype(BF16)

    def values(i, hh, p):
        return jnp.dot(p, v_ref[0, 0:(i + 1) * MLA_TQ, heads[hh]], preferred_element_type=F32)

    units = [(i, hh) for i in range(n_q) for hh in range(2)]
    acc, pending = {}, []
    for u in units:
        pending.append((u, probs(*u)))
        if len(pending) > MLA_SKEW:
            done, p = pending.pop(0)
            acc[done] = values(*done, p)
    for done, p in pending:
        acc[done] = values(*done, p)
    for i in range(n_q):
        num = jnp.where(lane < V_DIM, acc[(i, 0)], acc[(i, 1)])
        den = pltpu.roll(jnp.where(lane < V_DIM, acc[(i, 1)], acc[(i, 0)]), V_DIM, 1)
        o_ref[0, i * MLA_TQ:(i + 1) * MLA_TQ, :] = (num / den).astype(BF16)


def _mla_attn(q, k, v):
    B = q.shape[0]
    pair = lambda c: pl.BlockSpec((1, SEQ, c), lambda b, h: (b, 0, h))
    return pl.pallas_call(
        _mla_attn_kernel,
        grid=(B, N_HEADS_M // 2),
        in_specs=[pair(2 * HEAD_SLOT)] * 3,
        out_specs=pair(2 * V_DIM),
        out_shape=jax.ShapeDtypeStruct((B, SEQ, M_WIDTH), BF16),
        compiler_params=_cparams(("parallel", "parallel")),
        name="mla_attn",
    )(q, k, v)


N_BLK = SEQ // BAND_BLOCK


def _dilated_kernel(q_ref, k_ref, v_ref, t1_ref, t2_ref, t3_ref, ones_ref, o_ref,
                    q1, k1, v1, q2, k2, v2, ob1, ls1, ob2, ls2, ob3, ls3):
    bb = BAND_BLOCK
    lane = lax.broadcasted_iota(jnp.int32, (bb, LANES), 1)
    lane_row = lax.broadcasted_iota(jnp.int32, (1, LANES), 1)
    own = [(lane_row < HEAD_DIM_A).astype(BF16), (lane_row >= HEAD_DIM_A).astype(BF16)]
    rows = lambda lo, hi: slice(lo * bb, hi * bb)

    for src, d1, d2 in ((q_ref, q1, q2), (k_ref, k1, k2), (v_ref, v1, v2)):
        for n2 in range(N_BLK // 2):
            pieces = [src[0, r, 16 * n2:16 * n2 + 16, :].astype(F32) for r in range(N_RES)]
            d1[rows(2 * n2, 2 * n2 + 1), :] = jnp.concatenate([p[0:8] for p in pieces], axis=0).astype(BF16)
            d1[rows(2 * n2 + 1, 2 * n2 + 2), :] = jnp.concatenate([p[8:16] for p in pieces], axis=0).astype(BF16)
        for r4 in range(4):
            for n in range(4):
                d2[rows(r4 * 4 + n, r4 * 4 + n + 1), :] = jnp.concatenate(
                    [src[0, r4 + 4 * c, 32 * n:32 * n + 32, :] for c in range(4)], axis=0)

    blocks = []
    for idx in range(N_BLK):
        lo = idx - 1 if idx > 0 else idx
        blocks.append((q1, k1, v1, rows(lo, idx + 1), t1_ref, ob1, ls1, idx))
    for idx in range(N_BLK):
        lo = idx - 1 if idx % 4 else idx
        blocks.append((q2, k2, v2, rows(lo, idx + 1), t2_ref, ob2, ls2, idx))
    for r in range(N_RES):
        blocks.append((None, None, None, r, t3_ref, ob3, ls3, r))

    def scores_and_probs(blk, hh):
        qd, kd, _, kv_rows, t_ref, _, _, idx = blk
        q = qd[rows(idx, idx + 1), :] if qd is not None else q_ref[0, idx]
        keys = kd[kv_rows, :] if kd is not None else k_ref[0, kv_rows]
        n_keys = keys.shape[0]
        s = lax.dot_general(q * own[hh], keys, NT_DIMS, preferred_element_type=F32)
        s = s + t_ref[hh, :, 2 * bb - n_keys:2 * bb]
        m = jnp.max(s, axis=-1, keepdims=True)
        return m, jnp.exp2(s - m).astype(BF16)

    def finish(blk, m, e):
        _, _, vd, kv_rows, _, o_dst, l_dst, idx = blk
        vals = vd[kv_rows, :] if vd is not None else v_ref[0, kv_rows]
        n_keys = vals.shape[0]
        num = (jnp.dot(e[0], vals * own[0], preferred_element_type=F32)
               + jnp.dot(e[1], vals * own[1], preferred_element_type=F32))
        den = (jnp.dot(e[0], ones_ref[0, 0:n_keys, :], preferred_element_type=F32)
               + jnp.dot(e[1], ones_ref[1, 0:n_keys, :], preferred_element_type=F32))
        o_dst[idx] = num / den
        l_dst[idx] = jnp.where(lane < HEAD_DIM_A, m[0], m[1]) * LN2 + jnp.log(den)

    stage = {}
    for t in range(len(blocks) + ATTN_SKEW):
        if t < len(blocks):
            stage[t] = [scores_and_probs(blocks[t], hh) for hh in range(2)]
        d = t - ATTN_SKEW
        if d >= 0:
            finish(blocks[d], [stage[d][hh][0] for hh in range(2)], [stage[d][hh][1] for hh in range(2)])
            del stage[d]

    for r in range(N_RES):
        r4, c = r % 4, r // 4
        gather1 = lambda ref: jnp.concatenate([ref[n, 8 * r:8 * r + 8, :] for n in range(N_BLK)], axis=0)
        gather2 = lambda ref: jnp.concatenate(
            [ref[r4 * 4 + n, 32 * c:32 * c + 32, :] for n in range(4)], axis=0)
        o_b = [gather1(ob1), gather2(ob2), ob3[r]]
        l_b = [gather1(ls1), gather2(ls2), ls3[r]]
        top = jnp.maximum(jnp.maximum(l_b[0], l_b[1]), l_b[2])
        w_b = [jnp.exp(l - top) for l in l_b]
        num = w_b[0] * o_b[0] + w_b[1] * o_b[1] + w_b[2] * o_b[2]
        o_ref[0, r] = (num / (w_b[0] + w_b[1] + w_b[2])).astype(BF16)


def _dilated(qa, ka, va, t1, t2, t3):
    B = qa.shape[0]
    head_lanes = np.arange(LANES)[None, None, :] // HEAD_DIM_A == np.arange(2)[:, None, None]
    ones = jnp.asarray(np.broadcast_to(head_lanes, (2, 2 * BAND_BLOCK, LANES)).astype(np.float32), BF16)
    blk = pl.BlockSpec((1, N_RES, M_SUB, LANES), lambda b, h: (b, 0, 0, h))
    tab = lambda t: pl.BlockSpec((2,) + t.shape[1:], lambda b, h: (h, 0, 0))
    blocked_bf16 = pltpu.VMEM((N_BLK * BAND_BLOCK, LANES), BF16)
    blocked_f32 = pltpu.VMEM((N_BLK, BAND_BLOCK, LANES), F32)
    return pl.pallas_call(
        _dilated_kernel,
        grid=(B, N_HEADS_A // 2),
        in_specs=[blk, blk, blk, tab(t1), tab(t2), tab(t3), _full(ones)],
        out_specs=blk,
        out_shape=jax.ShapeDtypeStruct((B, N_RES, M_SUB, A_WIDTH), BF16),
        scratch_shapes=[blocked_bf16] * 6 + [blocked_f32] * 6,
        compiler_params=_cparams(("parallel", "parallel")),
        name="dilated_attn",
    )(qa, ka, va, t1, t2, t3, ones)


R_W1, R_W2, R_E1, R_E2, R_RANK1, R_RANK2 = range(6)
ROUTE_ROWS = 8
ROUTE_GROUP_ROWS = 8
ROUTE_EXPERT_ROW0 = 8


def _outproj_kernel(oa_ref, om_ref, x_ref, wo_ref, g_ref, wrh_ref, wrl_ref, br_ref, tri_ref,
                    h_ref, xn_ref, route_ref, cnt_ref, perm_scr, carry_scr):
    @pl.when((pl.program_id(0) == 0) & (pl.program_id(1) == 0))
    def _():
        carry_scr[...] = jnp.zeros(carry_scr.shape, F32)

    n_chunks = A_WIDTH // LANES
    for r in range(N_RES):
        for c in range(n_chunks):
            perm_scr[c, _residue_rows(r), :] = oa_ref[0, r, :, c * LANES:(c + 1) * LANES].astype(F32)
    oa = jnp.concatenate([perm_scr[c] for c in range(n_chunks)], axis=1).astype(BF16)
    h = (x_ref[0] + jnp.dot(oa, wo_ref[0:A_WIDTH, :], preferred_element_type=F32)
         + jnp.dot(om_ref[0], wo_ref[A_WIDTH:, :], preferred_element_type=F32))
    h_ref[0] = h
    xn = _rms(h, D_MODEL) * g_ref[...]
    xn_ref[0] = xn.astype(BF16)
    hi = xn.astype(BF16)
    lo = (xn - hi.astype(F32)).astype(BF16)
    lg = (lax.dot_general(wrh_ref[...], hi, NT_DIMS, preferred_element_type=F32)
          + lax.dot_general(wrh_ref[...], lo, NT_DIMS, preferred_element_type=F32)
          + lax.dot_general(wrl_ref[...], hi, NT_DIMS, preferred_element_type=F32)) + br_ref[...]

    sub = lax.broadcasted_iota(jnp.int32, (EXPERTS_PER_GROUP, TM), 0).astype(F32)
    cmax = lambda t: jnp.max(t, axis=0, keepdims=True)
    cmin = lambda t: jnp.min(t, axis=0, keepdims=True)
    csum = lambda t: jnp.sum(t, axis=0, keepdims=True)
    none = float(EXPERTS_PER_GROUP)

    gl = lg[0:ROUTE_GROUP_ROWS]
    ge = jnp.exp(gl - cmax(gl))
    gsum = csum(ge)
    g_gate = 1.0 / gsum
    g_idx = cmin(jnp.where(ge / gsum == g_gate, sub, none))
    el = lg[ROUTE_EXPERT_ROW0:ROUTE_EXPERT_ROW0 + EXPERTS_PER_GROUP]
    for g in range(1, N_GROUPS):
        r0 = ROUTE_EXPERT_ROW0 + g * EXPERTS_PER_GROUP
        el = jnp.where(g_idx == float(g), lg[r0:r0 + EXPERTS_PER_GROUP], el)
    ee = jnp.exp(el - cmax(el))
    esum = csum(ee)
    eprob = ee / esum
    p1 = 1.0 / esum
    i1 = cmin(jnp.where(eprob == p1, sub, none))
    rest = jnp.where(sub == i1, -1.0, eprob)
    p2 = cmax(rest)
    i2 = cmin(jnp.where(rest == p2, sub, none))
    den = p1 + p2
    e1 = g_idx * EXPERTS_PER_GROUP + i1
    e2 = g_idx * EXPERTS_PER_GROUP + i2
    erow = lax.broadcasted_iota(jnp.int32, (N_EXPERTS, TM), 0).astype(F32)
    onehot = ((erow == e1) | (erow == e2)).astype(F32)
    before = jnp.dot(onehot.astype(BF16), tri_ref[...], preferred_element_type=F32) + carry_scr[:, 0:1]
    rank1 = csum(jnp.where(erow == e1, before, 0.0))
    rank2 = csum(jnp.where(erow == e2, before, 0.0))
    carry_scr[...] = carry_scr[...] + jnp.sum(onehot, axis=1, keepdims=True)
    cnt_ref[...] = carry_scr[...]

    record = jnp.zeros((ROUTE_ROWS, TM), F32)
    for i, val in ((R_W1, g_gate * (p1 / den)), (R_W2, g_gate * (p2 / den)), (R_E1, e1), (R_E2, e2),
                   (R_RANK1, rank1), (R_RANK2, rank2)):
        record = jnp.where(sub == float(i), val, record)
    route_ref[0] = record


def _outproj(oa, om, x, b0, wo, g, wrh, wrl, br, tri):
    B = oa.shape[0]
    rows = lambda c: pl.BlockSpec((1, TM, c), lambda b, j: (b, j, 0))
    shp = lambda c, dt: jax.ShapeDtypeStruct((B, SEQ, c), dt)
    return pl.pallas_call(
        _outproj_kernel,
        grid=(B, N_ROW_STEPS),
        in_specs=[pl.BlockSpec((1, N_RES, M_STEP, A_WIDTH), lambda b, j: (b, 0, j, 0)),
                  rows(M_WIDTH), pl.BlockSpec((1, TM, D_MODEL), lambda b, j: (b + b0, j, 0)),
                  _full(wo), _full(g), _full(wrh), _full(wrl), _full(br), _full(tri)],
        out_specs=[rows(D_MODEL), rows(D_MODEL),
                   pl.BlockSpec((1, ROUTE_ROWS, TM), lambda b, j: (b * N_ROW_STEPS + j, 0, 0)),
                   pl.BlockSpec((N_EXPERTS, LANES), lambda b, j: (0, 0))],
        out_shape=[shp(D_MODEL, F32), shp(D_MODEL, BF16),
                   jax.ShapeDtypeStruct((B * N_ROW_STEPS, ROUTE_ROWS, TM), F32),
                   jax.ShapeDtypeStruct((N_EXPERTS, LANES), F32)],
        scratch_shapes=[pltpu.VMEM((A_WIDTH // LANES, TM, LANES), F32), pltpu.VMEM((N_EXPERTS, LANES), F32)],
        compiler_params=_cparams(("arbitrary", "arbitrary")),
        name="outproj_router",
    )(oa, om, x, wo, g, wrh, wrl, br, tri)


def _expert_kernel(blk_e_ref, n_used_ref, x_ref, wg_ref, wu_ref, wd_ref, y_ref, wg_s, wu_s, wd_s):
    i = pl.program_id(0)

    @pl.when(i < n_used_ref[0])
    def _():
        @pl.when((i == 0) | (blk_e_ref[i] != blk_e_ref[jnp.maximum(i - 1, 0)]))
        def _():
            wg_s[...] = wg_ref[0].astype(BF16)
            wu_s[...] = wu_ref[0].astype(BF16)
            wd_s[...] = wd_ref[0].astype(BF16)

        half = MOE_TM // SUB_TILES
        gate_up = []
        for s in range(SUB_TILES):
            x = x_ref[s * half:(s + 1) * half, :]
            gate_up.append((jnp.dot(x, wg_s[...], preferred_element_type=F32),
                            jnp.dot(x, wu_s[...], preferred_element_type=F32)))
        for s, (gate, up) in enumerate(gate_up):
            hdn = (gate * jax.nn.sigmoid(gate) * up).astype(BF16)
            y_ref[s * half:(s + 1) * half, :] = jnp.dot(hdn, wd_s[...], preferred_element_type=F32).astype(y_ref.dtype)

    @pl.when(i >= n_used_ref[0])
    def _():
        y_ref[...] = jnp.zeros(y_ref.shape, y_ref.dtype)


def _experts(blk_e, n_used, xs, wg, wu, wd):
    n_blocks = xs.shape[0] // MOE_TM
    row_in = lambda i, be, nu: (jnp.minimum(i, nu[0] - 1), 0)
    wsel = lambda i, be, nu: (be[i], 0, 0)
    return pl.pallas_call(
        _expert_kernel,
        grid_spec=pltpu.PrefetchScalarGridSpec(
            num_scalar_prefetch=2,
            grid=(n_blocks,),
            in_specs=[pl.BlockSpec((MOE_TM, D_MODEL), row_in),
                      pl.BlockSpec((1, D_MODEL, EXPERT_FF), wsel),
                      pl.BlockSpec((1, D_MODEL, EXPERT_FF), wsel),
                      pl.BlockSpec((1, EXPERT_FF, D_MODEL), wsel)],
            out_specs=pl.BlockSpec((MOE_TM, D_MODEL), lambda i, be, nu: (i, 0)),
            scratch_shapes=[pltpu.VMEM((D_MODEL, EXPERT_FF), BF16), pltpu.VMEM((D_MODEL, EXPERT_FF), BF16),
                            pltpu.VMEM((EXPERT_FF, D_MODEL), BF16)]),
        out_shape=jax.ShapeDtypeStruct(xs.shape, BF16),
        compiler_params=_cparams(("arbitrary",)),
        name="expert_ffn",
    )(blk_e, n_used, xs, wg, wu, wd)


def _ple_kernel(n_parts, part_b, *refs):
    parts = [refs[4 * i:4 * i + 4] for i in range(n_parts)]
    p_ref, wp_ref, gp_ref, wg_ref, bg_ref, o_ref = refs[4 * n_parts:]

    def combine_and_gate(h_ref, y1_ref, y2_ref, route_ref):
        rec = jnp.concatenate([route_ref[0], jnp.zeros((LANES - ROUTE_ROWS, TM), F32)], axis=0)
        route = jnp.concatenate([rec[:, c * LANES:(c + 1) * LANES].T for c in range(TM // LANES)], axis=0)
        h = (h_ref[0] + route[:, R_W1:R_W1 + 1] * y1_ref[0].astype(F32)
             + route[:, R_W2:R_W2 + 1] * y2_ref[0].astype(F32))
        e = _rms(jnp.dot(p_ref[0].astype(BF16), wp_ref[...], preferred_element_type=F32), D_MODEL) * gp_ref[...]
        g = jax.nn.sigmoid(jnp.dot(h.astype(BF16), wg_ref[...], preferred_element_type=F32) + bg_ref[...])
        o_ref[0] = h + g * e

    for i, part in enumerate(parts):
        pl.when(pl.program_id(0) // part_b == i)(functools.partial(combine_and_gate, *part))


def _ple(parts, p, wp, gp, wg, bg):
    n_parts, part_b = len(parts), parts[0][0].shape[0]
    rows = lambda c: pl.BlockSpec((1, TM, c), lambda b, j: (b, j, 0))
    in_specs, args = [], []
    for i, (h, y_tok, route) in enumerate(parts):
        local = lambda b, i=i: jnp.clip(b - i * part_b, 0, part_b - 1)
        step = lambda b, j, local=local: local(b) * N_ROW_STEPS + j
        in_specs += [pl.BlockSpec((1, TM, D_MODEL), lambda b, j, local=local: (local(b), j, 0)),
                     pl.BlockSpec((1, TM, D_MODEL), lambda b, j, step=step: (0, step(b, j), 0)),
                     pl.BlockSpec((1, TM, D_MODEL), lambda b, j, step=step: (1, step(b, j), 0)),
                     pl.BlockSpec((1, ROUTE_ROWS, TM), lambda b, j, step=step: (step(b, j), 0, 0))]
        args += [h, y_tok, y_tok, route]
    return pl.pallas_call(
        functools.partial(_ple_kernel, n_parts, part_b),
        grid=(n_parts * part_b, N_ROW_STEPS),
        in_specs=in_specs + [rows(PLE_DIM), _full(wp), _full(gp), _full(wg), _full(bg)],
        out_specs=rows(D_MODEL),
        out_shape=jax.ShapeDtypeStruct((n_parts * part_b, SEQ, D_MODEL), F32),
        compiler_params=_cparams(("parallel", "parallel")),
        name="ple_gate",
    )(*args, p, wp, gp, wg, bg)


def _t5_bucket(dist):
    max_exact = REL_BUCKETS // 2
    n = jnp.maximum(dist, 0)
    nf = jnp.maximum(n, 1).astype(F32)
    large = max_exact + (jnp.log(nf / max_exact) / math.log(REL_MAX_DISTANCE / max_exact)
                         * (REL_BUCKETS - max_exact)).astype(jnp.int32)
    large = jnp.minimum(large, REL_BUCKETS - 1)
    return jnp.where(n < max_exact, n, large)


def _bias_table(rel_bias, local_index, dilation, with_prev):
    loc = np.asarray(local_index)
    delta = loc[:, None] - loc[None, :]
    if with_prev:
        delta = np.concatenate([delta + BAND_BLOCK, delta], axis=1)
    ok = (delta >= 0) & (delta <= BAND_BLOCK)
    bucket = _t5_bucket(jnp.asarray(delta * dilation, jnp.int32))
    picked = jnp.where(bucket[None, :, :, None] == jnp.arange(REL_BUCKETS, dtype=jnp.int32),
                       rel_bias.astype(F32).T[:, None, None, :], 0.0)
    return jnp.where(jnp.asarray(ok)[None], jnp.sum(picked, axis=-1) * LOG2E, NEG)


def _block_diag_ones(sizes, total):
    g = np.zeros((total, total), np.float32)
    o = 0
    for s, on in sizes:
        if on:
            g[o:o + s, o:o + s] = 1.0
        o += s
    return jnp.asarray(g, BF16)


def _dispatch_plan(route, counts, n_tokens):
    field = lambda i: route[:, i, :].reshape(n_tokens).astype(jnp.int32)
    e = jnp.stack([field(R_E1), field(R_E2)], axis=-1)
    rank = jnp.stack([field(R_RANK1), field(R_RANK2)], axis=-1)
    pcounts = (counts + MOE_TM - 1) // MOE_TM * MOE_TM
    pend = jnp.cumsum(pcounts)
    pstart = pend - pcounts
    ids = jnp.arange(N_EXPERTS, dtype=jnp.int32)
    pos = rank + jnp.sum(jnp.where(e[..., None] == ids, pstart, 0), axis=-1)
    n_assign = n_tokens * TOP_K
    shift = int(math.ceil(math.log2(n_assign)))
    keys = (e.reshape(-1) << shift) | jnp.arange(n_assign, dtype=jnp.int32)
    tok_sorted = (jnp.sort(keys) & ((1 << shift) - 1)) // TOP_K
    n_rows = n_assign + N_EXPERTS * MOE_TM
    rows = jnp.arange(n_rows, dtype=jnp.int32)
    row_e = jnp.sum(rows[:, None] >= pend[None, :], axis=-1)
    row_e = jnp.minimum(row_e, N_EXPERTS - 1)
    pick = lambda tbl: jnp.sum(jnp.where(row_e[:, None] == ids, tbl, 0), axis=-1)
    within = rows - pick(pstart)
    src = jnp.clip(pick(jnp.cumsum(counts) - counts) + within, 0, n_assign - 1)
    row_tok = jnp.where(within < pick(counts), tok_sorted[src], rows % n_tokens)
    n_blocks = n_rows // MOE_TM
    n_used = (pend[-1] // MOE_TM).astype(jnp.int32)
    blk_start = jnp.minimum(jnp.arange(n_blocks, dtype=jnp.int32), n_used - 1) * MOE_TM
    blk_e = jnp.minimum(jnp.sum(blk_start[:, None] >= pend[None, :], axis=-1), N_EXPERTS - 1).astype(jnp.int32)
    return row_tok, pos, blk_e, n_used.reshape(1)


def kernel(x, p, rel_bias, norm_mix_gain, w_in, qn_a_gain, kn_a_gain, q_a_gain, w_q_up, kv_a_gain, w_kv_up, qn_nope_gain, qn_rope_gain, kn_nope_gain, kn_rope_gain, w_out, norm_ffn_gain, w_router_group, b_router_group, w_router_expert, b_router_expert, w_exp_gate, w_exp_up, w_exp_down, w_ple_proj, ple_norm_gain, w_ple_gate, b_ple_gate):
    B, S, D = x.shape
    assert (S, D) == (SEQ, D_MODEL) and p.shape[0] == 1
    row = lambda a: a.reshape(1, -1).astype(F32)
    zeros = lambda *s: jnp.zeros(s, F32)

    w_in_p = jnp.concatenate([w_in[0, :, :C_KR0], zeros(D, 64), w_in[0, :, C_KR0:], zeros(D, 32)],
                             axis=1).astype(BF16)
    gsum_a = _block_diag_ones([(HEAD_DIM_A, True)] * N_HEADS_A, A_WIDTH)
    gq_a = row(jnp.tile(qn_a_gain[0], N_HEADS_A)) * (HEAD_DIM_A ** -0.5 * LOG2E)
    gk_a = row(jnp.tile(kn_a_gain[0], N_HEADS_A))

    wq_p = jnp.pad(w_q_up[0].reshape(Q_LORA, N_HEADS_M, NOPE_DIM + ROPE_DIM),
                   ((0, 0), (0, 0), (0, HEAD_SLOT - NOPE_DIM - ROPE_DIM))).reshape(Q_LORA, -1).astype(BF16)
    wkv = w_kv_up[0].reshape(KV_LORA, N_HEADS_M, NOPE_DIM + V_DIM)
    wk_p = jnp.pad(wkv[..., :NOPE_DIM], ((0, 0), (0, 0), (0, HEAD_SLOT - NOPE_DIM))).reshape(KV_LORA, -1)
    wkv_p = jnp.concatenate([wk_p, wkv[..., NOPE_DIM:].reshape(KV_LORA, -1)], axis=1).astype(BF16)
    gs_q = _block_diag_ones([(NOPE_DIM, True), (ROPE_DIM, True), (32, False)], HEAD_SLOT)
    gs_k = _block_diag_ones([(NOPE_DIM, True), (64, False)], HEAD_SLOT)
    inv_cnt_q = jnp.asarray(np.concatenate([np.full(64, 1 / NOPE_DIM), np.full(32, 1 / ROPE_DIM),
                                            np.ones(32)]).astype(np.float32)).reshape(1, HEAD_SLOT)
    mla_scale = (NOPE_DIM + ROPE_DIM) ** -0.5 * LOG2E
    gq_m = row(jnp.concatenate([qn_nope_gain[0], qn_rope_gain[0], zeros(32)])) * mla_scale
    gk_m = row(jnp.concatenate([kn_nope_gain[0], zeros(64)]))
    gkr_m = row(jnp.concatenate([zeros(64), kn_rope_gain[0], zeros(32)]))

    half = ROPE_DIM // 2
    inv = 1.0 / (ROPE_THETA ** (jnp.arange(half, dtype=F32) * 2.0 / ROPE_DIM))
    ang = jnp.arange(S, dtype=jnp.int32).astype(F32)[:, None] * inv[None, :]
    cosv, sinv = jnp.cos(ang), jnp.sin(ang)
    cos_t = jnp.concatenate([jnp.ones((S, 64), F32), cosv, cosv, jnp.ones((S, 32), F32)], -1)
    sin_a = jnp.concatenate([zeros(S, 80), sinv, zeros(S, 32)], -1)
    sin_b = jnp.concatenate([zeros(S, 64), -sinv, zeros(S, 48)], -1)

    loc1 = [16 * a + r for r in range(16) for a in range(8)]
    loc2 = [4 * a + c for c in range(4) for a in range(32)]
    loc3 = list(range(BAND_BLOCK))
    t1 = _bias_table(rel_bias, loc1, 1, True)
    t2 = _bias_table(rel_bias, loc2, 4, True)
    t3 = jnp.concatenate([jnp.full((N_HEADS_A, BAND_BLOCK, BAND_BLOCK), NEG, F32),
                          _bias_table(rel_bias, loc3, 16, False)], axis=-1)

    n_pad_g, n_pad_e = ROUTE_GROUP_ROWS - N_GROUPS, LANES - ROUTE_EXPERT_ROW0 - N_EXPERTS
    w_r = jnp.concatenate([w_router_group[0].T, zeros(n_pad_g, D), w_router_expert[0].T, zeros(n_pad_e, D)], axis=0)
    w_r_hi = w_r.astype(BF16)
    w_r_lo = (w_r - w_r_hi.astype(F32)).astype(BF16)
    b_r = jnp.concatenate([b_router_group[0], jnp.full((n_pad_g,), NEG, F32), b_router_expert[0],
                           zeros(n_pad_e)]).reshape(LANES, 1)
    tri = jnp.asarray(np.triu(np.ones((TM, TM), np.float32), 1), BF16)

    w_out_b = w_out[0].astype(BF16)
    part_b = B // BATCH_PARTS
    n_tokens = part_b * S
    parts = []
    for b0 in range(0, B, part_b):
        qa, ka, va, cq, ckv, kr = _inproj(x, b0, part_b, row(norm_mix_gain[0]), w_in_p, gsum_a, gq_a, gk_a,
                                          row(q_a_gain[0]), row(kv_a_gain[0]))
        o_a = _dilated(qa, ka, va, t1, t2, t3)
        q_m, k_m, v_m = _mla_prep(cq, ckv, kr, wq_p, wkv_p, gs_q, gs_k, inv_cnt_q, gq_m, gk_m, gkr_m,
                                  cos_t, sin_a, sin_b)
        o_m = _mla_attn(q_m, k_m, v_m)
        h1, xn2, route, cnt = _outproj(o_a, o_m, x, b0, w_out_b, row(norm_ffn_gain[0]),
                                       w_r_hi, w_r_lo, b_r, tri)
        row_tok, pos, blk_e, n_used = _dispatch_plan(route, cnt[:, 0].astype(jnp.int32), n_tokens)
        xs = xn2.reshape(n_tokens, D)[row_tok]
        y = _experts(blk_e, n_used, xs, w_exp_gate[0], w_exp_up[0], w_exp_down[0])
        y_tok = y[pos.T.reshape(-1)].reshape(TOP_K, n_tokens, D)
        parts.append((h1, y_tok, route))
    return _ple(parts, p[0], w_ple_proj[0].astype(BF16), row(ple_norm_gain[0]),
                w_ple_gate[0].astype(BF16), row(b_ple_gate[0]))
```

```python
import functools
import math

import jax
import jax.numpy as jnp
import numpy as np
from jax import lax
from jax.experimental import pallas as pl
from jax.experimental.pallas import tpu as pltpu

F32 = jnp.float32
BF16 = jnp.bfloat16

D_MODEL = 1024
SEQ = 2048
PLE_DIM = 256
EPS = 1e-6
NEG = -1e30
HEAD_DIM_A = 64
A_WIDTH = 512
N_HEADS_A = 8
BAND_BLOCK = 128
REL_BUCKETS = 32
REL_MAX_DISTANCE = 2048
M_WIDTH = 512
V_DIM = 64
N_HEADS_M = 8
Q_LORA = 384
KV_LORA = 256
NOPE_DIM = 64
ROPE_DIM = 32
ROPE_THETA = 10000.0
N_GROUPS = 4
EXPERTS_PER_GROUP = 8
N_EXPERTS = 32
TOP_K = 2
EXPERT_FF = 512

LANES = 128
N_RES = 16
M_SUB = SEQ // N_RES
TM = 1024
N_ROW_STEPS = SEQ // TM
M_STEP = TM // N_RES
HEAD_SLOT = LANES
MLA_TQ = 128
MLA_SKEW = 20
MOE_TM = 512
ATTN_SKEW = 4
BATCH_PARTS = 1
SUB_TILES = 2
IN_ROWS, IN_TILES = 512, 2
VMEM_LIMIT = 48 * 1024 * 1024

IN_COLS_PAD = 3 * A_WIDTH + Q_LORA + KV_LORA + LANES
C_Q0, C_K0, C_V0, C_CQ0, C_CKV0, C_KR0 = 0, 512, 1024, 1536, 1920, 2176

NT_DIMS = (((1,), (1,)), ((), ()))
LOG2E = math.log2(math.e)
LN2 = math.log(2.0)


def _cparams(sem):
    return pltpu.CompilerParams(dimension_semantics=sem, vmem_limit_bytes=VMEM_LIMIT)


def _full(a):
    return pl.BlockSpec(a.shape, lambda *_: (0,) * a.ndim)


def _rms(x, n):
    return x * lax.rsqrt(jnp.sum(x * x, axis=-1, keepdims=True) * (1.0 / n) + EPS)


def _residue_rows(r):
    return pl.ds(r, M_STEP, stride=N_RES)


def _inproj_kernel(x_ref, g_ref, w_ref, gsum_ref, gq_ref, gk_ref, gcq_ref, gckv_ref,
                   qa_ref, ka_ref, va_ref, cq_ref, ckv_ref, kr_ref, perm_scr, xn_scr):
    n_chunks = A_WIDTH // LANES
    for s in range(IN_TILES):
        xn_scr[s] = (_rms(x_ref[0, s * IN_ROWS:(s + 1) * IN_ROWS], D_MODEL) * g_ref[...]).astype(BF16)

    def head_norm(t, gain_ref):
        ss = jnp.dot((t * t).astype(BF16), gsum_ref[...], preferred_element_type=F32)
        return t * lax.rsqrt(ss * (1.0 / HEAD_DIM_A) + EPS) * gain_ref[...]

    m_tile = IN_ROWS // N_RES
    for s in range(IN_TILES):
        rows = slice(s * IN_ROWS, (s + 1) * IN_ROWS)
        proj = lambda c0, c1, s=s: jnp.dot(xn_scr[s], w_ref[:, c0:c1], preferred_element_type=F32)

        def put_residue(ref, val, t, s=s):
            for c in range(n_chunks):
                perm_scr[s, t, c] = val[:, c * LANES:(c + 1) * LANES]
            for r in range(N_RES):
                for c in range(n_chunks):
                    ref[0, r, s * m_tile:(s + 1) * m_tile, c * LANES:(c + 1) * LANES] = (
                        perm_scr[s, t, c, pl.ds(r, m_tile, stride=N_RES), :].astype(ref.dtype))

        put_residue(qa_ref, head_norm(proj(C_Q0, C_K0), gq_ref), 0)
        put_residue(ka_ref, head_norm(proj(C_K0, C_V0), gk_ref), 1)
        put_residue(va_ref, proj(C_V0, C_CQ0), 2)
        cq_ref[0, rows] = (_rms(proj(C_CQ0, C_CKV0), Q_LORA) * gcq_ref[...]).astype(BF16)
        ckv_ref[0, rows] = (_rms(proj(C_CKV0, C_KR0), KV_LORA) * gckv_ref[...]).astype(BF16)
        kr_ref[0, rows] = proj(C_KR0, IN_COLS_PAD)


def _inproj(x, b0, B, g, w_in_p, gsum, gq, gk, gcq, gckv):
    step_rows = IN_TILES * IN_ROWS
    res = lambda: (jax.ShapeDtypeStruct((B, N_RES, M_SUB, A_WIDTH), BF16),
                   pl.BlockSpec((1, N_RES, step_rows // N_RES, A_WIDTH), lambda b, j: (b, 0, j, 0)))
    nat = lambda c, dt: (jax.ShapeDtypeStruct((B, SEQ, c), dt),
                         pl.BlockSpec((1, step_rows, c), lambda b, j: (b, j, 0)))
    outs = [res(), res(), res(), nat(Q_LORA, BF16), nat(KV_LORA, BF16), nat(LANES, F32)]
    return pl.pallas_call(
        _inproj_kernel,
        grid=(B, SEQ // step_rows),
        in_specs=[pl.BlockSpec((1, step_rows, D_MODEL), lambda b, j: (b + b0, j, 0)),
                  _full(g), _full(w_in_p), _full(gsum), _full(gq), _full(gk), _full(gcq), _full(gckv)],
        out_specs=[o[1] for o in outs],
        out_shape=[o[0] for o in outs],
        scratch_shapes=[pltpu.VMEM((IN_TILES, 3, A_WIDTH // LANES, IN_ROWS, LANES), F32),
                        pltpu.VMEM((IN_TILES, IN_ROWS, D_MODEL), BF16)],
        compiler_params=_cparams(("parallel", "parallel")),
        name="inproj",
    )(x, g, w_in_p, gsum, gq, gk, gcq, gckv)


def _mla_prep_kernel(cq_ref, ckv_ref, kr_ref, wq_ref, wkv_ref, gsq_ref, gsk_ref, icq_ref,
                     gq_ref, gk_ref, gkr_ref, cos_ref, sa_ref, sb_ref, q_ref, k_ref, v_ref):
    sub = TM // SUB_TILES
    lane = lax.broadcasted_iota(jnp.int32, (sub, LANES), 1)

    for t in range(SUB_TILES):
        rows = slice(t * sub, (t + 1) * sub)
        cos, sa, sb = cos_ref[rows, :], sa_ref[rows, :], sb_ref[rows, :]

        def rope(x, cos=cos, sa=sa, sb=sb):
            return x * cos + pltpu.roll(x, 16, 1) * sa + pltpu.roll(x, LANES - 16, 1) * sb

        q = jnp.dot(cq_ref[0, rows, :], wq_ref[...], preferred_element_type=F32)
        kv = jnp.dot(ckv_ref[0, rows, :], wkv_ref[...], preferred_element_type=F32)
        k_rope = rope(_rms(kr_ref[0, rows, :], ROPE_DIM) * gkr_ref[...])
        for h in range(N_HEADS_M):
            sl = slice(h * HEAD_SLOT, (h + 1) * HEAD_SLOT)
            qh = q[:, sl]
            ss = jnp.dot((qh * qh).astype(BF16), gsq_ref[...], preferred_element_type=F32)
            q_ref[0, rows, sl] = rope(qh * lax.rsqrt(ss * icq_ref[...] + EPS) * gq_ref[...]).astype(BF16)
            kh = kv[:, sl]
            ssk = jnp.dot((kh * kh).astype(BF16), gsk_ref[...], preferred_element_type=F32)
            kn = kh * lax.rsqrt(ssk * (1.0 / NOPE_DIM) + EPS) * gk_ref[...] + k_rope
            k_ref[0, rows, sl] = kn.astype(BF16)
        for hp in range(N_HEADS_M // 2):
            v_pair = kv[:, N_HEADS_M * HEAD_SLOT + hp * LANES:N_HEADS_M * HEAD_SLOT + (hp + 1) * LANES]
            v_ref[0, rows, (2 * hp) * HEAD_SLOT:(2 * hp + 1) * HEAD_SLOT] = jnp.where(lane < V_DIM, v_pair, 1.0).astype(BF16)
            v_ref[0, rows, (2 * hp + 1) * HEAD_SLOT:(2 * hp + 2) * HEAD_SLOT] = jnp.where(lane < V_DIM, 1.0, v_pair).astype(BF16)


def _mla_prep(cq, ckv, kr, wq_p, wkv_p, gsq, gsk, icq, gq, gk, gkr, cos, sa, sb):
    B = cq.shape[0]
    rows = lambda c: pl.BlockSpec((1, TM, c), lambda b, j: (b, j, 0))
    tab = pl.BlockSpec((TM, LANES), lambda b, j: (j, 0))
    wide = N_HEADS_M * HEAD_SLOT
    return pl.pallas_call(
        _mla_prep_kernel,
        grid=(B, N_ROW_STEPS),
        in_specs=[rows(Q_LORA), rows(KV_LORA), rows(LANES), _full(wq_p), _full(wkv_p), _full(gsq),
                  _full(gsk), _full(icq), _full(gq), _full(gk), _full(gkr), tab, tab, tab],
        out_specs=[rows(wide)] * 3,
        out_shape=[jax.ShapeDtypeStruct((B, SEQ, wide), BF16)] * 3,
        compiler_params=_cparams(("parallel", "parallel")),
        name="mla_prep",
    )(cq, ckv, kr, wq_p, wkv_p, gsq, gsk, icq, gq, gk, gkr, cos, sa, sb)


def _mla_attn_kernel(q_ref, k_ref, v_ref, o_ref):
    n_q = SEQ // MLA_TQ
    row = lax.broadcasted_iota(jnp.int32, (MLA_TQ, MLA_TQ), 0)
    col = lax.broadcasted_iota(jnp.int32, (MLA_TQ, MLA_TQ), 1)
    lane = lax.broadcasted_iota(jnp.int32, (MLA_TQ, LANES), 1)
    heads = [slice(hh * HEAD_SLOT, (hh + 1) * HEAD_SLOT) for hh in range(2)]

    def probs(i, hh):
        n_keys = (i + 1) * MLA_TQ
        s = lax.dot_general(q_ref[0, i * MLA_TQ:n_keys, heads[hh]], k_ref[0, 0:n_keys, heads[hh]],
                            NT_DIMS, preferred_element_type=F32)
        diag = jnp.where(col <= row, s[:, n_keys - MLA_TQ:], NEG)
        s = diag if i == 0 else jnp.concatenate([s[:, :n_keys - MLA_TQ], diag], axis=1)
        return jnp.exp2(s - jnp.max(s, axis=-1, keepdims=True)).astype(BF16)

    def values(i, hh, p):
        return jnp.dot(p, v_ref[0, 0:(i + 1) * MLA_TQ, heads[hh]], preferred_element_type=F32)

    units = [(i, hh) for i in range(n_q) for hh in range(2)]
    acc, pending = {}, []
    for u in units:
        pending.append((u, probs(*u)))
        if len(pending) > MLA_SKEW:
            done, p = pending.pop(0)
            acc[done] = values(*done, p)
    for done, p in pending:
        acc[done] = values(*done, p)
    for i in range(n_q):
        num = jnp.where(lane < V_DIM, acc[(i, 0)], acc[(i, 1)])
        den = pltpu.roll(jnp.where(lane < V_DIM, acc[(i, 1)], acc[(i, 0)]), V_DIM, 1)
        o_ref[0, i * MLA_TQ:(i + 1) * MLA_TQ, :] = (num / den).astype(BF16)


def _mla_attn(q, k, v):
    B = q.shape[0]
    pair = lambda c: pl.BlockSpec((1, SEQ, c), lambda b, h: (b, 0, h))
    return pl.pallas_call(
        _mla_attn_kernel,
        grid=(B, N_HEADS_M // 2),
        in_specs=[pair(2 * HEAD_SLOT)] * 3,
        out_specs=pair(2 * V_DIM),
        out_shape=jax.ShapeDtypeStruct((B, SEQ, M_WIDTH), BF16),
        compiler_params=_cparams(("parallel", "parallel")),
        name="mla_attn",
    )(q, k, v)


N_BLK = SEQ // BAND_BLOCK


def _dilated_kernel(q_ref, k_ref, v_ref, t1_ref, t2_ref, t3_ref, ones_ref, o_ref,
                    q1, k1, v1, q2, k2, v2, ob1, ls1, ob2, ls2, ob3, ls3):
    bb = BAND_BLOCK
    lane = lax.broadcasted_iota(jnp.int32, (bb, LANES), 1)
    lane_row = lax.broadcasted_iota(jnp.int32, (1, LANES), 1)
    own = [(lane_row < HEAD_DIM_A).astype(BF16), (lane_row >= HEAD_DIM_A).astype(BF16)]
    rows = lambda lo, hi: slice(lo * bb, hi * bb)

    for src, d1, d2 in ((q_ref, q1, q2), (k_ref, k1, k2), (v_ref, v1, v2)):
        for n2 in range(N_BLK // 2):
            pieces = [src[0, r, 16 * n2:16 * n2 + 16, :].astype(F32) for r in range(N_RES)]
            d1[rows(2 * n2, 2 * n2 + 1), :] = jnp.concatenate([p[0:8] for p in pieces], axis=0).astype(BF16)
            d1[rows(2 * n2 + 1, 2 * n2 + 2), :] = jnp.concatenate([p[8:16] for p in pieces], axis=0).astype(BF16)
        for r4 in range(4):
            for n in range(4):
                d2[rows(r4 * 4 + n, r4 * 4 + n + 1), :] = jnp.concatenate(
                    [src[0, r4 + 4 * c, 32 * n:32 * n + 32, :] for c in range(4)], axis=0)

    blocks = []
    for idx in range(N_BLK):
        lo = idx - 1 if idx > 0 else idx
        blocks.append((q1, k1, v1, rows(lo, idx + 1), t1_ref, ob1, ls1, idx))
    for idx in range(N_BLK):
        lo = idx - 1 if idx % 4 else idx
        blocks.append((q2, k2, v2, rows(lo, idx + 1), t2_ref, ob2, ls2, idx))
    for r in range(N_RES):
        blocks.append((None, None, None, r, t3_ref, ob3, ls3, r))

    def scores_and_probs(blk, hh):
        qd, kd, _, kv_rows, t_ref, _, _, idx = blk
        q = qd[rows(idx, idx + 1), :] if qd is not None else q_ref[0, idx]
        keys = kd[kv_rows, :] if kd is not None else k_ref[0, kv_rows]
        n_keys = keys.shape[0]
        s = lax.dot_general(q * own[hh], keys, NT_DIMS, preferred_element_type=F32)
        s = s + t_ref[hh, :, 2 * bb - n_keys:2 * bb]
        m = jnp.max(s, axis=-1, keepdims=True)
        return m, jnp.exp2(s - m).astype(BF16)

    def finish(blk, m, e):
        _, _, vd, kv_rows, _, o_dst, l_dst, idx = blk
        vals = vd[kv_rows, :] if vd is not None else v_ref[0, kv_rows]
        n_keys = vals.shape[0]
        num = (jnp.dot(e[0], vals * own[0], preferred_element_type=F32)
               + jnp.dot(e[1], vals * own[1], preferred_element_type=F32))
        den = (jnp.dot(e[0], ones_ref[0, 0:n_keys, :], preferred_element_type=F32)
               + jnp.dot(e[1], ones_ref[1, 0:n_keys, :], preferred_element_type=F32))
        o_dst[idx] = num / den
        l_dst[idx] = jnp.where(lane < HEAD_DIM_A, m[0], m[1]) * LN2 + jnp.log(den)

    stage = {}
    for t in range(len(blocks) + ATTN_SKEW):
        if t < len(blocks):
            stage[t] = [scores_and_probs(blocks[t], hh) for hh in range(2)]
        d = t - ATTN_SKEW
        if d >= 0:
            finish(blocks[d], [stage[d][hh][0] for hh in range(2)], [stage[d][hh][1] for hh in range(2)])
            del stage[d]

    for r in range(N_RES):
        r4, c = r % 4, r // 4
        gather1 = lambda ref: jnp.concatenate([ref[n, 8 * r:8 * r + 8, :] for n in range(N_BLK)], axis=0)
        gather2 = lambda ref: jnp.concatenate(
            [ref[r4 * 4 + n, 32 * c:32 * c + 32, :] for n in range(4)], axis=0)
        o_b = [gather1(ob1), gather2(ob2), ob3[r]]
        l_b = [gather1(ls1), gather2(ls2), ls3[r]]
        top = jnp.maximum(jnp.maximum(l_b[0], l_b[1]), l_b[2])
        w_b = [jnp.exp(l - top) for l in l_b]
        num = w_b[0] * o_b[0] + w_b[1] * o_b[1] + w_b[2] * o_b[2]
        o_ref[0, r] = (num / (w_b[0] + w_b[1] + w_b[2])).astype(BF16)


def _dilated(qa, ka, va, t1, t2, t3):
    B = qa.shape[0]
    head_lanes = np.arange(LANES)[None, None, :] // HEAD_DIM_A == np.arange(2)[:, None, None]
    ones = jnp.asarray(np.broadcast_to(head_lanes, (2, 2 * BAND_BLOCK, LANES)).astype(np.float32), BF16)
    blk = pl.BlockSpec((1, N_RES, M_SUB, LANES), lambda b, h: (b, 0, 0, h))
    tab = lambda t: pl.BlockSpec((2,) + t.shape[1:], lambda b, h: (h, 0, 0))
    blocked_bf16 = pltpu.VMEM((N_BLK * BAND_BLOCK, LANES), BF16)
    blocked_f32 = pltpu.VMEM((N_BLK, BAND_BLOCK, LANES), F32)
    return pl.pallas_call(
        _dilated_kernel,
        grid=(B, N_HEADS_A // 2),
        in_specs=[blk, blk, blk, tab(t1), tab(t2), tab(t3), _full(ones)],
        out_specs=blk,
        out_shape=jax.ShapeDtypeStruct((B, N_RES, M_SUB, A_WIDTH), BF16),
        scratch_shapes=[blocked_bf16] * 6 + [blocked_f32] * 6,
        compiler_params=_cparams(("parallel", "parallel")),
        name="dilated_attn",
    )(qa, ka, va, t1, t2, t3, ones)


R_W1, R_W2, R_E1, R_E2, R_RANK1, R_RANK2 = range(6)
ROUTE_ROWS = 8
ROUTE_GROUP_ROWS = 8
ROUTE_EXPERT_ROW0 = 8


def _outproj_kernel(oa_ref, om_ref, x_ref, wo_ref, g_ref, wrh_ref, wrl_ref, br_ref, tri_ref,
                    h_ref, xn_ref, route_ref, cnt_ref, perm_scr, carry_scr):
    @pl.when((pl.program_id(0) == 0) & (pl.program_id(1) == 0))
    def _():
        carry_scr[...] = jnp.zeros(carry_scr.shape, F32)

    n_chunks = A_WIDTH // LANES
    for r in range(N_RES):
        for c in range(n_chunks):
            perm_scr[c, _residue_rows(r), :] = oa_ref[0, r, :, c * LANES:(c + 1) * LANES].astype(F32)
    oa = jnp.concatenate([perm_scr[c] for c in range(n_chunks)], axis=1).astype(BF16)
    h = (x_ref[0] + jnp.dot(oa, wo_ref[0:A_WIDTH, :], preferred_element_type=F32)
         + jnp.dot(om_ref[0], wo_ref[A_WIDTH:, :], preferred_element_type=F32))
    h_ref[0] = h
    xn = _rms(h, D_MODEL) * g_ref[...]
    xn_ref[0] = xn.astype(BF16)
    hi = xn.astype(BF16)
    lo = (xn - hi.astype(F32)).astype(BF16)
    lg = (lax.dot_general(wrh_ref[...], hi, NT_DIMS, preferred_element_type=F32)
          + lax.dot_general(wrh_ref[...], lo, NT_DIMS, preferred_element_type=F32)
          + lax.dot_general(wrl_ref[...], hi, NT_DIMS, preferred_element_type=F32)) + br_ref[...]

    sub = lax.broadcasted_iota(jnp.int32, (EXPERTS_PER_GROUP, TM), 0).astype(F32)
    cmax = lambda t: jnp.max(t, axis=0, keepdims=True)
    cmin = lambda t: jnp.min(t, axis=0, keepdims=True)
    csum = lambda t: jnp.sum(t, axis=0, keepdims=True)
    none = float(EXPERTS_PER_GROUP)

    gl = lg[0:ROUTE_GROUP_ROWS]
    ge = jnp.exp(gl - cmax(gl))
    gsum = csum(ge)
    g_gate = 1.0 / gsum
    g_idx = cmin(jnp.where(ge / gsum == g_gate, sub, none))
    el = lg[ROUTE_EXPERT_ROW0:ROUTE_EXPERT_ROW0 + EXPERTS_PER_GROUP]
    for g in range(1, N_GROUPS):
        r0 = ROUTE_EXPERT_ROW0 + g * EXPERTS_PER_GROUP
        el = jnp.where(g_idx == float(g), lg[r0:r0 + EXPERTS_PER_GROUP], el)
    ee = jnp.exp(el - cmax(el))
    esum = csum(ee)
    eprob = ee / esum
    p1 = 1.0 / esum
    i1 = cmin(jnp.where(eprob == p1, sub, none))
    rest = jnp.where(sub == i1, -1.0, eprob)
    p2 = cmax(rest)
    i2 = cmin(jnp.where(rest == p2, sub, none))
    den = p1 + p2
    e1 = g_idx * EXPERTS_PER_GROUP + i1
    e2 = g_idx * EXPERTS_PER_GROUP + i2
    erow = lax.broadcasted_iota(jnp.int32, (N_EXPERTS, TM), 0).astype(F32)
    onehot = ((erow == e1) | (erow == e2)).astype(F32)
    before = jnp.dot(onehot.astype(BF16), tri_ref[...], preferred_element_type=F32) + carry_scr[:, 0:1]
    rank1 = csum(jnp.where(erow == e1, before, 0.0))
    rank2 = csum(jnp.where(erow == e2, before, 0.0))
    carry_scr[...] = carry_scr[...] + jnp.sum(onehot, axis=1, keepdims=True)
    cnt_ref[...] = carry_scr[...]

    record = jnp.zeros((ROUTE_ROWS, TM), F32)
    for i, val in ((R_W1, g_gate * (p1 / den)), (R_W2, g_gate * (p2 / den)), (R_E1, e1), (R_E2, e2),
                   (R_RANK1, rank1), (R_RANK2, rank2)):
        record = jnp.where(sub == float(i), val, record)
    route_ref[0] = record


def _outproj(oa, om, x, b0, wo, g, wrh, wrl, br, tri):
    B = oa.shape[0]
    rows = lambda c: pl.BlockSpec((1, TM, c), lambda b, j: (b, j, 0))
    shp = lambda c, dt: jax.ShapeDtypeStruct((B, SEQ, c), dt)
    return pl.pallas_call(
        _outproj_kernel,
        grid=(B, N_ROW_STEPS),
        in_specs=[pl.BlockSpec((1, N_RES, M_STEP, A_WIDTH), lambda b, j: (b, 0, j, 0)),
                  rows(M_WIDTH), pl.BlockSpec((1, TM, D_MODEL), lambda b, j: (b + b0, j, 0)),
                  _full(wo), _full(g), _full(wrh), _full(wrl), _full(br), _full(tri)],
        out_specs=[rows(D_MODEL), rows(D_MODEL),
                   pl.BlockSpec((1, ROUTE_ROWS, TM), lambda b, j: (b * N_ROW_STEPS + j, 0, 0)),
                   pl.BlockSpec((N_EXPERTS, LANES), lambda b, j: (0, 0))],
        out_shape=[shp(D_MODEL, F32), shp(D_MODEL, BF16),
                   jax.ShapeDtypeStruct((B * N_ROW_STEPS, ROUTE_ROWS, TM), F32),
                   jax.ShapeDtypeStruct((N_EXPERTS, LANES), F32)],
        scratch_shapes=[pltpu.VMEM((A_WIDTH // LANES, TM, LANES), F32), pltpu.VMEM((N_EXPERTS, LANES), F32)],
        compiler_params=_cparams(("arbitrary", "arbitrary")),
        name="outproj_router",
    )(oa, om, x, wo, g, wrh, wrl, br, tri)


def _expert_kernel(blk_e_ref, n_used_ref, x_ref, wg_ref, wu_ref, wd_ref, y_ref, wg_s, wu_s, wd_s):
    i = pl.program_id(0)

    @pl.when(i < n_used_ref[0])
    def _():
        @pl.when((i == 0) | (blk_e_ref[i] != blk_e_ref[jnp.maximum(i - 1, 0)]))
        def _():
            wg_s[...] = wg_ref[0].astype(BF16)
            wu_s[...] = wu_ref[0].astype(BF16)
            wd_s[...] = wd_ref[0].astype(BF16)

        half = MOE_TM // SUB_TILES
        gate_up = []
        for s in range(SUB_TILES):
            x = x_ref[s * half:(s + 1) * half, :]
            gate_up.append((jnp.dot(x, wg_s[...], preferred_element_type=F32),
                            jnp.dot(x, wu_s[...], preferred_element_type=F32)))
        for s, (gate, up) in enumerate(gate_up):
            hdn = (gate * jax.nn.sigmoid(gate) * up).astype(BF16)
            y_ref[s * half:(s + 1) * half, :] = jnp.dot(hdn, wd_s[...], preferred_element_type=F32).astype(y_ref.dtype)

    @pl.when(i >= n_used_ref[0])
    def _():
        y_ref[...] = jnp.zeros(y_ref.shape, y_ref.dtype)


def _experts(blk_e, n_used, xs, wg, wu, wd):
    n_blocks = xs.shape[0] // MOE_TM
    row_in = lambda i, be, nu: (jnp.minimum(i, nu[0] - 1), 0)
    wsel = lambda i, be, nu: (be[i], 0, 0)
    return pl.pallas_call(
        _expert_kernel,
        grid_spec=pltpu.PrefetchScalarGridSpec(
            num_scalar_prefetch=2,
            grid=(n_blocks,),
            in_specs=[pl.BlockSpec((MOE_TM, D_MODEL), row_in),
                      pl.BlockSpec((1, D_MODEL, EXPERT_FF), wsel),
                      pl.BlockSpec((1, D_MODEL, EXPERT_FF), wsel),
                      pl.BlockSpec((1, EXPERT_FF, D_MODEL), wsel)],
            out_specs=pl.BlockSpec((MOE_TM, D_MODEL), lambda i, be, nu: (i, 0)),
            scratch_shapes=[pltpu.VMEM((D_MODEL, EXPERT_FF), BF16), pltpu.VMEM((D_MODEL, EXPERT_FF), BF16),
                            pltpu.VMEM((EXPERT_FF, D_MODEL), BF16)]),
        out_shape=jax.ShapeDtypeStruct(xs.shape, BF16),
        compiler_params=_cparams(("arbitrary",)),
        name="expert_ffn",
    )(blk_e, n_used, xs, wg, wu, wd)


def _ple_kernel(n_parts, part_b, *refs):
    parts = [refs[4 * i:4 * i + 4] for i in range(n_parts)]
    p_ref, wp_ref, gp_ref, wg_ref, bg_ref, o_ref = refs[4 * n_parts:]

    def combine_and_gate(h_ref, y1_ref, y2_ref, route_ref):
        rec = jnp.concatenate([route_ref[0], jnp.zeros((LANES - ROUTE_ROWS, TM), F32)], axis=0)
        route = jnp.concatenate([rec[:, c * LANES:(c + 1) * LANES].T for c in range(TM // LANES)], axis=0)
        h = (h_ref[0] + route[:, R_W1:R_W1 + 1] * y1_ref[0].astype(F32)
             + route[:, R_W2:R_W2 + 1] * y2_ref[0].astype(F32))
        e = _rms(jnp.dot(p_ref[0].astype(BF16), wp_ref[...], preferred_element_type=F32), D_MODEL) * gp_ref[...]
        g = jax.nn.sigmoid(jnp.dot(h.astype(BF16), wg_ref[...], preferred_element_type=F32) + bg_ref[...])
        o_ref[0] = h + g * e

    for i, part in enumerate(parts):
        pl.when(pl.program_id(0) // part_b == i)(functools.partial(combine_and_gate, *part))


def _ple(parts, p, wp, gp, wg, bg):
    n_parts, part_b = len(parts), parts[0][0].shape[0]
    rows = lambda c: pl.BlockSpec((1, TM, c), lambda b, j: (b, j, 0))
    in_specs, args = [], []
    for i, (h, y_tok, route) in enumerate(parts):
        local = lambda b, i=i: jnp.clip(b - i * part_b, 0, part_b - 1)
        step = lambda b, j, local=local: local(b) * N_ROW_STEPS + j
        in_specs += [pl.BlockSpec((1, TM, D_MODEL), lambda b, j, local=local: (local(b), j, 0)),
                     pl.BlockSpec((1, TM, D_MODEL), lambda b, j, step=step: (0, step(b, j), 0)),
                     pl.BlockSpec((1, TM, D_MODEL), lambda b, j, step=step: (1, step(b, j), 0)),
                     pl.BlockSpec((1, ROUTE_ROWS, TM), lambda b, j, step=step: (step(b, j), 0, 0))]
        args += [h, y_tok, y_tok, route]
    return pl.pallas_call(
        functools.partial(_ple_kernel, n_parts, part_b),
        grid=(n_parts * part_b, N_ROW_STEPS),
        in_specs=in_specs + [rows(PLE_DIM), _full(wp), _full(gp), _full(wg), _full(bg)],
        out_specs=rows(D_MODEL),
        out_shape=jax.ShapeDtypeStruct((n_parts * part_b, SEQ, D_MODEL), F32),
        compiler_params=_cparams(("parallel", "parallel")),
        name="ple_gate",
    )(*args, p, wp, gp, wg, bg)


def _t5_bucket(dist):
    max_exact = REL_BUCKETS // 2
    n = jnp.maximum(dist, 0)
    nf = jnp.maximum(n, 1).astype(F32)
    large = max_exact + (jnp.log(nf / max_exact) / math.log(REL_MAX_DISTANCE / max_exact)
                         * (REL_BUCKETS - max_exact)).astype(jnp.int32)
    large = jnp.minimum(large, REL_BUCKETS - 1)
    return jnp.where(n < max_exact, n, large)


def _bias_table(rel_bias, local_index, dilation, with_prev):
    loc = np.asarray(local_index)
    delta = loc[:, None] - loc[None, :]
    if with_prev:
        delta = np.concatenate([delta + BAND_BLOCK, delta], axis=1)
    ok = (delta >= 0) & (delta <= BAND_BLOCK)
    bucket = _t5_bucket(jnp.asarray(delta * dilation, jnp.int32))
    picked = jnp.where(bucket[None, :, :, None] == jnp.arange(REL_BUCKETS, dtype=jnp.int32),
                       rel_bias.astype(F32).T[:, None, None, :], 0.0)
    return jnp.where(jnp.asarray(ok)[None], jnp.sum(picked, axis=-1) * LOG2E, NEG)


def _block_diag_ones(sizes, total):
    g = np.zeros((total, total), np.float32)
    o = 0
    for s, on in sizes:
        if on:
            g[o:o + s, o:o + s] = 1.0
        o += s
    return jnp.asarray(g, BF16)


def _dispatch_plan(route, counts, n_tokens):
    field = lambda i: route[:, i, :].reshape(n_tokens).astype(jnp.int32)
    e = jnp.stack([field(R_E1), field(R_E2)], axis=-1)
    rank = jnp.stack([field(R_RANK1), field(R_RANK2)], axis=-1)
    pcounts = (counts + MOE_TM - 1) // MOE_TM * MOE_TM
    pend = jnp.cumsum(pcounts)
    pstart = pend - pcounts
    ids = jnp.arange(N_EXPERTS, dtype=jnp.int32)
    pos = rank + jnp.sum(jnp.where(e[..., None] == ids, pstart, 0), axis=-1)
    n_assign = n_tokens * TOP_K
    shift = int(math.ceil(math.log2(n_assign)))
    keys = (e.reshape(-1) << shift) | jnp.arange(n_assign, dtype=jnp.int32)
    tok_sorted = (jnp.sort(keys) & ((1 << shift) - 1)) // TOP_K
    n_rows = n_assign + N_EXPERTS * MOE_TM
    rows = jnp.arange(n_rows, dtype=jnp.int32)
    row_e = jnp.sum(rows[:, None] >= pend[None, :], axis=-1)
    row_e = jnp.minimum(row_e, N_EXPERTS - 1)
    pick = lambda tbl: jnp.sum(jnp.where(row_e[:, None] == ids, tbl, 0), axis=-1)
    within = rows - pick(pstart)
    src = jnp.clip(pick(jnp.cumsum(counts) - counts) + within, 0, n_assign - 1)
    row_tok = jnp.where(within < pick(counts), tok_sorted[src], rows % n_tokens)
    n_blocks = n_rows // MOE_TM
    n_used = (pend[-1] // MOE_TM).astype(jnp.int32)
    blk_start = jnp.minimum(jnp.arange(n_blocks, dtype=jnp.int32), n_used - 1) * MOE_TM
    blk_e = jnp.minimum(jnp.sum(blk_start[:, None] >= pend[None, :], axis=-1), N_EXPERTS - 1).astype(jnp.int32)
    return row_tok, pos, blk_e, n_used.reshape(1)


def kernel(x, p, rel_bias, norm_mix_gain, w_in, qn_a_gain, kn_a_gain, q_a_gain, w_q_up, kv_a_gain, w_kv_up, qn_nope_gain, qn_rope_gain, kn_nope_gain, kn_rope_gain, w_out, norm_ffn_gain, w_router_group, b_router_group, w_router_expert, b_router_expert, w_exp_gate, w_exp_up, w_exp_down, w_ple_proj, ple_norm_gain, w_ple_gate, b_ple_gate):
    B, S, D = x.shape
    assert (S, D) == (SEQ, D_MODEL) and p.shape[0] == 1
    row = lambda a: a.reshape(1, -1).astype(F32)
    zeros = lambda *s: jnp.zeros(s, F32)

    w_in_p = jnp.concatenate([w_in[0, :, :C_KR0], zeros(D, 64), w_in[0, :, C_KR0:], zeros(D, 32)],
                             axis=1).astype(BF16)
    gsum_a = _block_diag_ones([(HEAD_DIM_A, True)] * N_HEADS_A, A_WIDTH)
    gq_a = row(jnp.tile(qn_a_gain[0], N_HEADS_A)) * (HEAD_DIM_A ** -0.5 * LOG2E)
    gk_a = row(jnp.tile(kn_a_gain[0], N_HEADS_A))

    wq_p = jnp.pad(w_q_up[0].reshape(Q_LORA, N_HEADS_M, NOPE_DIM + ROPE_DIM),
                   ((0, 0), (0, 0), (0, HEAD_SLOT - NOPE_DIM - ROPE_DIM))).reshape(Q_LORA, -1).astype(BF16)
    wkv = w_kv_up[0].reshape(KV_LORA, N_HEADS_M, NOPE_DIM + V_DIM)
    wk_p = jnp.pad(wkv[..., :NOPE_DIM], ((0, 0), (0, 0), (0, HEAD_SLOT - NOPE_DIM))).reshape(KV_LORA, -1)
    wkv_p = jnp.concatenate([wk_p, wkv[..., NOPE_DIM:].reshape(KV_LORA, -1)], axis=1).astype(BF16)
    gs_q = _block_diag_ones([(NOPE_DIM, True), (ROPE_DIM, True), (32, False)], HEAD_SLOT)
    gs_k = _block_diag_ones([(NOPE_DIM, True), (64, False)], HEAD_SLOT)
    inv_cnt_q = jnp.asarray(np.concatenate([np.full(64, 1 / NOPE_DIM), np.full(32, 1 / ROPE_DIM),
                                            np.ones(32)]).astype(np.float32)).reshape(1, HEAD_SLOT)
    mla_scale = (NOPE_DIM + ROPE_DIM) ** -0.5 * LOG2E
    gq_m = row(jnp.concatenate([qn_nope_gain[0], qn_rope_gain[0], zeros(32)])) * mla_scale
    gk_m = row(jnp.concatenate([kn_nope_gain[0], zeros(64)]))
    gkr_m = row(jnp.concatenate([zeros(64), kn_rope_gain[0], zeros(32)]))

    half = ROPE_DIM // 2
    inv = 1.0 / (ROPE_THETA ** (jnp.arange(half, dtype=F32) * 2.0 / ROPE_DIM))
    ang = jnp.arange(S, dtype=jnp.int32).astype(F32)[:, None] * inv[None, :]
    cosv, sinv = jnp.cos(ang), jnp.sin(ang)
    cos_t = jnp.concatenate([jnp.ones((S, 64), F32), cosv, cosv, jnp.ones((S, 32), F32)], -1)
    sin_a = jnp.concatenate([zeros(S, 80), sinv, zeros(S, 32)], -1)
    sin_b = jnp.concatenate([zeros(S, 64), -sinv, zeros(S, 48)], -1)

    loc1 = [16 * a + r for r in range(16) for a in range(8)]
    loc2 = [4 * a + c for c in range(4) for a in range(32)]
    loc3 = list(range(BAND_BLOCK))
    t1 = _bias_table(rel_bias, loc1, 1, True)
    t2 = _bias_table(rel_bias, loc2, 4, True)
    t3 = jnp.concatenate([jnp.full((N_HEADS_A, BAND_BLOCK, BAND_BLOCK), NEG, F32),
                          _bias_table(rel_bias, loc3, 16, False)], axis=-1)

    n_pad_g, n_pad_e = ROUTE_GROUP_ROWS - N_GROUPS, LANES - ROUTE_EXPERT_ROW0 - N_EXPERTS
    w_r = jnp.concatenate([w_router_group[0].T, zeros(n_pad_g, D), w_router_expert[0].T, zeros(n_pad_e, D)], axis=0)
    w_r_hi = w_r.astype(BF16)
    w_r_lo = (w_r - w_r_hi.astype(F32)).astype(BF16)
    b_r = jnp.concatenate([b_router_group[0], jnp.full((n_pad_g,), NEG, F32), b_router_expert[0],
                           zeros(n_pad_e)]).reshape(LANES, 1)
    tri = jnp.asarray(np.triu(np.ones((TM, TM), np.float32), 1), BF16)

    w_out_b = w_out[0].astype(BF16)
    part_b = B // BATCH_PARTS
    n_tokens = part_b * S
    parts = []
    for b0 in range(0, B, part_b):
        qa, ka, va, cq, ckv, kr = _inproj(x, b0, part_b, row(norm_mix_gain[0]), w_in_p, gsum_a, gq_a, gk_a,
                                          row(q_a_gain[0]), row(kv_a_gain[0]))
        o_a = _dilated(qa, ka, va, t1, t2, t3)
        q_m, k_m, v_m = _mla_prep(cq, ckv, kr, wq_p, wkv_p, gs_q, gs_k, inv_cnt_q, gq_m, gk_m, gkr_m,
                                  cos_t, sin_a, sin_b)
        o_m = _mla_attn(q_m, k_m, v_m)
        h1, xn2, route, cnt = _outproj(o_a, o_m, x, b0, w_out_b, row(norm_ffn_gain[0]),
                                       w_r_hi, w_r_lo, b_r, tri)
        row_tok, pos, blk_e, n_used = _dispatch_plan(route, cnt[:, 0].astype(jnp.int32), n_tokens)
        xs = xn2.reshape(n_tokens, D)[row_tok]
        y = _experts(blk_e, n_used, xs, w_exp_gate[0], w_exp_up[0], w_exp_down[0])
        y_tok = y[pos.T.reshape(-1)].reshape(TOP_K, n_tokens, D)
        parts.append((h1, y_tok, route))
    return _ple(parts, p[0], w_ple_proj[0].astype(BF16), row(ple_norm_gain[0]),
                w_ple_gate[0].astype(BF16), row(b_ple_gate[0]))
```

```python
import functools
import math

import jax
import jax.numpy as jnp
import numpy as np
from jax import lax
from jax.experimental import pallas as pl
from jax.experimental.pallas import tpu as pltpu

F32 = jnp.float32
BF16 = jnp.bfloat16

D_MODEL = 1024
SEQ = 2048
PLE_DIM = 256
EPS = 1e-6
NEG = -1e30
HEAD_DIM_A = 64
A_WIDTH = 512
N_HEADS_A = 8
BAND_BLOCK = 128
REL_BUCKETS = 32
REL_MAX_DISTANCE = 2048
M_WIDTH = 512
V_DIM = 64
N_HEADS_M = 8
Q_LORA = 384
KV_LORA = 256
NOPE_DIM = 64
ROPE_DIM = 32
ROPE_THETA = 10000.0
N_GROUPS = 4
EXPERTS_PER_GROUP = 8
N_EXPERTS = 32
TOP_K = 2
EXPERT_FF = 512

LANES = 128
N_RES = 16
M_SUB = SEQ // N_RES
TM = 1024
N_ROW_STEPS = SEQ // TM
M_STEP = TM // N_RES
HEAD_SLOT = LANES
MLA_TQ = 128
MLA_SKEW = 20
MOE_TM = 512
ATTN_SKEW = 4
BATCH_PARTS = 1
SUB_TILES = 2
IN_ROWS, IN_TILES = 512, 2
VMEM_LIMIT = 48 * 1024 * 1024

IN_COLS_PAD = 3 * A_WIDTH + Q_LORA + KV_LORA + LANES
C_Q0, C_K0, C_V0, C_CQ0, C_CKV0, C_KR0 = 0, 512, 1024, 1536, 1920, 2176

NT_DIMS = (((1,), (1,)), ((), ()))
LOG2E = math.log2(math.e)
LN2 = math.log(2.0)


def _cparams(sem):
    return pltpu.CompilerParams(dimension_semantics=sem, vmem_limit_bytes=VMEM_LIMIT)


def _full(a):
    return pl.BlockSpec(a.shape, lambda *_: (0,) * a.ndim)


def _rms(x, n):
    return x * lax.rsqrt(jnp.sum(x * x, axis=-1, keepdims=True) * (1.0 / n) + EPS)


def _residue_rows(r):
    return pl.ds(r, M_STEP, stride=N_RES)


def _inproj_kernel(x_ref, g_ref, w_ref, gsum_ref, gq_ref, gk_ref, gcq_ref, gckv_ref,
                   qa_ref, ka_ref, va_ref, cq_ref, ckv_ref, kr_ref, perm_scr, xn_scr):
    n_chunks = A_WIDTH // LANES
    for s in range(IN_TILES):
        xn_scr[s] = (_rms(x_ref[0, s * IN_ROWS:(s + 1) * IN_ROWS], D_MODEL) * g_ref[...]).astype(BF16)

    def head_norm(t, gain_ref):
        ss = jnp.dot((t * t).astype(BF16), gsum_ref[...], preferred_element_type=F32)
        return t * lax.rsqrt(ss * (1.0 / HEAD_DIM_A) + EPS) * gain_ref[...]

    m_tile = IN_ROWS // N_RES
    for s in range(IN_TILES):
        rows = slice(s * IN_ROWS, (s + 1) * IN_ROWS)
        proj = lambda c0, c1, s=s: jnp.dot(xn_scr[s], w_ref[:, c0:c1], preferred_element_type=F32)

        def put_residue(ref, val, t, s=s):
            for c in range(n_chunks):
                perm_scr[s, t, c] = val[:, c * LANES:(c + 1) * LANES]
            for r in range(N_RES):
                for c in range(n_chunks):
                    ref[0, r, s * m_tile:(s + 1) * m_tile, c * LANES:(c + 1) * LANES] = (
                        perm_scr[s, t, c, pl.ds(r, m_tile, stride=N_RES), :].astype(ref.dtype))

        put_residue(qa_ref, head_norm(proj(C_Q0, C_K0), gq_ref), 0)
        put_residue(ka_ref, head_norm(proj(C_K0, C_V0), gk_ref), 1)
        put_residue(va_ref, proj(C_V0, C_CQ0), 2)
        cq_ref[0, rows] = (_rms(proj(C_CQ0, C_CKV0), Q_LORA) * gcq_ref[...]).astype(BF16)
        ckv_ref[0, rows] = (_rms(proj(C_CKV0, C_KR0), KV_LORA) * gckv_ref[...]).astype(BF16)
        kr_ref[0, rows] = proj(C_KR0, IN_COLS_PAD)


def _inproj(x, b0, B, g, w_in_p, gsum, gq, gk, gcq, gckv):
    step_rows = IN_TILES * IN_ROWS
    res = lambda: (jax.ShapeDtypeStruct((B, N_RES, M_SUB, A_WIDTH), BF16),
                   pl.BlockSpec((1, N_RES, step_rows // N_RES, A_WIDTH), lambda b, j: (b, 0, j, 0)))
    nat = lambda c, dt: (jax.ShapeDtypeStruct((B, SEQ, c), dt),
                         pl.BlockSpec((1, step_rows, c), lambda b, j: (b, j, 0)))
    outs = [res(), res(), res(), nat(Q_LORA, BF16), nat(KV_LORA, BF16), nat(LANES, F32)]
    return pl.pallas_call(
        _inproj_kernel,
        grid=(B, SEQ // step_rows),
        in_specs=[pl.BlockSpec((1, step_rows, D_MODEL), lambda b, j: (b + b0, j, 0)),
                  _full(g), _full(w_in_p), _full(gsum), _full(gq), _full(gk), _full(gcq), _full(gckv)],
        out_specs=[o[1] for o in outs],
        out_shape=[o[0] for o in outs],
        scratch_shapes=[pltpu.VMEM((IN_TILES, 3, A_WIDTH // LANES, IN_ROWS, LANES), F32),
                        pltpu.VMEM((IN_TILES, IN_ROWS, D_MODEL), BF16)],
        compiler_params=_cparams(("parallel", "parallel")),
        name="inproj",
    )(x, g, w_in_p, gsum, gq, gk, gcq, gckv)


def _mla_prep_kernel(cq_ref, ckv_ref, kr_ref, wq_ref, wkv_ref, gsq_ref, gsk_ref, icq_ref,
                     gq_ref, gk_ref, gkr_ref, cos_ref, sa_ref, sb_ref, q_ref, k_ref, v_ref):
    sub = TM // SUB_TILES
    lane = lax.broadcasted_iota(jnp.int32, (sub, LANES), 1)

    for t in range(SUB_TILES):
        rows = slice(t * sub, (t + 1) * sub)
        cos, sa, sb = cos_ref[rows, :], sa_ref[rows, :], sb_ref[rows, :]

        def rope(x, cos=cos, sa=sa, sb=sb):
            return x * cos + pltpu.roll(x, 16, 1) * sa + pltpu.roll(x, LANES - 16, 1) * sb

        q = jnp.dot(cq_ref[0, rows, :], wq_ref[...], preferred_element_type=F32)
        kv = jnp.dot(ckv_ref[0, rows, :], wkv_ref[...], preferred_element_type=F32)
        k_rope = rope(_rms(kr_ref[0, rows, :], ROPE_DIM) * gkr_ref[...])
        for h in range(N_HEADS_M):
            sl = slice(h * HEAD_SLOT, (h + 1) * HEAD_SLOT)
            qh = q[:, sl]
            ss = jnp.dot((qh * qh).astype(BF16), gsq_ref[...], preferred_element_type=F32)
            q_ref[0, rows, sl] = rope(qh * lax.rsqrt(ss * icq_ref[...] + EPS) * gq_ref[...]).astype(BF16)
            kh = kv[:, sl]
            ssk = jnp.dot((kh * kh).astype(BF16), gsk_ref[...], preferred_element_type=F32)
            kn = kh * lax.rsqrt(ssk * (1.0 / NOPE_DIM) + EPS) * gk_ref[...] + k_rope
            k_ref[0, rows, sl] = kn.astype(BF16)
        for hp in range(N_HEADS_M // 2):
            v_pair = kv[:, N_HEADS_M * HEAD_SLOT + hp * LANES:N_HEADS_M * HEAD_SLOT + (hp + 1) * LANES]
            v_ref[0, rows, (2 * hp) * HEAD_SLOT:(2 * hp + 1) * HEAD_SLOT] = jnp.where(lane < V_DIM, v_pair, 1.0).astype(BF16)
            v_ref[0, rows, (2 * hp + 1) * HEAD_SLOT:(2 * hp + 2) * HEAD_SLOT] = jnp.where(lane < V_DIM, 1.0, v_pair).astype(BF16)


def _mla_prep(cq, ckv, kr, wq_p, wkv_p, gsq, gsk, icq, gq, gk, gkr, cos, sa, sb):
    B = cq.shape[0]
    rows = lambda c: pl.BlockSpec((1, TM, c), lambda b, j: (b, j, 0))
    tab = pl.BlockSpec((TM, LANES), lambda b, j: (j, 0))
    wide = N_HEADS_M * HEAD_SLOT
    return pl.pallas_call(
        _mla_prep_kernel,
        grid=(B, N_ROW_STEPS),
        in_specs=[rows(Q_LORA), rows(KV_LORA), rows(LANES), _full(wq_p), _full(wkv_p), _full(gsq),
                  _full(gsk), _full(icq), _full(gq), _full(gk), _full(gkr), tab, tab, tab],
        out_specs=[rows(wide)] * 3,
        out_shape=[jax.ShapeDtypeStruct((B, SEQ, wide), BF16)] * 3,
        compiler_params=_cparams(("parallel", "parallel")),
        name="mla_prep",
    )(cq, ckv, kr, wq_p, wkv_p, gsq, gsk, icq, gq, gk, gkr, cos, sa, sb)


def _mla_attn_kernel(q_ref, k_ref, v_ref, o_ref):
    n_q = SEQ // MLA_TQ
    row = lax.broadcasted_iota(jnp.int32, (MLA_TQ, MLA_TQ), 0)
    col = lax.broadcasted_iota(jnp.int32, (MLA_TQ, MLA_TQ), 1)
    lane = lax.broadcasted_iota(jnp.int32, (MLA_TQ, LANES), 1)
    heads = [slice(hh * HEAD_SLOT, (hh + 1) * HEAD_SLOT) for hh in range(2)]

    def probs(i, hh):
        n_keys = (i + 1) * MLA_TQ
        s = lax.dot_general(q_ref[0, i * MLA_TQ:n_keys, heads[hh]], k_ref[0, 0:n_keys, heads[hh]],
                            NT_DIMS, preferred_element_type=F32)
        diag = jnp.where(col <= row, s[:, n_keys - MLA_TQ:], NEG)
        s = diag if i == 0 else jnp.concatenate([s[:, :n_keys - MLA_TQ], diag], axis=1)
        return jnp.exp2(s - jnp.max(s, axis=-1, keepdims=True)).astype(BF16)

    def values(i, hh, p):
        return jnp.dot(p, v_ref[0, 0:(i + 1) * MLA_TQ, heads[hh]], preferred_element_type=F32)

    units = [(i, hh) for i in range(n_q) for hh in range(2)]
    acc, pending = {}, []
    for u in units:
        pending.append((u, probs(*u)))
        if len(pending) > MLA_SKEW:
            done, p = pending.pop(0)
            acc[done] = values(*done, p)
    for done, p in pending:
        acc[done] = values(*done, p)
    for i in range(n_q):
        num = jnp.where(lane < V_DIM, acc[(i, 0)], acc[(i, 1)])
        den = pltpu.roll(jnp.where(lane < V_DIM, acc[(i, 1)], acc[(i, 0)]), V_DIM, 1)
        o_ref[0, i * MLA_TQ:(i + 1) * MLA_TQ, :] = (num / den).astype(BF16)


def _mla_attn(q, k, v):
    B = q.shape[0]
    pair = lambda c: pl.BlockSpec((1, SEQ, c), lambda b, h: (b, 0, h))
    return pl.pallas_call(
        _mla_attn_kernel,
        grid=(B, N_HEADS_M // 2),
        in_specs=[pair(2 * HEAD_SLOT)] * 3,
        out_specs=pair(2 * V_DIM),
        out_shape=jax.ShapeDtypeStruct((B, SEQ, M_WIDTH), BF16),
        compiler_params=_cparams(("parallel", "parallel")),
        name="mla_attn",
    )(q, k, v)


N_BLK = SEQ // BAND_BLOCK


def _dilated_kernel(q_ref, k_ref, v_ref, t1_ref, t2_ref, t3_ref, ones_ref, o_ref,
                    q1, k1, v1, q2, k2, v2, ob1, ls1, ob2, ls2, ob3, ls3):
    bb = BAND_BLOCK
    lane = lax.broadcasted_iota(jnp.int32, (bb, LANES), 1)
    lane_row = lax.broadcasted_iota(jnp.int32, (1, LANES), 1)
    own = [(lane_row < HEAD_DIM_A).astype(BF16), (lane_row >= HEAD_DIM_A).astype(BF16)]
    rows = lambda lo, hi: slice(lo * bb, hi * bb)

    for src, d1, d2 in ((q_ref, q1, q2), (k_ref, k1, k2), (v_ref, v1, v2)):
        for n2 in range(N_BLK // 2):
            pieces = [src[0, r, 16 * n2:16 * n2 + 16, :].astype(F32) for r in range(N_RES)]
            d1[rows(2 * n2, 2 * n2 + 1), :] = jnp.concatenate([p[0:8] for p in pieces], axis=0).astype(BF16)
            d1[rows(2 * n2 + 1, 2 * n2 + 2), :] = jnp.concatenate([p[8:16] for p in pieces], axis=0).astype(BF16)
        for r4 in range(4):
            for n in range(4):
                d2[rows(r4 * 4 + n, r4 * 4 + n + 1), :] = jnp.concatenate(
                    [src[0, r4 + 4 * c, 32 * n:32 * n + 32, :] for c in range(4)], axis=0)

    blocks = []
    for idx in range(N_BLK):
        lo = idx - 1 if idx > 0 else idx
        blocks.append((q1, k1, v1, rows(lo, idx + 1), t1_ref, ob1, ls1, idx))
    for idx in range(N_BLK):
        lo = idx - 1 if idx % 4 else idx
        blocks.append((q2, k2, v2, rows(lo, idx + 1), t2_ref, ob2, ls2, idx))
    for r in range(N_RES):
        blocks.append((None, None, None, r, t3_ref, ob3, ls3, r))

    def scores_and_probs(blk, hh):
        qd, kd, _, kv_rows, t_ref, _, _, idx = blk
        q = qd[rows(idx, idx + 1), :] if qd is not None else q_ref[0, idx]
        keys = kd[kv_rows, :] if kd is not None else k_ref[0, kv_rows]
        n_keys = keys.shape[0]
        s = lax.dot_general(q * own[hh], keys, NT_DIMS, preferred_element_type=F32)
        s = s + t_ref[hh, :, 2 * bb - n_keys:2 * bb]
        m = jnp.max(s, axis=-1, keepdims=True)
        return m, jnp.exp2(s - m).astype(BF16)

    def finish(blk, m, e):
        _, _, vd, kv_rows, _, o_dst, l_dst, idx = blk
        vals = vd[kv_rows, :] if vd is not None else v_ref[0, kv_rows]
        n_keys = vals.shape[0]
        num = (jnp.dot(e[0], vals * own[0], preferred_element_type=F32)
               + jnp.dot(e[1], vals * own[1], preferred_element_type=F32))
        den = (jnp.dot(e[0], ones_ref[0, 0:n_keys, :], preferred_element_type=F32)
               + jnp.dot(e[1], ones_ref[1, 0:n_keys, :], preferred_element_type=F32))
        o_dst[idx] = num / den
        l_dst[idx] = jnp.where(lane < HEAD_DIM_A, m[0], m[1]) * LN2 + jnp.log(den)

    stage = {}
    for t in range(len(blocks) + ATTN_SKEW):
        if t < len(blocks):
            stage[t] = [scores_and_probs(blocks[t], hh) for hh in range(2)]
        d = t - ATTN_SKEW
        if d >= 0:
            finish(blocks[d], [stage[d][hh][0] for hh in range(2)], [stage[d][hh][1] for hh in range(2)])
            del stage[d]

    for r in range(N_RES):
        r4, c = r % 4, r // 4
        gather1 = lambda ref: jnp.concatenate([ref[n, 8 * r:8 * r + 8, :] for n in range(N_BLK)], axis=0)
        gather2 = lambda ref: jnp.concatenate(
            [ref[r4 * 4 + n, 32 * c:32 * c + 32, :] for n in range(4)], axis=0)
        o_b = [gather1(ob1), gather2(ob2), ob3[r]]
        l_b = [gather1(ls1), gather2(ls2), ls3[r]]
        top = jnp.maximum(jnp.maximum(l_b[0], l_b[1]), l_b[2])
        w_b = [jnp.exp(l - top) for l in l_b]
        num = w_b[0] * o_b[0] + w_b[1] * o_b[1] + w_b[2] * o_b[2]
        o_ref[0, r] = (num / (w_b[0] + w_b[1] + w_b[2])).astype(BF16)


def _dilated(qa, ka, va, t1, t2, t3):
    B = qa.shape[0]
    head_lanes = np.arange(LANES)[None, None, :] // HEAD_DIM_A == np.arange(2)[:, None, None]
    ones = jnp.asarray(np.broadcast_to(head_lanes, (2, 2 * BAND_BLOCK, LANES)).astype(np.float32), BF16)
    blk = pl.BlockSpec((1, N_RES, M_SUB, LANES), lambda b, h: (b, 0, 0, h))
    tab = lambda t: pl.BlockSpec((2,) + t.shape[1:], lambda b, h: (h, 0, 0))
    blocked_bf16 = pltpu.VMEM((N_BLK * BAND_BLOCK, LANES), BF16)
    blocked_f32 = pltpu.VMEM((N_BLK, BAND_BLOCK, LANES), F32)
    return pl.pallas_call(
        _dilated_kernel,
        grid=(B, N_HEADS_A // 2),
        in_specs=[blk, blk, blk, tab(t1), tab(t2), tab(t3), _full(ones)],
        out_specs=blk,
        out_shape=jax.ShapeDtypeStruct((B, N_RES, M_SUB, A_WIDTH), BF16),
        scratch_shapes=[blocked_bf16] * 6 + [blocked_f32] * 6,
        compiler_params=_cparams(("parallel", "parallel")),
        name="dilated_attn",
    )(qa, ka, va, t1, t2, t3, ones)


R_W1, R_W2, R_E1, R_E2, R_RANK1, R_RANK2 = range(6)
ROUTE_ROWS = 8
ROUTE_GROUP_ROWS = 8
ROUTE_EXPERT_ROW0 = 8


def _outproj_kernel(oa_ref, om_ref, x_ref, wo_ref, g_ref, wrh_ref, wrl_ref, br_ref, tri_ref,
                    h_ref, xn_ref, route_ref, cnt_ref, perm_scr, carry_scr):
    @pl.when((pl.program_id(0) == 0) & (pl.program_id(1) == 0))
    def _():
        carry_scr[...] = jnp.zeros(carry_scr.shape, F32)

    n_chunks = A_WIDTH // LANES
    for r in range(N_RES):
        for c in range(n_chunks):
            perm_scr[c, _residue_rows(r), :] = oa_ref[0, r, :, c * LANES:(c + 1) * LANES].astype(F32)
    oa = jnp.concatenate([perm_scr[c] for c in range(n_chunks)], axis=1).astype(BF16)
    h = (x_ref[0] + jnp.dot(oa, wo_ref[0:A_WIDTH, :], preferred_element_type=F32)
         + jnp.dot(om_ref[0], wo_ref[A_WIDTH:, :], preferred_element_type=F32))
    h_ref[0] = h
    xn = _rms(h, D_MODEL) * g_ref[...]
    xn_ref[0] = xn.astype(BF16)
    hi = xn.astype(BF16)
    lo = (xn - hi.astype(F32)).astype(BF16)
    lg = (lax.dot_general(wrh_ref[...], hi, NT_DIMS, preferred_element_type=F32)
          + lax.dot_general(wrh_ref[...], lo, NT_DIMS, preferred_element_type=F32)
          + lax.dot_general(wrl_ref[...], hi, NT_DIMS, preferred_element_type=F32)) + br_ref[...]

    sub = lax.broadcasted_iota(jnp.int32, (EXPERTS_PER_GROUP, TM), 0).astype(F32)
    cmax = lambda t: jnp.max(t, axis=0, keepdims=True)
    cmin = lambda t: jnp.min(t, axis=0, keepdims=True)
    csum = lambda t: jnp.sum(t, axis=0, keepdims=True)
    none = float(EXPERTS_PER_GROUP)

    gl = lg[0:ROUTE_GROUP_ROWS]
    ge = jnp.exp(gl - cmax(gl))
    gsum = csum(ge)
    g_gate = 1.0 / gsum
    g_idx = cmin(jnp.where(ge / gsum == g_gate, sub, none))
    el = lg[ROUTE_EXPERT_ROW0:ROUTE_EXPERT_ROW0 + EXPERTS_PER_GROUP]
    for g in range(1, N_GROUPS):
        r0 = ROUTE_EXPERT_ROW0 + g * EXPERTS_PER_GROUP
        el = jnp.where(g_idx == float(g), lg[r0:r0 + EXPERTS_PER_GROUP], el)
    ee = jnp.exp(el - cmax(el))
    esum = csum(ee)
    eprob = ee / esum
    p1 = 1.0 / esum
    i1 = cmin(jnp.where(eprob == p1, sub, none))
    rest = jnp.where(sub == i1, -1.0, eprob)
    p2 = cmax(rest)
    i2 = cmin(jnp.where(rest == p2, sub, none))
    den = p1 + p2
    e1 = g_idx * EXPERTS_PER_GROUP + i1
    e2 = g_idx * EXPERTS_PER_GROUP + i2
    erow = lax.broadcasted_iota(jnp.int32, (N_EXPERTS, TM), 0).astype(F32)
    onehot = ((erow == e1) | (erow == e2)).astype(F32)
    before = jnp.dot(onehot.astype(BF16), tri_ref[...], preferred_element_type=F32) + carry_scr[:, 0:1]
    rank1 = csum(jnp.where(erow == e1, before, 0.0))
    rank2 = csum(jnp.where(erow == e2, before, 0.0))
    carry_scr[...] = carry_scr[...] + jnp.sum(onehot, axis=1, keepdims=True)
    cnt_ref[...] = carry_scr[...]

    record = jnp.zeros((ROUTE_ROWS, TM), F32)
    for i, val in ((R_W1, g_gate * (p1 / den)), (R_W2, g_gate * (p2 / den)), (R_E1, e1), (R_E2, e2),
                   (R_RANK1, rank1), (R_RANK2, rank2)):
        record = jnp.where(sub == float(i), val, record)
    route_ref[0] = record


def _outproj(oa, om, x, b0, wo, g, wrh, wrl, br, tri):
    B = oa.shape[0]
    rows = lambda c: pl.BlockSpec((1, TM, c), lambda b, j: (b, j, 0))
    shp = lambda c, dt: jax.ShapeDtypeStruct((B, SEQ, c), dt)
    return pl.pallas_call(
        _outproj_kernel,
        grid=(B, N_ROW_STEPS),
        in_specs=[pl.BlockSpec((1, N_RES, M_STEP, A_WIDTH), lambda b, j: (b, 0, j, 0)),
                  rows(M_WIDTH), pl.BlockSpec((1, TM, D_MODEL), lambda b, j: (b + b0, j, 0)),
                  _full(wo), _full(g), _full(wrh), _full(wrl), _full(br), _full(tri)],
        out_specs=[rows(D_MODEL), rows(D_MODEL),
                   pl.BlockSpec((1, ROUTE_ROWS, TM), lambda b, j: (b * N_ROW_STEPS + j, 0, 0)),
                   pl.BlockSpec((N_EXPERTS, LANES), lambda b, j: (0, 0))],
        out_shape=[shp(D_MODEL, F32), shp(D_MODEL, BF16),
                   jax.ShapeDtypeStruct((B * N_ROW_STEPS, ROUTE_ROWS, TM), F32),
                   jax.ShapeDtypeStruct((N_EXPERTS, LANES), F32)],
        scratch_shapes=[pltpu.VMEM((A_WIDTH // LANES, TM, LANES), F32), pltpu.VMEM((N_EXPERTS, LANES), F32)],
        compiler_params=_cparams(("arbitrary", "arbitrary")),
        name="outproj_router",
    )(oa, om, x, wo, g, wrh, wrl, br, tri)


def _expert_kernel(blk_e_ref, n_used_ref, x_ref, wg_ref, wu_ref, wd_ref, y_ref, wgu_s, wd_s):
    i = pl.program_id(0)

    @pl.when(i < n_used_ref[0])
    def _():
        @pl.when((i == 0) | (blk_e_ref[i] != blk_e_ref[jnp.maximum(i - 1, 0)]))
        def _():
            wgu_s[:, 0:EXPERT_FF] = wg_ref[0].astype(BF16)
            wgu_s[:, EXPERT_FF:] = wu_ref[0].astype(BF16)
            wd_s[...] = wd_ref[0].astype(BF16)

        half = MOE_TM // SUB_TILES
        gate_up = []
        for s in range(SUB_TILES):
            gu = jnp.dot(x_ref[s * half:(s + 1) * half, :], wgu_s[...], preferred_element_type=F32)
            gate_up.append((gu[:, 0:EXPERT_FF], gu[:, EXPERT_FF:]))
        for s, (gate, up) in enumerate(gate_up):
            hdn = (gate * jax.nn.sigmoid(gate) * up).astype(BF16)
            y_ref[s * half:(s + 1) * half, :] = jnp.dot(hdn, wd_s[...], preferred_element_type=F32).astype(y_ref.dtype)

    @pl.when(i >= n_used_ref[0])
    def _():
        y_ref[...] = jnp.zeros(y_ref.shape, y_ref.dtype)


def _experts(blk_e, n_used, xs, wg, wu, wd):
    n_blocks = xs.shape[0] // MOE_TM
    row_in = lambda i, be, nu: (jnp.minimum(i, nu[0] - 1), 0)
    wsel = lambda i, be, nu: (be[i], 0, 0)
    return pl.pallas_call(
        _expert_kernel,
        grid_spec=pltpu.PrefetchScalarGridSpec(
            num_scalar_prefetch=2,
            grid=(n_blocks,),
            in_specs=[pl.BlockSpec((MOE_TM, D_MODEL), row_in),
                      pl.BlockSpec((1, D_MODEL, EXPERT_FF), wsel),
                      pl.BlockSpec((1, D_MODEL, EXPERT_FF), wsel),
                      pl.BlockSpec((1, EXPERT_FF, D_MODEL), wsel)],
            out_specs=pl.BlockSpec((MOE_TM, D_MODEL), lambda i, be, nu: (i, 0)),
            scratch_shapes=[pltpu.VMEM((D_MODEL, 2 * EXPERT_FF), BF16), pltpu.VMEM((EXPERT_FF, D_MODEL), BF16)]),
        out_shape=jax.ShapeDtypeStruct(xs.shape, BF16),
        compiler_params=_cparams(("arbitrary",)),
        name="expert_ffn",
    )(blk_e, n_used, xs, wg, wu, wd)


def _ple_kernel(n_parts, part_b, *refs):
    parts = [refs[4 * i:4 * i + 4] for i in range(n_parts)]
    p_ref, wp_ref, gp_ref, wg_ref, bg_ref, o_ref = refs[4 * n_parts:]

    def combine_and_gate(h_ref, y1_ref, y2_ref, route_ref):
        rec = jnp.concatenate([route_ref[0], jnp.zeros((LANES - ROUTE_ROWS, TM), F32)], axis=0)
        route = jnp.concatenate([rec[:, c * LANES:(c + 1) * LANES].T for c in range(TM // LANES)], axis=0)
        h = (h_ref[0] + route[:, R_W1:R_W1 + 1] * y1_ref[0].astype(F32)
             + route[:, R_W2:R_W2 + 1] * y2_ref[0].astype(F32))
        e = _rms(jnp.dot(p_ref[0].astype(BF16), wp_ref[...], preferred_element_type=F32), D_MODEL) * gp_ref[...]
        g = jax.nn.sigmoid(jnp.dot(h.astype(BF16), wg_ref[...], preferred_element_type=F32) + bg_ref[...])
        o_ref[0] = h + g * e

    for i, part in enumerate(parts):
        pl.when(pl.program_id(0) // part_b == i)(functools.partial(combine_and_gate, *part))


def _ple(parts, p, wp, gp, wg, bg):
    n_parts, part_b = len(parts), parts[0][0].shape[0]
    rows = lambda c: pl.BlockSpec((1, TM, c), lambda b, j: (b, j, 0))
    in_specs, args = [], []
    for i, (h, y_tok, route) in enumerate(parts):
        local = lambda b, i=i: jnp.clip(b - i * part_b, 0, part_b - 1)
        step = lambda b, j, local=local: local(b) * N_ROW_STEPS + j
        in_specs += [pl.BlockSpec((1, TM, D_MODEL), lambda b, j, local=local: (local(b), j, 0)),
                     pl.BlockSpec((1, TM, D_MODEL), lambda b, j, step=step: (0, step(b, j), 0)),
                     pl.BlockSpec((1, TM, D_MODEL), lambda b, j, step=step: (1, step(b, j), 0)),
                     pl.BlockSpec((1, ROUTE_ROWS, TM), lambda b, j, step=step: (step(b, j), 0, 0))]
        args += [h, y_tok, y_tok, route]
    return pl.pallas_call(
        functools.partial(_ple_kernel, n_parts, part_b),
        grid=(n_parts * part_b, N_ROW_STEPS),
        in_specs=in_specs + [rows(PLE_DIM), _full(wp), _full(gp), _full(wg), _full(bg)],
        out_specs=rows(D_MODEL),
        out_shape=jax.ShapeDtypeStruct((n_parts * part_b, SEQ, D_MODEL), F32),
        compiler_params=_cparams(("parallel", "parallel")),
        name="ple_gate",
    )(*args, p, wp, gp, wg, bg)


def _t5_bucket(dist):
    max_exact = REL_BUCKETS // 2
    n = jnp.maximum(dist, 0)
    nf = jnp.maximum(n, 1).astype(F32)
    large = max_exact + (jnp.log(nf / max_exact) / math.log(REL_MAX_DISTANCE / max_exact)
                         * (REL_BUCKETS - max_exact)).astype(jnp.int32)
    large = jnp.minimum(large, REL_BUCKETS - 1)
    return jnp.where(n < max_exact, n, large)


def _bias_table(rel_bias, local_index, dilation, with_prev):
    loc = np.asarray(local_index)
    delta = loc[:, None] - loc[None, :]
    if with_prev:
        delta = np.concatenate([delta + BAND_BLOCK, delta], axis=1)
    ok = (delta >= 0) & (delta <= BAND_BLOCK)
    bucket = _t5_bucket(jnp.asarray(delta * dilation, jnp.int32))
    picked = jnp.where(bucket[None, :, :, None] == jnp.arange(REL_BUCKETS, dtype=jnp.int32),
                       rel_bias.astype(F32).T[:, None, None, :], 0.0)
    return jnp.where(jnp.asarray(ok)[None], jnp.sum(picked, axis=-1) * LOG2E, NEG)


def _block_diag_ones(sizes, total):
    g = np.zeros((total, total), np.float32)
    o = 0
    for s, on in sizes:
        if on:
            g[o:o + s, o:o + s] = 1.0
        o += s
    return jnp.asarray(g, BF16)


def _dispatch_plan(route, counts, n_tokens):
    field = lambda i: route[:, i, :].reshape(n_tokens).astype(jnp.int32)
    e = jnp.stack([field(R_E1), field(R_E2)], axis=-1)
    rank = jnp.stack([field(R_RANK1), field(R_RANK2)], axis=-1)
    pcounts = (counts + MOE_TM - 1) // MOE_TM * MOE_TM
    pend = jnp.cumsum(pcounts)
    pstart = pend - pcounts
    ids = jnp.arange(N_EXPERTS, dtype=jnp.int32)
    pos = rank + jnp.sum(jnp.where(e[..., None] == ids, pstart, 0), axis=-1)
    n_assign = n_tokens * TOP_K
    shift = int(math.ceil(math.log2(n_assign)))
    keys = (e.reshape(-1) << shift) | jnp.arange(n_assign, dtype=jnp.int32)
    tok_sorted = (jnp.sort(keys) & ((1 << shift) - 1)) // TOP_K
    n_rows = n_assign + N_EXPERTS * MOE_TM
    rows = jnp.arange(n_rows, dtype=jnp.int32)
    row_e = jnp.sum(rows[:, None] >= pend[None, :], axis=-1)
    row_e = jnp.minimum(row_e, N_EXPERTS - 1)
    pick = lambda tbl: jnp.sum(jnp.where(row_e[:, None] == ids, tbl, 0), axis=-1)
    within = rows - pick(pstart)
    src = jnp.clip(pick(jnp.cumsum(counts) - counts) + within, 0, n_assign - 1)
    row_tok = jnp.where(within < pick(counts), tok_sorted[src], rows % n_tokens)
    n_blocks = n_rows // MOE_TM
    n_used = (pend[-1] // MOE_TM).astype(jnp.int32)
    blk_start = jnp.minimum(jnp.arange(n_blocks, dtype=jnp.int32), n_used - 1) * MOE_TM
    blk_e = jnp.minimum(jnp.sum(blk_start[:, None] >= pend[None, :], axis=-1), N_EXPERTS - 1).astype(jnp.int32)
    return row_tok, pos, blk_e, n_used.reshape(1)


def kernel(x, p, rel_bias, norm_mix_gain, w_in, qn_a_gain, kn_a_gain, q_a_gain, w_q_up, kv_a_gain, w_kv_up, qn_nope_gain, qn_rope_gain, kn_nope_gain, kn_rope_gain, w_out, norm_ffn_gain, w_router_group, b_router_group, w_router_expert, b_router_expert, w_exp_gate, w_exp_up, w_exp_down, w_ple_proj, ple_norm_gain, w_ple_gate, b_ple_gate):
    B, S, D = x.shape
    assert (S, D) == (SEQ, D_MODEL) and p.shape[0] == 1
    row = lambda a: a.reshape(1, -1).astype(F32)
    zeros = lambda *s: jnp.zeros(s, F32)

    w_in_p = jnp.concatenate([w_in[0, :, :C_KR0], zeros(D, 64), w_in[0, :, C_KR0:], zeros(D, 32)],
                             axis=1).astype(BF16)
    gsum_a = _block_diag_ones([(HEAD_DIM_A, True)] * N_HEADS_A, A_WIDTH)
    gq_a = row(jnp.tile(qn_a_gain[0], N_HEADS_A)) * (HEAD_DIM_A ** -0.5 * LOG2E)
    gk_a = row(jnp.tile(kn_a_gain[0], N_HEADS_A))

    wq_p = jnp.pad(w_q_up[0].reshape(Q_LORA, N_HEADS_M, NOPE_DIM + ROPE_DIM),
                   ((0, 0), (0, 0), (0, HEAD_SLOT - NOPE_DIM - ROPE_DIM))).reshape(Q_LORA, -1).astype(BF16)
    wkv = w_kv_up[0].reshape(KV_LORA, N_HEADS_M, NOPE_DIM + V_DIM)
    wk_p = jnp.pad(wkv[..., :NOPE_DIM], ((0, 0), (0, 0), (0, HEAD_SLOT - NOPE_DIM))).reshape(KV_LORA, -1)
    wkv_p = jnp.concatenate([wk_p, wkv[..., NOPE_DIM:].reshape(KV_LORA, -1)], axis=1).astype(BF16)
    gs_q = _block_diag_ones([(NOPE_DIM, True), (ROPE_DIM, True), (32, False)], HEAD_SLOT)
    gs_k = _block_diag_ones([(NOPE_DIM, True), (64, False)], HEAD_SLOT)
    inv_cnt_q = jnp.asarray(np.concatenate([np.full(64, 1 / NOPE_DIM), np.full(32, 1 / ROPE_DIM),
                                            np.ones(32)]).astype(np.float32)).reshape(1, HEAD_SLOT)
    mla_scale = (NOPE_DIM + ROPE_DIM) ** -0.5 * LOG2E
    gq_m = row(jnp.concatenate([qn_nope_gain[0], qn_rope_gain[0], zeros(32)])) * mla_scale
    gk_m = row(jnp.concatenate([kn_nope_gain[0], zeros(64)]))
    gkr_m = row(jnp.concatenate([zeros(64), kn_rope_gain[0], zeros(32)]))

    half = ROPE_DIM // 2
    inv = 1.0 / (ROPE_THETA ** (jnp.arange(half, dtype=F32) * 2.0 / ROPE_DIM))
    ang = jnp.arange(S, dtype=jnp.int32).astype(F32)[:, None] * inv[None, :]
    cosv, sinv = jnp.cos(ang), jnp.sin(ang)
    cos_t = jnp.concatenate([jnp.ones((S, 64), F32), cosv, cosv, jnp.ones((S, 32), F32)], -1)
    sin_a = jnp.concatenate([zeros(S, 80), sinv, zeros(S, 32)], -1)
    sin_b = jnp.concatenate([zeros(S, 64), -sinv, zeros(S, 48)], -1)

    loc1 = [16 * a + r for r in range(16) for a in range(8)]
    loc2 = [4 * a + c for c in range(4) for a in range(32)]
    loc3 = list(range(BAND_BLOCK))
    t1 = _bias_table(rel_bias, loc1, 1, True)
    t2 = _bias_table(rel_bias, loc2, 4, True)
    t3 = jnp.concatenate([jnp.full((N_HEADS_A, BAND_BLOCK, BAND_BLOCK), NEG, F32),
                          _bias_table(rel_bias, loc3, 16, False)], axis=-1)

    n_pad_g, n_pad_e = ROUTE_GROUP_ROWS - N_GROUPS, LANES - ROUTE_EXPERT_ROW0 - N_EXPERTS
    w_r = jnp.concatenate([w_router_group[0].T, zeros(n_pad_g, D), w_router_expert[0].T, zeros(n_pad_e, D)], axis=0)
    w_r_hi = w_r.astype(BF16)
    w_r_lo = (w_r - w_r_hi.astype(F32)).astype(BF16)
    b_r = jnp.concatenate([b_router_group[0], jnp.full((n_pad_g,), NEG, F32), b_router_expert[0],
                           zeros(n_pad_e)]).reshape(LANES, 1)
    tri = jnp.asarray(np.triu(np.ones((TM, TM), np.float32), 1), BF16)

    w_out_b = w_out[0].astype(BF16)
    part_b = B // BATCH_PARTS
    n_tokens = part_b * S
    parts = []
    for b0 in range(0, B, part_b):
        qa, ka, va, cq, ckv, kr = _inproj(x, b0, part_b, row(norm_mix_gain[0]), w_in_p, gsum_a, gq_a, gk_a,
                                          row(q_a_gain[0]), row(kv_a_gain[0]))
        o_a = _dilated(qa, ka, va, t1, t2, t3)
        q_m, k_m, v_m = _mla_prep(cq, ckv, kr, wq_p, wkv_p, gs_q, gs_k, inv_cnt_q, gq_m, gk_m, gkr_m,
                                  cos_t, sin_a, sin_b)
        o_m = _mla_attn(q_m, k_m, v_m)
        h1, xn2, route, cnt = _outproj(o_a, o_m, x, b0, w_out_b, row(norm_ffn_gain[0]),
                                       w_r_hi, w_r_lo, b_r, tri)
        row_tok, pos, blk_e, n_used = _dispatch_plan(route, cnt[:, 0].astype(jnp.int32), n_tokens)
        xs = xn2.reshape(n_tokens, D)[row_tok]
        y = _experts(blk_e, n_used, xs, w_exp_gate[0], w_exp_up[0], w_exp_down[0])
        y_tok = y[pos.T.reshape(-1)].reshape(TOP_K, n_tokens, D)
        parts.append((h1, y_tok, route))
    return _ple(parts, p[0], w_ple_proj[0].astype(BF16), row(ple_norm_gain[0]),
                w_ple_gate[0].astype(BF16), row(b_ple_gate[0]))
```
